```python
import math
import jax, jax.numpy as jnp
from jax import lax
import numpy as np

D_MODEL = 2048
BATCH = 4
SEQ = 2048
DEPTH = 1
DEC_BATCH = 32
DEC_SEQ = 4
PAST_LEN = 16384
PAGE_SIZE = 128

MIX_WIDTH = D_MODEL
ATTN_WIDTH = MIX_WIDTH // 2
HG_WIDTH = MIX_WIDTH - ATTN_WIDTH
QK_DIM = 64
V_DIM = 2 * QK_DIM
ATTN_HEADS = ATTN_WIDTH // V_DIM
ATTN_KV_HEADS = ATTN_HEADS // 2
ATTN_GROUP = ATTN_HEADS // ATTN_KV_HEADS
HG_EXPAND = 128
HG_V_DIM = 128
HG_HEADS = HG_WIDTH // HG_V_DIM
HG_F_DIM = HG_HEADS * HG_EXPAND
Q_COLS = ATTN_HEADS * 2 * QK_DIM
K_COLS = ATTN_KV_HEADS * 2 * QK_DIM
V_COLS = ATTN_KV_HEADS * V_DIM
IN_COLS = Q_COLS + K_COLS + V_COLS + 2 * HG_F_DIM + 2 * HG_WIDTH
Q_BLOCK = 128
HG_CHUNK = 64
N_GROUPS = 4
EXPERTS_PER_GROUP = 4
N_EXPERTS = N_GROUPS * EXPERTS_PER_GROUP
TOP_K_INNER = 2
D_FF_EXPERT = D_MODEL // 4
NORM_EPS = 1e-6
ALIBI_MAX_BIAS = 8.0

kernel_name = "hymba_diffattn_hgrn2_hmoe_step"


def _rmsnorm(x, g):
    xf = x.astype(jnp.float32)
    y = xf * lax.rsqrt(jnp.mean(xf * xf, axis=-1, keepdims=True) + NORM_EPS)
    return (y * g.astype(jnp.float32)).astype(x.dtype)


def _alibi_slopes():
    h = jnp.arange(1, ATTN_HEADS + 1, dtype=jnp.float32)
    return jnp.exp2(-ALIBI_MAX_BIAS * h / ATTN_HEADS).reshape(ATTN_KV_HEADS, ATTN_GROUP)


def _adaln(c, w_ada, b_ada):
    mod = jax.nn.silu(c) @ w_ada + b_ada
    return [m[:, None, :] for m in jnp.split(mod, 6, axis=-1)]


def _project(h, w_in, q_norm_g, k_norm_g, lb):
    B, T, _ = h.shape
    p = h @ w_in
    cuts = np.cumsum([Q_COLS, K_COLS, V_COLS, HG_F_DIM, HG_F_DIM, HG_WIDTH]).tolist()
    q, k, v, hq, hf, hi, hg = jnp.split(p, cuts, axis=-1)
    q = _rmsnorm(q.reshape(B, T, ATTN_KV_HEADS, ATTN_GROUP, 2, QK_DIM), q_norm_g)
    k = _rmsnorm(k.reshape(B, T, ATTN_KV_HEADS, 2, QK_DIM), k_norm_g)
    v = v.reshape(B, T, ATTN_KV_HEADS, V_DIM)
    lb = lb.reshape(HG_HEADS, HG_EXPAND)
    z = hf.reshape(B, T, HG_HEADS, HG_EXPAND).astype(jnp.float32)
    log_f = jnp.log(lb + (1.0 - lb) * jax.nn.sigmoid(z))
    hk = (1.0 - lb) * jax.nn.sigmoid(-z)
    hq = jax.nn.silu(hq.reshape(B, T, HG_HEADS, HG_EXPAND).astype(jnp.float32))
    hi = hi.reshape(B, T, HG_HEADS, HG_V_DIM)
    hg = hg.reshape(B, T, HG_HEADS, HG_V_DIM)
    return q, k, v, hq, hk, hi, log_f, hg


def _attn_logits(q, k, q_pos, k_pos, slopes):
    s = jnp.einsum('bqkgmd,bskmd->bkgmqs', q, k, preferred_element_type=jnp.float32) * (QK_DIM ** -0.5)
    dist = (q_pos[:, None] - k_pos[None, :]).astype(jnp.float32)
    s = s - slopes[None, :, :, None, None, None] * dist
    return jnp.where(dist >= 0, s, -jnp.inf)


def _diff_combine(logits, values, lam):
    sizes = [l.shape[-1] for l in logits]
    p = jax.nn.softmax(jnp.concatenate(logits, axis=-1), axis=-1)
    d = p[:, :, :, 0] - lam * p[:, :, :, 1]
    parts = [d] if len(sizes) == 1 else jnp.split(d, np.cumsum(sizes)[:-1].tolist(), axis=-1)
    out = jnp.einsum('bkgqs,bskv->bqkgv', parts[0], values[0])
    for dp, vv in zip(parts[1:], values[1:]):
        out = out + jnp.einsum('bkgqs,bskv->bqkgv', dp, vv)
    return out


def _prompt_attention(q, k, v, slopes, lam):
    B, S = q.shape[:2]
    nb = S // Q_BLOCK
    qb = q.reshape(B, nb, Q_BLOCK, ATTN_KV_HEADS, ATTN_GROUP, 2, QK_DIM).swapaxes(0, 1)
    pos = jnp.arange(S)
    qpos = pos.reshape(nb, Q_BLOCK)

    def blk(args):
        qi, qp = args
        return _diff_combine([_attn_logits(qi, k, qp, pos, slopes)], [v], lam)

    o = lax.map(blk, (qb, qpos))
    return o.swapaxes(0, 1).reshape(B, S, ATTN_KV_HEADS, ATTN_GROUP, V_DIM)


def _sample_attention(q, k, v, cache_k, cache_v, page_table, layer, slopes, lam):
    B, T = q.shape[:2]
    past = page_table.shape[1] * cache_k.shape[2]
    kp = cache_k[layer, page_table].reshape(B, past, ATTN_KV_HEADS, 2, QK_DIM)
    vp = cache_v[layer, page_table].reshape(B, past, ATTN_KV_HEADS, V_DIM)
    past_pos = jnp.arange(past)
    q_pos = past + jnp.arange(T)
    logits = [_attn_logits(q, kp, q_pos, past_pos, slopes), _attn_logits(q, k, q_pos, q_pos, slopes)]
    return _diff_combine(logits, [vp, v], lam)


def _attn_finish(o, subln_g, lam_init):
    B, T = o.shape[:2]
    o = _rmsnorm(o, subln_g) * (1.0 - lam_init)
    return o.reshape(B, T, ATTN_WIDTH)


def _hgrn2(q, k, v, log_f, s0):
    B, T, H, K = q.shape
    V = v.shape[-1]
    c = min(HG_CHUNK, T)
    n = -(-T // c)
    pad = n * c - T

    def chunks(a):
        a = jnp.pad(a.astype(jnp.float32), ((0, 0), (0, pad), (0, 0), (0, 0)))
        return a.reshape(B, n, c, H, a.shape[-1]).transpose(1, 0, 3, 2, 4)

    tri = jnp.tril(jnp.ones((c, c), dtype=bool))

    def step(S, inp):
        qc, kc, vc, gc = inp
        G = jnp.cumsum(gc, axis=2)
        inter = jnp.einsum('bhtk,bhkv->bhtv', qc * jnp.exp(G), S)
        rel = jnp.where(tri[None, None, :, :, None], G[:, :, :, None, :] - G[:, :, None, :, :], -jnp.inf)
        a = jnp.einsum('bhtk,bhsk,bhtsk->bhts', qc, kc, jnp.exp(rel))
        intra = jnp.einsum('bhts,bhsv->bhtv', a, vc)
        g_end = G[:, :, -1, :]
        S = jnp.exp(g_end)[..., None] * S + jnp.einsum('bhsk,bhsv->bhkv', kc * jnp.exp(g_end[:, :, None, :] - G), vc)
        return S, inter + intra

    S, o = lax.scan(step, s0.astype(jnp.float32), (chunks(q), chunks(k), chunks(v), chunks(log_f)))
    o = o.transpose(1, 0, 3, 2, 4).reshape(B, n * c, H, V)[:, :T]
    return o, S


def _hgrn2_finish(o, g, hg_norm_g):
    B, T = o.shape[:2]
    o = _rmsnorm(o, hg_norm_g) * jax.nn.silu(g.astype(jnp.float32))
    return o.reshape(B, T, HG_WIDTH)


def _hier_moe(h, w_rg, b_rg, w_re, b_re, w_gate, w_up, w_down):
    B, T, D = h.shape
    t = h.reshape(B * T, D)
    pg = jax.nn.softmax((t @ w_rg + b_rg).astype(jnp.float32), axis=-1)
    pg_top, g_idx = lax.top_k(pg, 1)
    le = jnp.einsum('nd,gde->nge', t, w_re) + b_re[None]
    le_sel = jnp.einsum('nge,ng->ne', le.astype(jnp.float32), jax.nn.one_hot(g_idx[:, 0], N_GROUPS, dtype=jnp.float32))
    pe_top, e_idx = lax.top_k(jax.nn.softmax(le_sel, axis=-1), TOP_K_INNER)
    pe_top = pe_top / jnp.sum(pe_top, axis=-1, keepdims=True)
    expert_id = g_idx * EXPERTS_PER_GROUP + e_idx
    combine = jnp.sum(jax.nn.one_hot(expert_id, N_EXPERTS, dtype=jnp.float32) * (pg_top * pe_top)[..., None], axis=1)
    a = jnp.einsum('nd,edf->nef', t, w_gate)
    u = jnp.einsum('nd,edf->nef', t, w_up)
    hid = jax.nn.silu(a) * u * combine[:, :, None]
    y = jnp.einsum('nef,efd->nd', hid, w_down)
    return y.reshape(B, T, D).astype(h.dtype)


def _layer(x, c, s0, attend, lb, lw):
    (n1, n2, w_ada, b_ada, w_in, qg, kg, subln_g, lam, lam_init, hg_g, w_out,
     w_rg, b_rg, w_re, b_re, w_gate, w_up, w_down) = lw
    sh1, sc1, ga1, sh2, sc2, ga2 = _adaln(c, w_ada, b_ada)
    h = _rmsnorm(x, n1) * (1.0 + sc1) + sh1
    q, k, v, hq, hk, hi, log_f, hg = _project(h, w_in, qg, kg, lb)
    a = _attn_finish(attend(q, k, v, lam), subln_g, lam_init)
    o_hg, s_new = _hgrn2(hq, hk, hi, log_f, s0)
    m = _hgrn2_finish(o_hg, hg, hg_g)
    mix = jnp.concatenate([a.astype(x.dtype), m.astype(x.dtype)], axis=-1) @ w_out
    x = x + ga1 * mix
    h2 = _rmsnorm(x, n2) * (1.0 + sc2) + sh2
    x = x + ga2 * _hier_moe(h2, w_rg, b_rg, w_re, b_re, w_gate, w_up, w_down)
    B, T = k.shape[:2]
    return x, k.reshape(B, T, ATTN_KV_HEADS, 2 * QK_DIM), v, s_new.astype(s0.dtype)


def setup_inputs(seed: int = 0) -> dict:
    key = jax.random.key(seed)
    ks = jax.random.split(key, 40)
    f32 = jnp.float32
    n_pages = PAST_LEN // PAGE_SIZE
    n_pool = (5 * DEC_BATCH * n_pages + 3) // 4

    def nrm(k, shape, scale):
        return jax.random.normal(k, shape, f32) * scale

    def gain(k, shape):
        return 1.0 + 0.01 * jax.random.normal(k, shape, f32)

    page_table = jax.random.permutation(ks[5], n_pool)[:DEC_BATCH * n_pages].reshape(DEC_BATCH, n_pages).astype(jnp.int32)
    return {
        "x_prompt": nrm(ks[0], (BATCH, SEQ, D_MODEL), 1.0),
        "x_sample": nrm(ks[1], (DEC_BATCH, DEC_SEQ, D_MODEL), 1.0),
        "cache_k": nrm(ks[2], (DEPTH, n_pool, PAGE_SIZE, ATTN_KV_HEADS, 2 * QK_DIM), 1.0),
        "cache_v": nrm(ks[3], (DEPTH, n_pool, PAGE_SIZE, ATTN_KV_HEADS, V_DIM), 1.0),
        "state_hgrn": nrm(ks[4], (DEPTH, DEC_BATCH, HG_HEADS, HG_EXPAND, HG_V_DIM), 1.0),
        "page_table": page_table,
        "c_prompt": nrm(ks[6], (BATCH, D_MODEL), 1.0),
        "c_sample": nrm(ks[7], (DEC_BATCH, D_MODEL), 1.0),
        "norm1_g": gain(ks[8], (DEPTH, D_MODEL)),
        "norm2_g": gain(ks[9], (DEPTH, D_MODEL)),
        "w_ada": nrm(ks[10], (DEPTH, D_MODEL, 6 * D_MODEL), 0.5 * D_MODEL ** -0.5),
        "b_ada": nrm(ks[11], (DEPTH, 6 * D_MODEL), 0.01),
        "w_in": nrm(ks[12], (DEPTH, D_MODEL, IN_COLS), D_MODEL ** -0.5),
        "q_norm_g": gain(ks[13], (DEPTH, QK_DIM)),
        "k_norm_g": gain(ks[14], (DEPTH, QK_DIM)),
        "lambda_q1": nrm(ks[15], (DEPTH, QK_DIM), 0.1),
        "lambda_k1": nrm(ks[16], (DEPTH, QK_DIM), 0.1),
        "lambda_q2": nrm(ks[17], (DEPTH, QK_DIM), 0.1),
        "lambda_k2": nrm(ks[18], (DEPTH, QK_DIM), 0.1),
        "subln_g": gain(ks[19], (DEPTH, V_DIM)),
        "hg_lower_bound": nrm(ks[20], (DEPTH + 1, HG_F_DIM), 0.1),
        "hg_norm_g": gain(ks[21], (DEPTH, HG_V_DIM)),
        "w_out": nrm(ks[22], (DEPTH, MIX_WIDTH, D_MODEL), MIX_WIDTH ** -0.5),
        "w_router_group": nrm(ks[23], (DEPTH, D_MODEL, N_GROUPS), D_MODEL ** -0.5),
        "b_router_group": nrm(ks[24], (DEPTH, N_GROUPS), 0.01),
        "w_router_expert": nrm(ks[25], (DEPTH, N_GROUPS, D_MODEL, EXPERTS_PER_GROUP), D_MODEL ** -0.5),
        "b_router_expert": nrm(ks[26], (DEPTH, N_GROUPS, EXPERTS_PER_GROUP), 0.01),
        "w_exp_gate": nrm(ks[27], (DEPTH, N_EXPERTS, D_MODEL, D_FF_EXPERT), D_MODEL ** -0.5),
        "w_exp_up": nrm(ks[28], (DEPTH, N_EXPERTS, D_MODEL, D_FF_EXPERT), D_MODEL ** -0.5),
        "w_exp_down": nrm(ks[29], (DEPTH, N_EXPERTS, D_FF_EXPERT, D_MODEL), D_FF_EXPERT ** -0.5),
    }


def reference(x_prompt, x_sample, cache_k, cache_v, state_hgrn, page_table, c_prompt, c_sample,
              norm1_g, norm2_g, w_ada, b_ada, w_in, q_norm_g, k_norm_g,
              lambda_q1, lambda_k1, lambda_q2, lambda_k2, subln_g, hg_lower_bound, hg_norm_g, w_out,
              w_router_group, b_router_group, w_router_expert, b_router_expert,
              w_exp_gate, w_exp_up, w_exp_down):
    slopes = _alibi_slopes()
    lbs = jnp.cumsum(jax.nn.softmax(hg_lower_bound.astype(jnp.float32), axis=0), axis=0)
    yp, ys = x_prompt, x_sample
    kp_l, vp_l, ks_l, vs_l, sp_l, ss_l = [], [], [], [], [], []
    for l in range(DEPTH):
        lam_init = 0.8 - 0.6 * math.exp(-0.3 * l)
        lam = (jnp.exp(jnp.sum(lambda_q1[l].astype(jnp.float32) * lambda_k1[l].astype(jnp.float32)))
               - jnp.exp(jnp.sum(lambda_q2[l].astype(jnp.float32) * lambda_k2[l].astype(jnp.float32))) + lam_init)
        lw = (norm1_g[l], norm2_g[l], w_ada[l], b_ada[l], w_in[l], q_norm_g[l], k_norm_g[l], subln_g[l],
              lam, lam_init, hg_norm_g[l], w_out[l], w_router_group[l], b_router_group[l],
              w_router_expert[l], b_router_expert[l], w_exp_gate[l], w_exp_up[l], w_exp_down[l])
        s0_prompt = jnp.zeros((yp.shape[0], HG_HEADS, HG_EXPAND, HG_V_DIM), yp.dtype)
        yp, kp, vp, sp = _layer(yp, c_prompt, s0_prompt,
                                lambda q, k, v, lm: _prompt_attention(q, k, v, slopes, lm), lbs[l], lw)
        ys, kn, vn, sn = _layer(ys, c_sample, state_hgrn[l],
                                lambda q, k, v, lm: _sample_attention(q, k, v, cache_k, cache_v, page_table, l, slopes, lm),
                                lbs[l], lw)
        kp_l.append(kp); vp_l.append(vp); ks_l.append(kn); vs_l.append(vn); sp_l.append(sp); ss_l.append(sn)
    new_k_prompt = jnp.stack(kp_l, axis=0)
    new_v_prompt = jnp.stack(vp_l, axis=0)
    new_k_sample = jnp.stack(ks_l, axis=0)
    new_v_sample = jnp.stack(vs_l, axis=0)
    new_state_prompt = jnp.stack(sp_l, axis=0)
    new_state_sample = jnp.stack(ss_l, axis=0)
    return (yp, ys, new_k_prompt, new_v_prompt, new_k_sample, new_v_sample, new_state_prompt, new_state_sample)
```

```python
import functools
import math

import jax
import jax.numpy as jnp
import numpy as np
from jax import lax
from jax.experimental import pallas as pl
from jax.experimental.pallas import tpu as pltpu

F32 = jnp.float32
BF16 = jnp.bfloat16

QK_DIM = 64
V_DIM = 128
ATTN_KV_HEADS = 4
ATTN_GROUP = 2
ATTN_HEADS = ATTN_KV_HEADS * ATTN_GROUP
HG_HEADS = 8
HG_DIM = 128
N_GROUPS = 4
EXPERTS_PER_GROUP = 4
N_EXPERTS = N_GROUPS * EXPERTS_PER_GROUP
NORM_EPS = 1e-6
ALIBI_MAX_BIAS = 8.0
LANES = 128
MXU_DIM = 256
VMEM_LIMIT = 56 * 1024 * 1024
ROUTE_LANE0 = N_GROUPS


def _cparams(sem):
    return pltpu.CompilerParams(dimension_semantics=sem, vmem_limit_bytes=VMEM_LIMIT)


def _dot(a, b):
    return jnp.dot(a, b, preferred_element_type=F32)


def _dot_nt(a, b):
    return lax.dot_general(a, b, (((1,), (1,)), ((), ())), preferred_element_type=F32)


def _dot_tn(a, b):
    return lax.dot_general(a, b, (((0,), (0,)), ((), ())), preferred_element_type=F32)


def _split3(x):
    hi = x.astype(BF16)
    r = x - hi.astype(F32)
    mid = r.astype(BF16)
    lo = (r - mid.astype(F32)).astype(BF16)
    return hi, mid, lo


def _sigmoid(x):
    return 1.0 / (1.0 + jnp.exp(-x))


def _rms(x, g):
    return x * lax.rsqrt(jnp.mean(x * x, axis=-1, keepdims=True) + NORM_EPS) * g


def _ada_kernel(c_ref, w_ref, b_ref, o_ref):
    c = c_ref[...]
    s = c * _sigmoid(c)
    o_ref[...] = _dot(s.astype(BF16), w_ref[...].astype(BF16)) + b_ref[...]


def _adaln(c_all, w_ada, b_ada, tn=1024):
    r, d = c_all.shape
    n = w_ada.shape[1]
    return pl.pallas_call(
        _ada_kernel,
        grid=(n // tn,),
        in_specs=[
            pl.BlockSpec((r, d), lambda j: (0, 0)),
            pl.BlockSpec((d, tn), lambda j: (0, j)),
            pl.BlockSpec((1, tn), lambda j: (0, j)),
        ],
        out_specs=pl.BlockSpec((r, tn), lambda j: (0, j)),
        out_shape=jax.ShapeDtypeStruct((r, n), F32),
        compiler_params=_cparams(("arbitrary",)),
        name="adaln",
    )(c_all, w_ada, b_ada.reshape(1, n))


IN_TN = 512


def _group_sumsq(p, bd):
    x2 = p * p
    hi = x2.astype(BF16)
    lo = (x2 - hi.astype(F32)).astype(BF16)
    outs = []
    for c in range(p.shape[1] // MXU_DIM):
        sl = slice(c * MXU_DIM, (c + 1) * MXU_DIM)
        outs.append(_dot(hi[:, sl], bd) + _dot(lo[:, sl], bd))
    return jnp.concatenate(outs, axis=1)


def _in_kernel(x_ref, n1_ref, sc_ref, sh_ref, w_ref, gq_ref, gk_ref, lb_ref, bd_ref,
               q_ref, k_ref, v_ref, hm_ref, hk_ref, h_scr):
    j = pl.program_id(1)

    @pl.when(j == 0)
    def _():
        y = _rms(x_ref[...], n1_ref[...])
        h_scr[...] = (y * (1.0 + sc_ref[...]) + sh_ref[...]).astype(BF16)

    p = _dot(h_scr[...], w_ref[...].astype(BF16))

    def qk_norm(g):
        ms = _group_sumsq(p, bd_ref[...]) * (1.0 / QK_DIM)
        return p * lax.rsqrt(ms + NORM_EPS) * g

    @pl.when(j < 2)
    def _():
        q_ref[...] = (qk_norm(gq_ref[...]) * (QK_DIM ** -0.5)).astype(BF16)

    @pl.when(j == 2)
    def _():
        k_ref[...] = qk_norm(gk_ref[...])

    @pl.when(j == 3)
    def _():
        v_ref[...] = p

    @pl.when((j == 4) | (j == 5) | (j >= 10))
    def _():
        hm_ref[...] = p * _sigmoid(p)

    @pl.when((j == 6) | (j == 7))
    def _():
        a = lb_ref[...]
        e = jnp.exp(a - jnp.max(a, axis=0, keepdims=True))
        lb = e[0:1] / jnp.sum(e, axis=0, keepdims=True)
        hm_ref[...] = jnp.log(lb + (1.0 - lb) * _sigmoid(p))
        hk_ref[...] = (1.0 - lb) * _sigmoid(-p)

    @pl.when((j == 8) | (j == 9))
    def _():
        hm_ref[...] = p


def _in_proj(x2d, n1, sc, sh, w_in, gq, gk, hlb, tm, rows_per_mod):
    n, d = x2d.shape
    cols = w_in.shape[1]
    nj = cols // IN_TN
    assert nj == 12 and n % tm == 0
    r = sc.shape[1]
    tiles_per_mod = rows_per_mod // tm
    bd = jnp.asarray(np.kron(np.eye(MXU_DIM // QK_DIM), np.ones((QK_DIM, QK_DIM))), BF16)
    clamp = lambda j, lo: jnp.clip(j - lo, 0, 1)
    mod_spec = pl.BlockSpec((None, r, d), lambda i, j: (i // tiles_per_mod, 0, 0))
    return pl.pallas_call(
        _in_kernel,
        grid=(n // tm, nj),
        in_specs=[
            pl.BlockSpec((tm, d), lambda i, j: (i, 0)),
            pl.BlockSpec((1, d), lambda i, j: (0, 0)),
            mod_spec, mod_spec,
            pl.BlockSpec((d, IN_TN), lambda i, j: (0, j)),
            pl.BlockSpec((1, IN_TN), lambda i, j: (0, 0)),
            pl.BlockSpec((1, IN_TN), lambda i, j: (0, 0)),
            pl.BlockSpec((hlb.shape[0], IN_TN), lambda i, j: (0, clamp(j, 6))),
            pl.BlockSpec((MXU_DIM, MXU_DIM), lambda i, j: (0, 0)),
        ],
        out_specs=[
            pl.BlockSpec((tm, IN_TN), lambda i, j: (i, clamp(j, 0))),
            pl.BlockSpec((tm, IN_TN), lambda i, j: (i, 0)),
            pl.BlockSpec((tm, IN_TN), lambda i, j: (i, 0)),
            pl.BlockSpec((tm, IN_TN), lambda i, j: (i, jnp.clip(j - 4, 0, 7))),
            pl.BlockSpec((tm, IN_TN), lambda i, j: (i, clamp(j, 6))),
        ],
        out_shape=[
            jax.ShapeDtypeStruct((n, 2 * IN_TN), BF16),
            jax.ShapeDtypeStruct((n, IN_TN), F32),
            jax.ShapeDtypeStruct((n, IN_TN), F32),
            jax.ShapeDtypeStruct((n, 8 * IN_TN), F32),
            jax.ShapeDtypeStruct((n, 2 * IN_TN), F32),
        ],
        scratch_shapes=[pltpu.VMEM((tm, d), BF16)],
        compiler_params=_cparams(("arbitrary", "arbitrary")),
        name="in_proj",
    )(x2d, n1.reshape(1, d), sc, sh, w_in,
      jnp.tile(gq.reshape(1, QK_DIM), (1, IN_TN // QK_DIM)),
      jnp.tile(gk.reshape(1, QK_DIM), (1, IN_TN // QK_DIM)), hlb, bd)


def _alibi_slopes():
    h = np.arange(1, ATTN_HEADS + 1, dtype=np.float64)
    s = np.exp2(-ALIBI_MAX_BIAS * h / ATTN_HEADS).reshape(ATTN_KV_HEADS, 1, ATTN_GROUP)
    out = np.zeros((ATTN_KV_HEADS, 1, LANES), np.float32)
    out[:, :, :ATTN_GROUP] = s
    return jnp.asarray(out)


def _stack_queries(q, t):
    qf = q.astype(F32)
    lane = lax.broadcasted_iota(jnp.int32, (t, LANES), 1)
    parts = []
    for g in range(ATTN_GROUP):
        qg = qf[:, g * LANES:(g + 1) * LANES]
        parts.append(jnp.where(lane < QK_DIM, qg, 0.0))
        parts.append(jnp.where(lane >= QK_DIM, qg, 0.0))
    return jnp.concatenate(parts, axis=0).astype(BF16)


def _lambda(lam_ref, lam_init):
    l = lam_ref[...]
    s1 = jnp.sum(l[0:1] * l[1:2], axis=-1, keepdims=True)
    s2 = jnp.sum(l[2:3] * l[3:4], axis=-1, keepdims=True)
    return jnp.exp(s1) - jnp.exp(s2) + lam_init


def _diff_finish(o_all, t, lam, subln, lam_init, o_ref):
    for g in range(ATTN_GROUP):
        o = o_all[(2 * g) * t:(2 * g + 1) * t] - lam * o_all[(2 * g + 1) * t:(2 * g + 2) * t]
        y = _rms(o, subln) * (1.0 - lam_init)
        o_ref[:, g * LANES:(g + 1) * LANES] = y.astype(o_ref.dtype)


def _online_softmax_step(s, vblk, m_scr, l_scr, acc):
    m_old = m_scr[...]
    m_new = jnp.maximum(m_old, jnp.max(s, axis=-1, keepdims=True))
    alpha = jnp.exp(m_old - m_new)
    p = jnp.exp(s - m_new)
    l_scr[...] = alpha * l_scr[...] + jnp.sum(p, axis=-1, keepdims=True)
    acc[...] = alpha * acc[...] + _dot(p.astype(BF16), vblk)
    m_scr[...] = m_new


def _pattn_kernel(sl_ref, lam_ref, subln_ref, q_ref, k_ref, v_ref, o_ref, kb, vb, m_scr, l_scr, acc,
                  *, tq, tk, lam_init):
    qi = pl.program_id(2)

    @pl.when(qi == 0)
    def _():
        kb[...] = k_ref[...].astype(BF16)
        vb[...] = v_ref[...].astype(BF16)

    rows = 4 * tq
    row = lax.broadcasted_iota(jnp.int32, (rows, 1), 0)
    t_r = row % tq
    col = lax.broadcasted_iota(jnp.int32, (1, tk), 1)
    slope = jnp.where(row < 2 * tq, sl_ref[0:1, 0:1], sl_ref[0:1, 1:2])
    qs = _stack_queries(q_ref[...], tq)
    m_scr[...] = jnp.full(m_scr.shape, -jnp.inf, F32)
    l_scr[...] = jnp.zeros(l_scr.shape, F32)
    acc[...] = jnp.zeros(acc.shape, F32)
    q0 = qi * tq

    def body(kj, carry):
        k0 = pl.multiple_of(kj * tk, tk)
        s = _dot_nt(qs, kb[pl.ds(k0, tk), :])
        dist = (t_r - col) + (q0 - k0)
        s = jnp.where(dist >= 0, s - slope * dist.astype(F32), -jnp.inf)
        _online_softmax_step(s, vb[pl.ds(k0, tk), :], m_scr, l_scr, acc)
        return carry

    lax.fori_loop(0, (q0 + tq + tk - 1) // tk, body, 0)
    o_all = acc[...] * (1.0 / l_scr[...])
    _diff_finish(o_all, tq, _lambda(lam_ref, lam_init), subln_ref[...], lam_init, o_ref)


def _prompt_attention(q, k, v, lam4, subln, lam_init, tq=128, tk=512):
    b, t, _ = q.shape
    tk = min(tk, t)
    rows = 4 * tq
    kern = functools.partial(_pattn_kernel, tq=tq, tk=tk, lam_init=lam_init)
    return pl.pallas_call(
        kern,
        grid=(b, ATTN_KV_HEADS, t // tq),
        in_specs=[
            pl.BlockSpec((None, 1, LANES), lambda bi, h, qi: (h, 0, 0)),
            pl.BlockSpec((4, QK_DIM), lambda bi, h, qi: (0, 0)),
            pl.BlockSpec((1, V_DIM), lambda bi, h, qi: (0, 0)),
            pl.BlockSpec((None, tq, 2 * LANES), lambda bi, h, qi: (bi, qi, h)),
            pl.BlockSpec((None, t, LANES), lambda bi, h, qi: (bi, 0, h)),
            pl.BlockSpec((None, t, LANES), lambda bi, h, qi: (bi, 0, h)),
        ],
        out_specs=pl.BlockSpec((None, tq, 2 * LANES), lambda bi, h, qi: (bi, qi, h)),
        out_shape=jax.ShapeDtypeStruct((b, t, ATTN_HEADS * V_DIM), BF16),
        scratch_shapes=[
            pltpu.VMEM((t, LANES), BF16), pltpu.VMEM((t, LANES), BF16),
            pltpu.VMEM((rows, 1), F32), pltpu.VMEM((rows, 1), F32), pltpu.VMEM((rows, V_DIM), F32),
        ],
        compiler_params=_cparams(("arbitrary", "arbitrary", "arbitrary")),
        name="prompt_attention",
    )(_alibi_slopes(), lam4, subln.reshape(1, V_DIM), q, k, v)


def _sattn_kernel(pt_ref, sl_ref, lam_ref, subln_ref, q_ref, kn_ref, vn_ref, *rest,
                  npg, page, t, t_real, lam_init):
    kp_refs, vp_refs = rest[:npg], rest[npg:2 * npg]
    o_ref = rest[2 * npg]
    kc, vc, m_scr, l_scr, acc = rest[2 * npg + 1:]
    j = pl.program_id(1)
    nj = pl.num_programs(1)
    rows = 4 * t
    past = nj * npg * page

    @pl.when(j == 0)
    def _():
        m_scr[...] = jnp.full(m_scr.shape, -jnp.inf, F32)
        l_scr[...] = jnp.zeros(l_scr.shape, F32)
        acc[...] = jnp.zeros(acc.shape, F32)

    for p in range(npg):
        kc[p * page:(p + 1) * page, :] = kp_refs[p][...].reshape(page, ATTN_KV_HEADS * LANES).astype(BF16)
        vc[p * page:(p + 1) * page, :] = vp_refs[p][...].reshape(page, ATTN_KV_HEADS * LANES).astype(BF16)

    row = lax.broadcasted_iota(jnp.int32, (rows, 1), 0)
    t_r = row % t
    q = q_ref[...]
    ntok = npg * page
    col = lax.broadcasted_iota(jnp.int32, (1, ntok), 1)
    dist = ((past + t_r) - (j * ntok + col)).astype(F32)

    for h in range(ATTN_KV_HEADS):
        hs = slice(h * LANES, (h + 1) * LANES)
        slope = jnp.where(row < 2 * t, sl_ref[h, 0:1, 0:1], sl_ref[h, 0:1, 1:2])
        qs = _stack_queries(q[:, 2 * h * LANES:(2 * h + 2) * LANES], t)
        s = _dot_nt(qs, kc[:, hs]) - slope * dist
        _online_softmax_step(s, vc[:, hs], m_scr.at[h], l_scr.at[h], acc.at[h])

    @pl.when(j == nj - 1)
    def _():
        lam = _lambda(lam_ref, lam_init)
        coln = lax.broadcasted_iota(jnp.int32, (1, t), 1)
        dn = t_r - coln
        for h in range(ATTN_KV_HEADS):
            hs = slice(h * LANES, (h + 1) * LANES)
            slope = jnp.where(row < 2 * t, sl_ref[h, 0:1, 0:1], sl_ref[h, 0:1, 1:2])
            qs = _stack_queries(q[:, 2 * h * LANES:(2 * h + 2) * LANES], t)
            s = _dot_nt(qs, kn_ref[:, hs].astype(BF16))
            s = jnp.where((dn >= 0) & (coln < t_real), s - slope * dn.astype(F32), -jnp.inf)
            _online_softmax_step(s, vn_ref[:, hs].astype(BF16), m_scr.at[h], l_scr.at[h], acc.at[h])
            o_all = acc[h] * (1.0 / l_scr[h])
            _diff_finish(o_all, t, lam, subln_ref[...], lam_init, o_ref.at[:, 2 * h * LANES:(2 * h + 2) * LANES])


def _sample_attention(q, k_new, v_new, t_real, cache_k, cache_v, page_table, lam4, subln, lam_init, npg=8):
    b, t, _ = q.shape
    tpad = t
    page = cache_k.shape[1]
    n_pages = page_table.shape[1]
    npg = min(npg, n_pages)
    assert n_pages % npg == 0
    rows = 4 * t
    ntok = npg * page
    kern = functools.partial(_sattn_kernel, npg=npg, page=page, t=t, t_real=t_real, lam_init=lam_init)

    def page_spec(p):
        return pl.BlockSpec((None, page, ATTN_KV_HEADS, LANES),
                            lambda bi, j, pt: (pt[bi * n_pages + j * npg + p], 0, 0, 0))

    grid_spec = pltpu.PrefetchScalarGridSpec(
        num_scalar_prefetch=1,
        grid=(b, n_pages // npg),
        in_specs=[
            pl.BlockSpec((ATTN_KV_HEADS, 1, LANES), lambda bi, j, pt: (0, 0, 0)),
            pl.BlockSpec((4, QK_DIM), lambda bi, j, pt: (0, 0)),
            pl.BlockSpec((1, V_DIM), lambda bi, j, pt: (0, 0)),
            pl.BlockSpec((None, t, ATTN_HEADS * V_DIM), lambda bi, j, pt: (bi, 0, 0)),
            pl.BlockSpec((None, tpad, ATTN_KV_HEADS * LANES), lambda bi, j, pt: (bi, 0, 0)),
            pl.BlockSpec((None, tpad, ATTN_KV_HEADS * LANES), lambda bi, j, pt: (bi, 0, 0)),
        ] + [page_spec(p) for p in range(npg)] * 2,
        out_specs=pl.BlockSpec((None, t, ATTN_HEADS * V_DIM), lambda bi, j, pt: (bi, 0, 0)),
        scratch_shapes=[
            pltpu.VMEM((ntok, ATTN_KV_HEADS * LANES), BF16), pltpu.VMEM((ntok, ATTN_KV_HEADS * LANES), BF16),
            pltpu.VMEM((ATTN_KV_HEADS, rows, 1), F32), pltpu.VMEM((ATTN_KV_HEADS, rows, 1), F32),
            pltpu.VMEM((ATTN_KV_HEADS, rows, V_DIM), F32),
        ],
    )
    return pl.pallas_call(
        kern,
        grid_spec=grid_spec,
        out_shape=jax.ShapeDtypeStruct((b, t, ATTN_HEADS * V_DIM), BF16),
        compiler_params=_cparams(("arbitrary", "arbitrary")),
        name="sample_attention",
    )(page_table.reshape(-1), _alibi_slopes(), lam4, subln.reshape(1, V_DIM), q, k_new, v_new,
      *([cache_k] * npg), *([cache_v] * npg))


def _hgrn_tables(c):
    levels = int(math.log2(c))
    tril = np.tril(np.ones((c, c), np.float32))
    w = [tril]
    mask = [np.eye(c, dtype=np.float32)]
    idx = np.arange(c)
    for l in range(levels):
        bs = 2 << l
        mid = (idx // bs) * bs + bs // 2
        w.append(tril - tril[mid])
        upper = (idx % bs) >= bs // 2
        same = (idx[:, None] // bs) == (idx[None, :] // bs)
        mask.append((same & upper[:, None] & ~upper[None, :]).astype(np.float32))
    return jnp.asarray(np.concatenate(w, 0), BF16), jnp.asarray(np.stack(mask, 0)), levels


def _hgrn_kernel(*refs, c, levels, nchunk, has_s0):
    if has_s0:
        s0_ref, refs = refs[0], refs[1:]
    (w_ref, mask_ref, g_ref, q_ref, lf_ref, v_ref, gate_ref, k_ref, o_ref, sn_ref, st) = refs
    ci = pl.program_id(2)

    @pl.when(ci == 0)
    def _():
        if has_s0:
            st[...] = s0_ref[...].T
        else:
            st[...] = jnp.zeros(st.shape, F32)

    w = w_ref[...]
    for n in range(nchunk):
        rs = slice(n * c, (n + 1) * c)
        qs, kk, v = q_ref[rs, :], k_ref[rs, :], v_ref[rs, :]
        l_hi, l_mid, l_lo = _split3(lf_ref[rs, :])
        gx = _dot(w, l_hi) + _dot(w, l_mid) + _dot(w, l_lo)
        g = gx[0:c]
        a = mask_ref[0] * _dot_nt(qs.astype(BF16), kk.astype(BF16))
        for l in range(levels):
            x = gx[(l + 1) * c:(l + 2) * c]
            qa = qs * jnp.exp(jnp.minimum(x, 0.0))
            kb = kk * jnp.exp(jnp.minimum(-x, 0.0))
            a = a + mask_ref[l + 1] * _dot_nt(qa.astype(BF16), kb.astype(BF16))
        vb = v.astype(BF16)
        st_old = st[...]
        o = _dot_nt((qs * jnp.exp(g)).astype(BF16), st_old.astype(BF16)) + _dot(a.astype(BF16), vb)
        g_end = g[c - 1:c]
        kd = kk * jnp.exp(g_end - g)
        st[...] = st_old * jnp.exp(g_end) + _dot_tn(vb, kd.astype(BF16))
        o_ref[rs, :] = (_rms(o, g_ref[...]) * gate_ref[rs, :]).astype(o_ref.dtype)

    @pl.when(ci == pl.num_programs(2) - 1)
    def _():
        sn_ref[...] = st[...].T


def _hgrn2(hmain, hk, hg_norm_g, s0, c, tc):
    b, t, _ = hmain.shape
    w, mask, levels = _hgrn_tables(c)
    nl = levels + 1
    has_s0 = s0 is not None
    kern = functools.partial(_hgrn_kernel, c=c, levels=levels, nchunk=tc // c, has_s0=has_s0)
    col = lambda off: pl.BlockSpec((None, tc, HG_DIM), lambda bi, h, ci: (bi, ci, off + h))
    state_spec = pl.BlockSpec((None, None, HG_DIM, HG_DIM), lambda bi, h, ci: (bi, h, 0, 0))
    in_specs = [
        pl.BlockSpec((nl * c, c), lambda bi, h, ci: (0, 0)),
        pl.BlockSpec((nl, c, c), lambda bi, h, ci: (0, 0, 0)),
        pl.BlockSpec((1, HG_DIM), lambda bi, h, ci: (0, 0)),
        col(0), col(HG_HEADS), col(2 * HG_HEADS), col(3 * HG_HEADS), col(0),
    ]
    args = [w, mask, hg_norm_g.reshape(1, HG_DIM), hmain, hmain, hmain, hmain, hk]
    if has_s0:
        in_specs, args = [state_spec] + in_specs, [s0] + args
    return pl.pallas_call(
        kern,
        grid=(b, HG_HEADS, t // tc),
        in_specs=in_specs,
        out_specs=[col(0), state_spec],
        out_shape=[jax.ShapeDtypeStruct((b, t, HG_HEADS * HG_DIM), BF16),
                   jax.ShapeDtypeStruct((b, HG_HEADS, HG_DIM, HG_DIM), F32)],
        scratch_shapes=[pltpu.VMEM((HG_DIM, HG_DIM), F32)],
        compiler_params=_cparams(("arbitrary", "arbitrary", "arbitrary")),
        name="hgrn2",
    )(*args)


def _out_kernel(a_ref, m_ref, w_ref, x_ref, ga_ref, o_ref):
    half = a_ref.shape[1]
    w = w_ref[...].astype(BF16)
    mix = _dot(a_ref[...], w[:half]) + _dot(m_ref[...], w[half:])
    o_ref[...] = x_ref[...] + ga_ref[...] * mix


def _out_proj(a2d, m2d, w_out, x2d, ga, tm, rows_per_mod, tn=512):
    n, d = x2d.shape
    half = a2d.shape[1]
    r = ga.shape[1]
    tiles_per_mod = rows_per_mod // tm
    return pl.pallas_call(
        _out_kernel,
        grid=(n // tm, d // tn),
        in_specs=[
            pl.BlockSpec((tm, half), lambda i, j: (i, 0)),
            pl.BlockSpec((tm, half), lambda i, j: (i, 0)),
            pl.BlockSpec((2 * half, tn), lambda i, j: (0, j)),
            pl.BlockSpec((tm, tn), lambda i, j: (i, j)),
            pl.BlockSpec((None, r, tn), lambda i, j: (i // tiles_per_mod, 0, j)),
        ],
        out_specs=pl.BlockSpec((tm, tn), lambda i, j: (i, j)),
        out_shape=jax.ShapeDtypeStruct((n, d), F32),
        compiler_params=_cparams(("arbitrary", "arbitrary")),
        name="out_proj",
    )(a2d, m2d, w_out, x2d, ga)


def _router_kernel(x_ref, n2_ref, sc_ref, sh_ref, wr_ref, br_ref, h_ref, comb_ref):
    h = _rms(x_ref[...], n2_ref[...]) * (1.0 + sc_ref[...]) + sh_ref[...]
    h_ref[...] = h.astype(BF16)
    h_hi = h.astype(BF16)
    h_lo = (h - h_hi.astype(F32)).astype(BF16)
    w = wr_ref[...]
    w_hi = w.astype(BF16)
    w_lo = (w - w_hi.astype(F32)).astype(BF16)
    lg = _dot(h_hi, w_hi) + _dot(h_hi, w_lo) + _dot(h_lo, w_hi) + br_ref[...]

    lane = lax.broadcasted_iota(jnp.int32, lg.shape, 1)
    neg = -jnp.inf
    big = jnp.int32(LANES)

    def top(valid):
        m = jnp.max(jnp.where(valid, lg, neg), axis=-1, keepdims=True)
        idx = jnp.min(jnp.where(valid & (lg == m), lane, big), axis=-1, keepdims=True)
        return m, idx

    is_group = lane < N_GROUPS
    gmax, gidx = top(is_group)
    pg_top = 1.0 / jnp.sum(jnp.where(is_group, jnp.exp(lg - gmax), 0.0), axis=-1, keepdims=True)
    lo = ROUTE_LANE0 + EXPERTS_PER_GROUP * gidx
    in_group = (lane >= lo) & (lane < lo + EXPERTS_PER_GROUP)
    l0, e0 = top(in_group)
    l1, e1 = top(in_group & (lane != e0))
    r = jnp.exp(l1 - l0)
    w0 = pg_top / (1.0 + r)
    comb_ref[...] = jnp.where(lane == e0, w0, 0.0) + jnp.where(lane == e1, w0 * r, 0.0)


def _router(x1, n2, sc, sh, w_route, b_route, tm, rows_per_mod):
    n, d = x1.shape
    r = sc.shape[1]
    tiles_per_mod = rows_per_mod // tm
    mod_spec = pl.BlockSpec((None, r, d), lambda i: (i // tiles_per_mod, 0, 0))
    return pl.pallas_call(
        _router_kernel,
        grid=(n // tm,),
        in_specs=[
            pl.BlockSpec((tm, d), lambda i: (i, 0)),
            pl.BlockSpec((1, d), lambda i: (0, 0)),
            mod_spec, mod_spec,
            pl.BlockSpec((d, LANES), lambda i: (0, 0)),
            pl.BlockSpec((1, LANES), lambda i: (0, 0)),
        ],
        out_specs=[pl.BlockSpec((tm, d), lambda i: (i, 0)), pl.BlockSpec((tm, LANES), lambda i: (i, 0))],
        out_shape=[jax.ShapeDtypeStruct((n, d), BF16), jax.ShapeDtypeStruct((n, LANES), F32)],
        compiler_params=_cparams(("arbitrary",)),
        name="router",
    )(x1, n2.reshape(1, d), sc, sh, w_route, b_route)


def _moe_kernel(h_ref, comb_ref, wg_ref, wu_ref, wd_ref, x_ref, ga_ref, o_ref, acc):
    e = pl.program_id(1)

    @pl.when(e == 0)
    def _():
        acc[...] = jnp.zeros(acc.shape, F32)

    h = h_ref[...]
    a = _dot(h, wg_ref[...].astype(BF16))
    u = _dot(h, wu_ref[...].astype(BF16))
    lane = lax.broadcasted_iota(jnp.int32, comb_ref.shape, 1)
    cw = jnp.sum(jnp.where(lane == e + ROUTE_LANE0, comb_ref[...], 0.0), axis=-1, keepdims=True)
    hid = a * _sigmoid(a) * u * cw
    acc[...] += _dot(hid.astype(BF16), wd_ref[...].astype(BF16))

    @pl.when(e == pl.num_programs(1) - 1)
    def _():
        o_ref[...] = x_ref[...] + ga_ref[...] * acc[...]


def _moe(h2, comb, w_gate, w_up, w_down, x1, ga, tm, rows_per_mod):
    n, d = x1.shape
    ne, _, f = w_gate.shape
    r = ga.shape[1]
    tiles_per_mod = rows_per_mod // tm
    return pl.pallas_call(
        _moe_kernel,
        grid=(n // tm, ne),
        in_specs=[
            pl.BlockSpec((tm, d), lambda i, e: (i, 0)),
            pl.BlockSpec((tm, LANES), lambda i, e: (i, 0)),
            pl.BlockSpec((None, d, f), lambda i, e: (e, 0, 0)),
            pl.BlockSpec((None, d, f), lambda i, e: (e, 0, 0)),
            pl.BlockSpec((None, f, d), lambda i, e: (e, 0, 0)),
            pl.BlockSpec((tm, d), lambda i, e: (i, 0)),
            pl.BlockSpec((None, r, d), lambda i, e: (i // tiles_per_mod, 0, 0)),
        ],
        out_specs=pl.BlockSpec((tm, d), lambda i, e: (i, 0)),
        out_shape=jax.ShapeDtypeStruct((n, d), F32),
        scratch_shapes=[pltpu.VMEM((tm, d), F32)],
        compiler_params=_cparams(("arbitrary", "arbitrary")),
        name="moe",
    )(h2, comb, w_gate, w_up, w_down, x1, ga)


def _pick_tile(n, pref):
    t = min(pref, n)
    while n % t:
        t //= 2
    return t


def _layer(x, mods, s0, attend, lw, lam_init, hg_chunk, per_row_mod):
    (n1, n2, w_in, gq, gk, subln, lam4, hlb, hg_g, w_out, w_route, b_route, w_gate, w_up, w_down) = lw
    b, t, d = x.shape
    n = b * t
    x2d = x.reshape(n, d)
    if per_row_mod:
        tm = n
        rows_per_mod = n
        mods = [jnp.repeat(m, t, axis=0).reshape(1, n, d) for m in mods]
    else:
        tm = _pick_tile(t, 512)
        rows_per_mod = t
        mods = [m.reshape(b, 1, d) for m in mods]
    sh1, sc1, ga1, sh2, sc2, ga2 = mods

    q, k, v, hmain, hk = _in_proj(x2d, n1, sc1, sh1, w_in, gq, gk, hlb, tm, rows_per_mod)
    a = attend(q.reshape(b, t, -1), k.reshape(b, t, -1), v.reshape(b, t, -1))
    tpad = -(-t // hg_chunk) * hg_chunk
    hm3, hk3 = hmain.reshape(b, t, -1), hk.reshape(b, t, -1)
    if tpad != t:
        hm3 = jnp.pad(hm3, ((0, 0), (0, tpad - t), (0, 0)))
        hk3 = jnp.pad(hk3, ((0, 0), (0, tpad - t), (0, 0)))
    m, s_new = _hgrn2(hm3, hk3, hg_g, s0, hg_chunk, _pick_tile(tpad, 4 * hg_chunk))
    m = m[:, :t]
    x1 = _out_proj(a.reshape(n, -1), m.reshape(n, -1), w_out, x2d, ga1, tm, rows_per_mod)
    h2, comb = _router(x1, n2, sc2, sh2, w_route, b_route, tm, rows_per_mod)
    y = _moe(h2, comb, w_gate, w_up, w_down, x1, ga2, tm, rows_per_mod)
    return y.reshape(b, t, d), k, v, s_new


def kernel(x_prompt, x_sample, cache_k, cache_v, state_hgrn, page_table, c_prompt, c_sample, norm1_g, norm2_g, w_ada, b_ada, w_in, q_norm_g, k_norm_g, lambda_q1, lambda_k1, lambda_q2, lambda_k2, subln_g, hg_lower_bound, hg_norm_g, w_out, w_router_group, b_router_group, w_router_expert, b_router_expert, w_exp_gate, w_exp_up, w_exp_down):
    depth = norm1_g.shape[0]
    assert depth == 1, "single-layer trunk"
    l = 0
    lam_init = 0.8 - 0.6 * math.exp(-0.3 * l)
    bp, tp, d = x_prompt.shape
    bs, ts, _ = x_sample.shape

    c_all = jnp.concatenate([c_prompt, c_sample], axis=0)
    rpad = -(-c_all.shape[0] // 8) * 8
    c_all = jnp.pad(c_all, ((0, rpad - c_all.shape[0]), (0, 0)))
    mod = _adaln(c_all, w_ada[l], b_ada[l])
    mods_p = [mod[:bp, i * d:(i + 1) * d] for i in range(6)]
    mods_s = [mod[bp:bp + bs, i * d:(i + 1) * d] for i in range(6)]

    w_re = jnp.transpose(w_router_expert[l], (1, 0, 2)).reshape(d, N_EXPERTS)
    w_route = jnp.pad(jnp.concatenate([w_router_group[l], w_re], axis=1), ((0, 0), (0, LANES - N_GROUPS - N_EXPERTS)))
    b_route = jnp.pad(jnp.concatenate([b_router_group[l], b_router_expert[l].reshape(-1)]),
                      (0, LANES - N_GROUPS - N_EXPERTS)).reshape(1, LANES)
    lam4 = jnp.stack([lambda_q1[l], lambda_k1[l], lambda_q2[l], lambda_k2[l]], axis=0)

    lw = (norm1_g[l], norm2_g[l], w_in[l], q_norm_g[l], k_norm_g[l], subln_g[l], lam4, hg_lower_bound,
          hg_norm_g[l], w_out[l], w_route, b_route, w_exp_gate[l], w_exp_up[l], w_exp_down[l])

    def attend_prompt(q, k, v):
        return _prompt_attention(q, k, v, lam4, subln_g[l], lam_init)

    def attend_sample(q, k, v):
        pad = lambda a: jnp.pad(a, ((0, 0), (0, -ts % 8), (0, 0)))
        o = _sample_attention(pad(q), pad(k), pad(v), ts, cache_k[l], cache_v[l], page_table, lam4,
                              subln_g[l], lam_init)
        return o[:, :ts]

    yp, kp, vp, sp = _layer(x_prompt, mods_p, None, attend_prompt, lw, lam_init, 128, False)
    ys, kn, vn, sn = _layer(x_sample, mods_s, state_hgrn[l], attend_sample, lw, lam_init, 8, True)

    kv_shape = lambda b, t: (1, b, t, ATTN_KV_HEADS, 2 * QK_DIM)
    return (yp, ys,
            kp.reshape(kv_shape(bp, tp)), vp.reshape(kv_shape(bp, tp)),
            kn.reshape(kv_shape(bs, ts)), vn.reshape(kv_shape(bs, ts)),
            sp[None], sn[None])
```

```python
import functools
import math

import jax
import jax.numpy as jnp
import numpy as np
from jax import lax
from jax.experimental import pallas as pl
from jax.experimental.pallas import tpu as pltpu

F32 = jnp.float32
BF16 = jnp.bfloat16

QK_DIM = 64
V_DIM = 128
ATTN_KV_HEADS = 4
ATTN_GROUP = 2
ATTN_HEADS = ATTN_KV_HEADS * ATTN_GROUP
HG_HEADS = 8
HG_DIM = 128
N_GROUPS = 4
EXPERTS_PER_GROUP = 4
N_EXPERTS = N_GROUPS * EXPERTS_PER_GROUP
NORM_EPS = 1e-6
ALIBI_MAX_BIAS = 8.0
LANES = 128
MXU_DIM = 256
VMEM_LIMIT = 60 * 1024 * 1024
ROUTE_LANE0 = N_GROUPS


def _cparams(sem):
    return pltpu.CompilerParams(dimension_semantics=sem, vmem_limit_bytes=VMEM_LIMIT)


def _dot(a, b):
    return jnp.dot(a, b, preferred_element_type=F32)


def _dot_nt(a, b):
    return lax.dot_general(a, b, (((1,), (1,)), ((), ())), preferred_element_type=F32)


def _dot_tn(a, b):
    return lax.dot_general(a, b, (((0,), (0,)), ((), ())), preferred_element_type=F32)


def _split3(x):
    hi = x.astype(BF16)
    r = x - hi.astype(F32)
    mid = r.astype(BF16)
    lo = (r - mid.astype(F32)).astype(BF16)
    return hi, mid, lo


def _sigmoid(x):
    return 1.0 / (1.0 + jnp.exp(-x))


def _rms(x, g):
    return x * lax.rsqrt(jnp.mean(x * x, axis=-1, keepdims=True) + NORM_EPS) * g


def _ada_kernel(c_ref, w_ref, b_ref, o_ref):
    c = c_ref[...]
    s = c * _sigmoid(c)
    o_ref[...] = _dot(s.astype(BF16), w_ref[...].astype(BF16)) + b_ref[...]


def _adaln(c_all, w_ada, b_ada, tn=1024):
    r, d = c_all.shape
    n = w_ada.shape[1]
    return pl.pallas_call(
        _ada_kernel,
        grid=(n // tn,),
        in_specs=[
            pl.BlockSpec((r, d), lambda j: (0, 0)),
            pl.BlockSpec((d, tn), lambda j: (0, j)),
            pl.BlockSpec((1, tn), lambda j: (0, j)),
        ],
        out_specs=pl.BlockSpec((r, tn), lambda j: (0, j)),
        out_shape=jax.ShapeDtypeStruct((r, n), F32),
        compiler_params=_cparams(("arbitrary",)),
        name="adaln",
    )(c_all, w_ada, b_ada.reshape(1, n))


IN_TN = 512


def _group_sumsq(p, bd):
    x2 = p * p
    hi = x2.astype(BF16)
    lo = (x2 - hi.astype(F32)).astype(BF16)
    outs = []
    for c in range(p.shape[1] // MXU_DIM):
        sl = slice(c * MXU_DIM, (c + 1) * MXU_DIM)
        outs.append(_dot(hi[:, sl], bd) + _dot(lo[:, sl], bd))
    return jnp.concatenate(outs, axis=1)


def _in_kernel(x_ref, n1_ref, sc_ref, sh_ref, w_ref, gq_ref, gk_ref, lb_ref, bd_ref,
               q_ref, k_ref, v_ref, hm_ref, hk_ref, h_scr):
    j = pl.program_id(1)

    @pl.when(j == 0)
    def _():
        y = _rms(x_ref[...], n1_ref[...])
        h_scr[...] = (y * (1.0 + sc_ref[...]) + sh_ref[...]).astype(BF16)

    p = _dot(h_scr[...], w_ref[...].astype(BF16))

    def qk_norm(g):
        ms = _group_sumsq(p, bd_ref[...]) * (1.0 / QK_DIM)
        return p * lax.rsqrt(ms + NORM_EPS) * g

    @pl.when(j < 2)
    def _():
        q_ref[...] = (qk_norm(gq_ref[...]) * (QK_DIM ** -0.5)).astype(BF16)

    @pl.when(j == 2)
    def _():
        k_ref[...] = qk_norm(gk_ref[...])

    @pl.when(j == 3)
    def _():
        v_ref[...] = p

    @pl.when((j == 4) | (j == 5) | (j >= 10))
    def _():
        hm_ref[...] = p * _sigmoid(p)

    @pl.when((j == 6) | (j == 7))
    def _():
        a = lb_ref[...]
        e = jnp.exp(a - jnp.max(a, axis=0, keepdims=True))
        lb = e[0:1] / jnp.sum(e, axis=0, keepdims=True)
        hm_ref[...] = jnp.log(lb + (1.0 - lb) * _sigmoid(p))
        hk_ref[...] = (1.0 - lb) * _sigmoid(-p)

    @pl.when((j == 8) | (j == 9))
    def _():
        hm_ref[...] = p


def _in_proj(x2d, n1, sc, sh, w_in, gq, gk, hlb, tm, rows_per_mod):
    n, d = x2d.shape
    cols = w_in.shape[1]
    nj = cols // IN_TN
    assert nj == 12 and n % tm == 0
    r = sc.shape[1]
    tiles_per_mod = rows_per_mod // tm
    bd = jnp.asarray(np.kron(np.eye(MXU_DIM // QK_DIM), np.ones((QK_DIM, QK_DIM))), BF16)
    clamp = lambda j, lo: jnp.clip(j - lo, 0, 1)
    mod_spec = pl.BlockSpec((None, r, d), lambda i, j: (i // tiles_per_mod, 0, 0))
    return pl.pallas_call(
        _in_kernel,
        grid=(n // tm, nj),
        in_specs=[
            pl.BlockSpec((tm, d), lambda i, j: (i, 0)),
            pl.BlockSpec((1, d), lambda i, j: (0, 0)),
            mod_spec, mod_spec,
            pl.BlockSpec((d, IN_TN), lambda i, j: (0, j)),
            pl.BlockSpec((1, IN_TN), lambda i, j: (0, 0)),
            pl.BlockSpec((1, IN_TN), lambda i, j: (0, 0)),
            pl.BlockSpec((hlb.shape[0], IN_TN), lambda i, j: (0, clamp(j, 6))),
            pl.BlockSpec((MXU_DIM, MXU_DIM), lambda i, j: (0, 0)),
        ],
        out_specs=[
            pl.BlockSpec((tm, IN_TN), lambda i, j: (i, clamp(j, 0))),
            pl.BlockSpec((tm, IN_TN), lambda i, j: (i, 0)),
            pl.BlockSpec((tm, IN_TN), lambda i, j: (i, 0)),
            pl.BlockSpec((tm, IN_TN), lambda i, j: (i, jnp.clip(j - 4, 0, 7))),
            pl.BlockSpec((tm, IN_TN), lambda i, j: (i, clamp(j, 6))),
        ],
        out_shape=[
            jax.ShapeDtypeStruct((n, 2 * IN_TN), BF16),
            jax.ShapeDtypeStruct((n, IN_TN), F32),
            jax.ShapeDtypeStruct((n, IN_TN), F32),
            jax.ShapeDtypeStruct((n, 8 * IN_TN), F32),
            jax.ShapeDtypeStruct((n, 2 * IN_TN), F32),
        ],
        scratch_shapes=[pltpu.VMEM((tm, d), BF16)],
        compiler_params=_cparams(("arbitrary", "arbitrary")),
        name="in_proj",
    )(x2d, n1.reshape(1, d), sc, sh, w_in,
      jnp.tile(gq.reshape(1, QK_DIM), (1, IN_TN // QK_DIM)),
      jnp.tile(gk.reshape(1, QK_DIM), (1, IN_TN // QK_DIM)), hlb, bd)


def _alibi_slopes_np():
    h = np.arange(1, ATTN_HEADS + 1, dtype=np.float64)
    return np.exp2(-ALIBI_MAX_BIAS * h / ATTN_HEADS).reshape(ATTN_KV_HEADS, ATTN_GROUP).astype(np.float32)


def _alibi_slopes():
    out = np.zeros((ATTN_KV_HEADS, 1, LANES), np.float32)
    out[:, 0, :ATTN_GROUP] = _alibi_slopes_np()
    return jnp.asarray(out)


def _lambda(lam_ref, lam_init):
    l = lam_ref[...]
    s1 = jnp.sum(l[0:1] * l[1:2], axis=-1, keepdims=True)
    s2 = jnp.sum(l[2:3] * l[3:4], axis=-1, keepdims=True)
    return jnp.exp(s1) - jnp.exp(s2) + lam_init


def _diff_finish(o_all, t, lam, subln, lam_init, o_ref):
    for g in range(ATTN_GROUP):
        o = o_all[(2 * g) * t:(2 * g + 1) * t] - lam * o_all[(2 * g + 1) * t:(2 * g + 2) * t]
        y = _rms(o, subln) * (1.0 - lam_init)
        o_ref[:, g * LANES:(g + 1) * LANES] = y.astype(o_ref.dtype)


def _softmax_step_t(s, m_old, l_old):
    m_new = jnp.maximum(m_old, jnp.max(s, axis=0, keepdims=True))
    alpha = jnp.exp(m_old - m_new)
    p = jnp.exp(s - m_new)
    return m_new, alpha, alpha * l_old + jnp.sum(p, axis=0, keepdims=True), p.astype(BF16)


def _pattn_kernel(sl_ref, lam_ref, subln_ref, q_ref, k_ref, v_ref, o_ref, kb, vt, qs, s_scr, m_scr, l_scr, acc,
                  *, tq, lam_init):
    qi = pl.program_id(2)

    @pl.when(qi == 0)
    def _():
        t = k_ref.shape[0]
        pos = lax.broadcasted_iota(jnp.int32, (t, LANES), 0)
        ln = lax.broadcasted_iota(jnp.int32, (t, LANES), 1)
        aug = jnp.where(ln == 0, pos & ~(LANES - 1), jnp.where(ln == 1, pos & (LANES - 1), 0))
        kb[:, :LANES] = k_ref[...].astype(BF16)
        kb[:, LANES:] = aug.astype(F32).astype(BF16)
        for jb in range(t // tq):
            vt[jb] = v_ref[jb * tq:(jb + 1) * tq, :].T.astype(BF16)

    rows = 4 * tq
    half = 2 * tq
    q = q_ref[...]
    lane = lax.broadcasted_iota(jnp.int32, (tq, LANES), 1)
    zero = jnp.zeros((tq, LANES), BF16)
    for g in range(ATTN_GROUP):
        qg = q[:, g * LANES:(g + 1) * LANES]
        aug = jnp.where(lane < 2, sl_ref[0:1, g:g + 1], 0.0).astype(BF16)
        qs[(2 * g) * tq:(2 * g + 1) * tq, :] = jnp.concatenate([jnp.where(lane < QK_DIM, qg, zero), aug], axis=1)
        qs[(2 * g + 1) * tq:(2 * g + 2) * tq, :] = jnp.concatenate([jnp.where(lane >= QK_DIM, qg, zero), aug], axis=1)
    t_q = lax.broadcasted_iota(jnp.int32, (1, half), 1) % tq
    m_scr[...] = jnp.full(m_scr.shape, -jnp.inf, F32)
    l_scr[...] = jnp.zeros(l_scr.shape, F32)
    acc[...] = jnp.zeros(acc.shape, F32)

    def logits(jb, slot):
        s_scr[slot] = _dot_nt(kb[pl.ds(pl.multiple_of(jb * tq, tq), tq), :], qs[...])

    def attend(jb, slot, masked):
        for g in range(ATTN_GROUP):
            cs = slice(g * half, (g + 1) * half)
            s = s_scr[slot, :, cs]
            if masked:
                key = lax.broadcasted_iota(jnp.int32, (tq, 1), 0)
                s = jnp.where(key <= t_q, s, -jnp.inf)
            m_new, alpha, l_new, p = _softmax_step_t(s, m_scr[:, cs], l_scr[:, cs])
            acc[:, cs] = alpha * acc[:, cs] + _dot(vt[jb], p)
            m_scr[:, cs] = m_new
            l_scr[:, cs] = l_new

    logits(0, 0)

    def body(j, carry):
        logits(j + 1, (j + 1) % 2)
        attend(j, j % 2, False)
        return carry

    lax.fori_loop(0, qi, body, 0)
    attend(qi, qi % 2, True)

    o_t = acc[...] * (1.0 / l_scr[...])
    lam = _lambda(lam_ref, lam_init)
    for g in range(ATTN_GROUP):
        d = o_t[:, (2 * g) * tq:(2 * g + 1) * tq] - lam * o_t[:, (2 * g + 1) * tq:(2 * g + 2) * tq]
        y = _rms(d.T, subln_ref[...]) * (1.0 - lam_init)
        o_ref[:, g * LANES:(g + 1) * LANES] = y.astype(o_ref.dtype)


def _prompt_attention(q, k, v, lam4, subln, lam_init, tq=512):
    b, t, _ = q.shape
    tq = min(tq, t)
    assert t % tq == 0 and t < LANES * 256
    rows = 4 * tq
    kern = functools.partial(_pattn_kernel, tq=tq, lam_init=lam_init)
    return pl.pallas_call(
        kern,
        grid=(b, ATTN_KV_HEADS, t // tq),
        in_specs=[
            pl.BlockSpec((None, 1, LANES), lambda bi, h, qi: (h, 0, 0)),
            pl.BlockSpec((4, QK_DIM), lambda bi, h, qi: (0, 0)),
            pl.BlockSpec((1, V_DIM), lambda bi, h, qi: (0, 0)),
            pl.BlockSpec((None, tq, 2 * LANES), lambda bi, h, qi: (bi, qi, h)),
            pl.BlockSpec((None, t, LANES), lambda bi, h, qi: (bi, 0, h)),
            pl.BlockSpec((None, t, LANES), lambda bi, h, qi: (bi, 0, h)),
        ],
        out_specs=pl.BlockSpec((None, tq, 2 * LANES), lambda bi, h, qi: (bi, qi, h)),
        out_shape=jax.ShapeDtypeStruct((b, t, ATTN_HEADS * V_DIM), BF16),
        scratch_shapes=[
            pltpu.VMEM((t, 2 * LANES), BF16), pltpu.VMEM((t // tq, V_DIM, tq), BF16),
            pltpu.VMEM((rows, 2 * LANES), BF16), pltpu.VMEM((2, tq, rows), F32),
            pltpu.VMEM((1, rows), F32), pltpu.VMEM((1, rows), F32), pltpu.VMEM((V_DIM, rows), F32),
        ],
        compiler_params=_cparams(("arbitrary", "arbitrary", "arbitrary")),
        name="prompt_attention",
    )(_alibi_slopes(), lam4, subln.reshape(1, V_DIM), q, k, v)


def _to_col(row):
    n = row.shape[1]
    eye = lax.broadcasted_iota(jnp.int32, (n, n), 0) == lax.broadcasted_iota(jnp.int32, (n, n), 1)
    return jnp.sum(jnp.where(eye, row, 0.0), axis=1, keepdims=True)


def _sattn_kernel(pt_ref, srow_ref, lam_ref, subln_ref, qbd_ref, kn_ref, vn_ref, *rest,
                  npg, page, t, t_real, lam_init):
    del pt_ref
    kp_refs, vp_refs = rest[:npg], rest[npg:2 * npg]
    o_ref = rest[2 * npg]
    kc, vc, m_scr, l_scr, acc = rest[2 * npg + 1:]
    j = pl.program_id(1)
    nj = pl.num_programs(1)
    ntok = npg * page
    past = nj * ntok
    nq = qbd_ref.shape[0]
    width = ATTN_KV_HEADS * LANES

    @pl.when(j == 0)
    def _():
        m_scr[...] = jnp.full(m_scr.shape, -jnp.inf, F32)
        l_scr[...] = jnp.zeros(l_scr.shape, F32)
        acc[...] = jnp.zeros(acc.shape, F32)

    qbd = qbd_ref[...]
    srow = srow_ref[...]
    for p in range(npg):
        kc[p * page:(p + 1) * page, :] = kp_refs[p][...].reshape(page, width).astype(BF16)
        vc[p * page:(p + 1) * page, :] = vp_refs[p][...].reshape(page, width).astype(BF16)
    kpos = lax.broadcasted_iota(jnp.int32, (ntok, nq), 0) + j * ntok
    s = _dot_nt(kc[...], qbd) + kpos.astype(F32) * srow
    m_new, alpha, l_new, p = _softmax_step_t(s, m_scr[...], l_scr[...])
    acc[...] = _to_col(alpha) * acc[...] + _dot_tn(p, vc[...])
    m_scr[...] = m_new
    l_scr[...] = l_new

    @pl.when(j == nj - 1)
    def _():
        t_q = lax.broadcasted_iota(jnp.int32, (1, nq), 1) % t
        c = lax.broadcasted_iota(jnp.int32, (t, 1), 0)
        sn = _dot_nt(kn_ref[...].astype(BF16), qbd) + (past + c).astype(F32) * srow
        sn = jnp.where((c <= t_q) & (c < t_real), sn, -jnp.inf)
        m_fin, alpha_n, l_fin, _ = _softmax_step_t(sn, m_scr[...], l_scr[...])
        pn = jnp.exp(sn - m_fin)
        o = _to_col(alpha_n) * acc[...]
        vn = vn_ref[...]
        for u in range(t_real):
            o = o + _to_col(pn[u:u + 1]) * vn[u:u + 1]
        o = o * _to_col(1.0 / l_fin)
        lam = _lambda(lam_ref, lam_init)
        rq = nq // ATTN_KV_HEADS
        for h in range(ATTN_KV_HEADS):
            o_h = o[h * rq:(h + 1) * rq, h * LANES:(h + 1) * LANES]
            _diff_finish(o_h, t, lam, subln_ref[...], lam_init, o_ref.at[:, 2 * h * LANES:(2 * h + 2) * LANES])


def _sample_attention(q, k_new, v_new, t_real, cache_k, cache_v, page_table, lam4, subln, lam_init, npg=8):
    b, t, _ = q.shape
    tpad = t
    page = cache_k.shape[1]
    n_pages = page_table.shape[1]
    npg = min(npg, n_pages)
    assert n_pages % npg == 0
    ntok = npg * page
    width = ATTN_KV_HEADS * LANES
    nq = ATTN_KV_HEADS * ATTN_GROUP * 2 * t
    kern = functools.partial(_sattn_kernel, npg=npg, page=page, t=t, t_real=t_real, lam_init=lam_init)
    q6 = q.reshape(b, t, ATTN_KV_HEADS, ATTN_GROUP, 2, QK_DIM).transpose(0, 2, 3, 4, 1, 5)
    eye_h = jnp.eye(ATTN_KV_HEADS, dtype=q.dtype)[None, :, None, None, None, :, None, None]
    eye_m = jnp.eye(2, dtype=q.dtype)[None, None, None, :, None, None, :, None]
    qbd = (q6[:, :, :, :, :, None, None, :] * eye_h * eye_m).reshape(b, nq, width)
    slope_row = jnp.asarray(np.repeat(_alibi_slopes_np().reshape(-1), 2 * t)[None, :])

    def page_spec(p):
        return pl.BlockSpec((None, page, ATTN_KV_HEADS, LANES),
                            lambda bi, j, pt: (pt[bi * n_pages + j * npg + p], 0, 0, 0))

    grid_spec = pltpu.PrefetchScalarGridSpec(
        num_scalar_prefetch=1,
        grid=(b, n_pages // npg),
        in_specs=[
            pl.BlockSpec((1, nq), lambda bi, j, pt: (0, 0)),
            pl.BlockSpec((4, QK_DIM), lambda bi, j, pt: (0, 0)),
            pl.BlockSpec((1, V_DIM), lambda bi, j, pt: (0, 0)),
            pl.BlockSpec((None, nq, width), lambda bi, j, pt: (bi, 0, 0)),
            pl.BlockSpec((None, tpad, width), lambda bi, j, pt: (bi, 0, 0)),
            pl.BlockSpec((None, tpad, width), lambda bi, j, pt: (bi, 0, 0)),
        ] + [page_spec(p) for p in range(npg)] * 2,
        out_specs=pl.BlockSpec((None, t, ATTN_HEADS * V_DIM), lambda bi, j, pt: (bi, 0, 0)),
        scratch_shapes=[
            pltpu.VMEM((ntok, width), BF16), pltpu.VMEM((ntok, width), BF16),
            pltpu.VMEM((1, nq), F32), pltpu.VMEM((1, nq), F32), pltpu.VMEM((nq, width), F32),
        ],
    )
    return pl.pallas_call(
        kern,
        grid_spec=grid_spec,
        out_shape=jax.ShapeDtypeStruct((b, t, ATTN_HEADS * V_DIM), BF16),
        compiler_params=_cparams(("arbitrary", "arbitrary")),
        name="sample_attention",
    )(page_table.reshape(-1), slope_row, lam4, subln.reshape(1, V_DIM), qbd, k_new, v_new,
      *([cache_k] * npg), *([cache_v] * npg))


def _hgrn_tables(c):
    levels = int(math.log2(c))
    tril = np.tril(np.ones((c, c), np.float32))
    w = [tril]
    mask = [np.eye(c, dtype=np.float32)]
    idx = np.arange(c)
    for l in range(levels):
        bs = 2 << l
        mid = (idx // bs) * bs + bs // 2
        w.append(tril - tril[mid])
        upper = (idx % bs) >= bs // 2
        same = (idx[:, None] // bs) == (idx[None, :] // bs)
        mask.append((same & upper[:, None] & ~upper[None, :]).astype(np.float32))
    return jnp.asarray(np.concatenate(w, 0), BF16), jnp.asarray(np.stack(mask, 0)), levels


def _hgrn_kernel(*refs, c, levels, nchunk, hps, has_s0):
    if has_s0:
        s0_ref, refs = refs[0], refs[1:]
    (w_ref, mask_ref, g_ref, q_ref, lf_ref, v_ref, gate_ref, k_ref, o_ref, sn_ref, st) = refs
    ci = pl.program_id(2)

    @pl.when(ci == 0)
    def _():
        for hh in range(hps):
            st[hh] = s0_ref[hh].T if has_s0 else jnp.zeros((HG_DIM, HG_DIM), F32)

    w = w_ref[...]
    for n in range(nchunk):
        rs = slice(n * c, (n + 1) * c)
        for hh in range(hps):
            hs = slice(hh * HG_DIM, (hh + 1) * HG_DIM)
            qs, kk, v = q_ref[rs, hs], k_ref[rs, hs], v_ref[rs, hs]
            l_hi, l_mid, l_lo = _split3(lf_ref[rs, hs])
            gx = _dot(w, l_hi) + _dot(w, l_mid) + _dot(w, l_lo)
            g = gx[0:c]
            a = mask_ref[0] * _dot_nt(qs.astype(BF16), kk.astype(BF16))
            for l in range(levels):
                x = gx[(l + 1) * c:(l + 2) * c]
                qa = qs * jnp.exp(jnp.minimum(x, 0.0))
                kb = kk * jnp.exp(jnp.minimum(-x, 0.0))
                a = a + mask_ref[l + 1] * _dot_nt(qa.astype(BF16), kb.astype(BF16))
            vb = v.astype(BF16)
            st_old = st[hh]
            o = _dot_nt((qs * jnp.exp(g)).astype(BF16), st_old.astype(BF16)) + _dot(a.astype(BF16), vb)
            g_end = g[c - 1:c]
            kd = kk * jnp.exp(g_end - g)
            st[hh] = st_old * jnp.exp(g_end) + _dot_tn(vb, kd.astype(BF16))
            o_ref[rs, hs] = (_rms(o, g_ref[...]) * gate_ref[rs, hs]).astype(o_ref.dtype)

    @pl.when(ci == pl.num_programs(2) - 1)
    def _():
        for hh in range(hps):
            sn_ref[hh] = st[hh].T


def _hgrn2(hmain, hk, hg_norm_g, s0, c, tc, hps):
    b, t, _ = hmain.shape
    w, mask, levels = _hgrn_tables(c)
    nl = levels + 1
    has_s0 = s0 is not None
    kern = functools.partial(_hgrn_kernel, c=c, levels=levels, nchunk=tc // c, hps=hps, has_s0=has_s0)
    nhg = HG_HEADS // hps
    col = lambda off: pl.BlockSpec((None, tc, hps * HG_DIM), lambda bi, h, ci: (bi, ci, off * nhg + h))
    state_spec = pl.BlockSpec((None, hps, HG_DIM, HG_DIM), lambda bi, h, ci: (bi, h, 0, 0))
    in_specs = [
        pl.BlockSpec((nl * c, c), lambda bi, h, ci: (0, 0)),
        pl.BlockSpec((nl, c, c), lambda bi, h, ci: (0, 0, 0)),
        pl.BlockSpec((1, HG_DIM), lambda bi, h, ci: (0, 0)),
        col(0), col(1), col(2), col(3), col(0),
    ]
    args = [w, mask, hg_norm_g.reshape(1, HG_DIM), hmain, hmain, hmain, hmain, hk]
    if has_s0:
        in_specs, args = [state_spec] + in_specs, [s0] + args
    return pl.pallas_call(
        kern,
        grid=(b, nhg, t // tc),
        in_specs=in_specs,
        out_specs=[col(0), state_spec],
        out_shape=[jax.ShapeDtypeStruct((b, t, HG_HEADS * HG_DIM), BF16),
                   jax.ShapeDtypeStruct((b, HG_HEADS, HG_DIM, HG_DIM), F32)],
        scratch_shapes=[pltpu.VMEM((hps, HG_DIM, HG_DIM), F32)],
        compiler_params=_cparams(("arbitrary", "arbitrary", "arbitrary")),
        name="hgrn2",
    )(*args)


def _out_kernel(a_ref, m_ref, w_ref, x_ref, ga_ref, o_ref):
    half = a_ref.shape[1]
    w = w_ref[...].astype(BF16)
    mix = _dot(a_ref[...], w[:half]) + _dot(m_ref[...], w[half:])
    o_ref[...] = x_ref[...] + ga_ref[...] * mix


def _out_proj(a2d, m2d, w_out, x2d, ga, tm, rows_per_mod, tn=512):
    n, d = x2d.shape
    half = a2d.shape[1]
    r = ga.shape[1]
    tiles_per_mod = rows_per_mod // tm
    return pl.pallas_call(
        _out_kernel,
        grid=(n // tm, d // tn),
        in_specs=[
            pl.BlockSpec((tm, half), lambda i, j: (i, 0)),
            pl.BlockSpec((tm, half), lambda i, j: (i, 0)),
            pl.BlockSpec((2 * half, tn), lambda i, j: (0, j)),
            pl.BlockSpec((tm, tn), lambda i, j: (i, j)),
            pl.BlockSpec((None, r, tn), lambda i, j: (i // tiles_per_mod, 0, j)),
        ],
        out_specs=pl.BlockSpec((tm, tn), lambda i, j: (i, j)),
        out_shape=jax.ShapeDtypeStruct((n, d), F32),
        compiler_params=_cparams(("arbitrary", "arbitrary")),
        name="out_proj",
    )(a2d, m2d, w_out, x2d, ga)


def _pack_bf16_pairs(h):
    c = h.shape[1] // 2
    bits = lax.bitcast_convert_type(h.astype(BF16).astype(F32), jnp.uint32)
    return bits[:, :c] | (bits[:, c:] >> 16)


def _unpack_bf16_pairs(pk):
    hi = lax.bitcast_convert_type(pk & jnp.uint32(0xFFFF0000), F32).astype(BF16)
    lo = lax.bitcast_convert_type(pk << 16, F32).astype(BF16)
    return hi, lo


def _router_kernel(x_ref, n2_ref, sc_ref, sh_ref, wr_ref, br_ref, *rest, grouped):
    h = _rms(x_ref[...], n2_ref[...]) * (1.0 + sc_ref[...]) + sh_ref[...]
    if grouped:
        tril_ref, h_ref, idx_ref, rw_ref, cnt_ref, cnt_scr = rest
        h_ref[...] = _pack_bf16_pairs(h)
    else:
        h_ref, comb_ref = rest
        h_ref[...] = h.astype(BF16)
    h_hi = h.astype(BF16)
    h_lo = (h - h_hi.astype(F32)).astype(BF16)
    w = wr_ref[...]
    w_hi = w.astype(BF16)
    w_lo = (w - w_hi.astype(F32)).astype(BF16)
    lg = _dot(h_hi, w_hi) + _dot(h_hi, w_lo) + _dot(h_lo, w_hi) + br_ref[...]

    lane = lax.broadcasted_iota(jnp.int32, lg.shape, 1)
    neg = -jnp.inf
    big = jnp.int32(LANES)

    def top(valid):
        m = jnp.max(jnp.where(valid, lg, neg), axis=-1, keepdims=True)
        idx = jnp.min(jnp.where(valid & (lg == m), lane, big), axis=-1, keepdims=True)
        return m, idx

    is_group = lane < N_GROUPS
    gmax, gidx = top(is_group)
    pg_top = 1.0 / jnp.sum(jnp.where(is_group, jnp.exp(lg - gmax), 0.0), axis=-1, keepdims=True)
    lo = ROUTE_LANE0 + EXPERTS_PER_GROUP * gidx
    in_group = (lane >= lo) & (lane < lo + EXPERTS_PER_GROUP)
    l0, e0 = top(in_group)
    l1, e1 = top(in_group & (lane != e0))
    r = jnp.exp(l1 - l0)
    w0 = pg_top / (1.0 + r)
    w1 = w0 * r
    if not grouped:
        comb_ref[...] = jnp.where(lane == e0, w0, 0.0) + jnp.where(lane == e1, w1, 0.0)
        return

    @pl.when(pl.program_id(0) == 0)
    def _():
        cnt_scr[...] = jnp.zeros(cnt_scr.shape, F32)

    is0, is1 = lane == e0, lane == e1
    onehot = jnp.where(is0 | is1, 1.0, 0.0)
    before = _dot(tril_ref[...], onehot.astype(BF16)) + cnt_scr[...]
    rank0 = jnp.sum(jnp.where(is0, before, 0.0), axis=-1, keepdims=True).astype(jnp.int32)
    rank1 = jnp.sum(jnp.where(is1, before, 0.0), axis=-1, keepdims=True).astype(jnp.int32)
    cnt_scr[...] = cnt_scr[...] + jnp.sum(onehot, axis=0, keepdims=True)
    idx_ref[...] = jnp.where(lane == 0, e0 - ROUTE_LANE0, jnp.where(lane == 1, e1 - ROUTE_LANE0,
                             jnp.where(lane == 2, rank0, jnp.where(lane == 3, rank1, 0))))
    rw_ref[...] = jnp.where(lane == 0, w0, jnp.where(lane == 1, w1, 0.0))
    cnt_ref[...] = cnt_scr[...].astype(jnp.int32)


def _router(x1, n2, sc, sh, w_route, b_route, tm, rows_per_mod, grouped):
    n, d = x1.shape
    r = sc.shape[1]
    tiles_per_mod = rows_per_mod // tm
    mod_spec = pl.BlockSpec((None, r, d), lambda i: (i // tiles_per_mod, 0, 0))
    row_spec = lambda w: pl.BlockSpec((tm, w), lambda i: (i, 0))
    in_specs = [row_spec(d), pl.BlockSpec((1, d), lambda i: (0, 0)), mod_spec, mod_spec,
                pl.BlockSpec((d, LANES), lambda i: (0, 0)), pl.BlockSpec((1, LANES), lambda i: (0, 0))]
    args = [x1, n2.reshape(1, d), sc, sh, w_route, b_route]
    if grouped:
        in_specs.append(pl.BlockSpec((tm, tm), lambda i: (0, 0)))
        args.append(jnp.asarray(np.tril(np.ones((tm, tm), np.float32), -1), BF16))
        out_specs = [row_spec(d // 2), row_spec(LANES), row_spec(LANES), pl.BlockSpec((1, LANES), lambda i: (0, 0))]
        out_shape = [jax.ShapeDtypeStruct((n, d // 2), jnp.uint32), jax.ShapeDtypeStruct((n, LANES), jnp.int32),
                     jax.ShapeDtypeStruct((n, LANES), F32), jax.ShapeDtypeStruct((1, LANES), jnp.int32)]
        scratch = [pltpu.VMEM((1, LANES), F32)]
    else:
        out_specs = [row_spec(d), row_spec(LANES)]
        out_shape = [jax.ShapeDtypeStruct((n, d), BF16), jax.ShapeDtypeStruct((n, LANES), F32)]
        scratch = []
    return pl.pallas_call(
        functools.partial(_router_kernel, grouped=grouped),
        grid=(n // tm,),
        in_specs=in_specs,
        out_specs=out_specs,
        out_shape=out_shape,
        scratch_shapes=scratch,
        compiler_params=_cparams(("arbitrary",)),
        name="router",
    )(*args)


def _dispatch_kernel(pos_ref, h_hbm, xs_init, xs_hbm, sem, *, chunk):
    del xs_init
    n = h_hbm.shape[0]
    nch = n // chunk

    def wait_chunk(slot):
        pltpu.make_async_copy(h_hbm.at[pl.ds(0, 2 * chunk)], xs_hbm.at[pl.ds(0, 2 * chunk)], sem.at[slot]).wait()

    def outer(ci, carry):
        slot = ci % 2

        def body(r, c):
            t = ci * chunk + r
            src = h_hbm.at[pl.ds(t, 1)]
            pltpu.make_async_copy(src, xs_hbm.at[pl.ds(pos_ref[2 * t], 1)], sem.at[slot]).start()
            pltpu.make_async_copy(src, xs_hbm.at[pl.ds(pos_ref[2 * t + 1], 1)], sem.at[slot]).start()
            return c

        lax.fori_loop(0, chunk, body, 0)

        @pl.when(ci > 0)
        def _():
            wait_chunk(1 - slot)

        return carry

    lax.fori_loop(0, nch, outer, 0)
    wait_chunk((nch - 1) % 2)


def _dispatch(pos, h_packed, n_rows, chunk=256):
    n, w = h_packed.shape
    chunk = min(chunk, n)
    assert n % chunk == 0
    any_spec = pl.BlockSpec(memory_space=pl.ANY)
    return pl.pallas_call(
        functools.partial(_dispatch_kernel, chunk=chunk),
        grid_spec=pltpu.PrefetchScalarGridSpec(
            num_scalar_prefetch=1, grid=(1,), in_specs=[any_spec, any_spec], out_specs=any_spec,
            scratch_shapes=[pltpu.SemaphoreType.DMA((2,))]),
        out_shape=jax.ShapeDtypeStruct((n_rows, w), jnp.uint32),
        input_output_aliases={2: 0},
        compiler_params=_cparams(("arbitrary",)),
        name="moe_dispatch",
    )(pos, h_packed, jnp.zeros((n_rows, w), jnp.uint32))


def _expert_kernel(te_ref, nv_ref, x_ref, wg_ref, wu_ref, wd_ref, y_ref):
    del te_ref
    i = pl.program_id(0)

    @pl.when(i < nv_ref[0])
    def _():
        half = x_ref.shape[1]
        x_hi, x_lo = _unpack_bf16_pairs(x_ref[...])
        wg = wg_ref[...].astype(BF16)
        wu = wu_ref[...].astype(BF16)
        a = _dot(x_hi, wg[:half]) + _dot(x_lo, wg[half:])
        u = _dot(x_hi, wu[:half]) + _dot(x_lo, wu[half:])
        hid = a * _sigmoid(a) * u
        y_ref[...] = _dot(hid.astype(BF16), wd_ref[...].astype(BF16))

    @pl.when(i >= nv_ref[0])
    def _():
        y_ref[...] = jnp.zeros(y_ref.shape, F32)


def _experts(tile_expert, n_valid, xs, w_gate, w_up, w_down, tm):
    rows, half = xs.shape
    ne, d, f = w_gate.shape
    w_spec = lambda a, b: pl.BlockSpec((None, a, b), lambda i, te, nv: (te[i], 0, 0))
    return pl.pallas_call(
        _expert_kernel,
        grid_spec=pltpu.PrefetchScalarGridSpec(
            num_scalar_prefetch=2, grid=(rows // tm,),
            in_specs=[pl.BlockSpec((tm, half), lambda i, te, nv: (i, 0)), w_spec(d, f), w_spec(d, f), w_spec(f, d)],
            out_specs=pl.BlockSpec((tm, d), lambda i, te, nv: (i, 0))),
        out_shape=jax.ShapeDtypeStruct((rows, d), F32),
        compiler_params=_cparams(("arbitrary",)),
        name="moe_experts",
    )(tile_expert, n_valid, xs, w_gate, w_up, w_down)


def _combine_kernel(pos_ref, ys_hbm, x_ref, ga_ref, rw_ref, o_ref, buf, sem, *, tm):
    i = pl.program_id(0)
    n = pl.num_programs(0)

    def issue(tile, slot):
        def body(r, c):
            t = tile * tm + r
            pltpu.make_async_copy(ys_hbm.at[pl.ds(pos_ref[2 * t], 1)], buf.at[slot, pl.ds(r, 1)],
                                  sem.at[slot]).start()
            pltpu.make_async_copy(ys_hbm.at[pl.ds(pos_ref[2 * t + 1], 1)], buf.at[slot, pl.ds(tm + r, 1)],
                                  sem.at[slot]).start()
            return c

        lax.fori_loop(0, tm, body, 0)

    @pl.when(i == 0)
    def _():
        issue(0, 0)

    @pl.when(i + 1 < n)
    def _():
        issue(i + 1, (i + 1) % 2)

    slot = i % 2
    pltpu.make_async_copy(ys_hbm.at[pl.ds(0, 2 * tm)], buf.at[slot], sem.at[slot]).wait()
    w = rw_ref[...]
    y = w[:, 0:1] * buf[slot, 0:tm] + w[:, 1:2] * buf[slot, tm:2 * tm]
    o_ref[...] = x_ref[...] + ga_ref[...] * y


def _combine(pos, ys, x1, ga, rw, tm, rows_per_mod):
    n, d = x1.shape
    r = ga.shape[1]
    tiles_per_mod = rows_per_mod // tm
    return pl.pallas_call(
        functools.partial(_combine_kernel, tm=tm),
        grid_spec=pltpu.PrefetchScalarGridSpec(
            num_scalar_prefetch=1, grid=(n // tm,),
            in_specs=[pl.BlockSpec(memory_space=pl.ANY),
                      pl.BlockSpec((tm, d), lambda i, p: (i, 0)),
                      pl.BlockSpec((None, r, d), lambda i, p: (i // tiles_per_mod, 0, 0)),
                      pl.BlockSpec((tm, LANES), lambda i, p: (i, 0))],
            out_specs=pl.BlockSpec((tm, d), lambda i, p: (i, 0)),
            scratch_shapes=[pltpu.VMEM((2, 2 * tm, d), F32), pltpu.SemaphoreType.DMA((2,))]),
        out_shape=jax.ShapeDtypeStruct((n, d), F32),
        compiler_params=_cparams(("arbitrary",)),
        name="moe_combine",
    )(pos, ys, x1, ga, rw)


def _grouped_moe(hp, idx, rw, cnt, w_gate, w_up, w_down, x1, ga, rows_per_mod, tm_e=512, tm_c=256):
    n = hp.shape[0]
    ne = w_gate.shape[0]
    tm_e = min(tm_e, n)
    counts = cnt[0, ROUTE_LANE0:ROUTE_LANE0 + ne]
    tiles = (counts + tm_e - 1) // tm_e
    tile_end = jnp.cumsum(tiles)
    start_row = (tile_end - tiles) * tm_e
    n_tiles = (2 * n) // tm_e + ne
    tile_expert = jnp.minimum(jnp.sum(tile_end[None, :] <= jnp.arange(n_tiles)[:, None], axis=1), ne - 1)
    pos = (jnp.take(start_row, idx[:, 0:2], axis=0) + idx[:, 2:4]).reshape(-1).astype(jnp.int32)
    xs = _dispatch(pos, hp, n_tiles * tm_e)
    ys = _experts(tile_expert.astype(jnp.int32), tile_end[ne - 1:].astype(jnp.int32), xs, w_gate, w_up, w_down, tm_e)
    return _combine(pos, ys, x1, ga, rw, min(tm_c, n), rows_per_mod)


def _moe_kernel(h_ref, comb_ref, wg_ref, wu_ref, wd_ref, x_ref, ga_ref, o_ref, acc):
    e = pl.program_id(1)

    @pl.when(e == 0)
    def _():
        acc[...] = jnp.zeros(acc.shape, F32)

    h = h_ref[...]
    a = _dot(h, wg_ref[...].astype(BF16))
    u = _dot(h, wu_ref[...].astype(BF16))
    lane = lax.broadcasted_iota(jnp.int32, comb_ref.shape, 1)
    cw = jnp.sum(jnp.where(lane == e + ROUTE_LANE0, comb_ref[...], 0.0), axis=-1, keepdims=True)
    hid = a * _sigmoid(a) * u * cw
    acc[...] += _dot(hid.astype(BF16), wd_ref[...].astype(BF16))

    @pl.when(e == pl.num_programs(1) - 1)
    def _():
        o_ref[...] = x_ref[...] + ga_ref[...] * acc[...]


def _moe(h2, comb, w_gate, w_up, w_down, x1, ga, tm, rows_per_mod):
    n, d = x1.shape
    ne, _, f = w_gate.shape
    r = ga.shape[1]
    tiles_per_mod = rows_per_mod // tm
    return pl.pallas_call(
        _moe_kernel,
        grid=(n // tm, ne),
        in_specs=[
            pl.BlockSpec((tm, d), lambda i, e: (i, 0)),
            pl.BlockSpec((tm, LANES), lambda i, e: (i, 0)),
            pl.BlockSpec((None, d, f), lambda i, e: (e, 0, 0)),
            pl.BlockSpec((None, d, f), lambda i, e: (e, 0, 0)),
            pl.BlockSpec((None, f, d), lambda i, e: (e, 0, 0)),
            pl.BlockSpec((tm, d), lambda i, e: (i, 0)),
            pl.BlockSpec((None, r, d), lambda i, e: (i // tiles_per_mod, 0, 0)),
        ],
        out_specs=pl.BlockSpec((tm, d), lambda i, e: (i, 0)),
        out_shape=jax.ShapeDtypeStruct((n, d), F32),
        scratch_shapes=[pltpu.VMEM((tm, d), F32)],
        compiler_params=_cparams(("arbitrary", "arbitrary")),
        name="moe",
    )(h2, comb, w_gate, w_up, w_down, x1, ga)


def _pick_tile(n, pref):
    t = min(pref, n)
    while n % t:
        t //= 2
    return t


def _layer(x, mods, s0, attend, lw, lam_init, hg_chunk, hg_heads_per_step, per_row_mod):
    (n1, n2, w_in, gq, gk, subln, lam4, hlb, hg_g, w_out, w_route, b_route, w_gate, w_up, w_down) = lw
    b, t, d = x.shape
    n = b * t
    x2d = x.reshape(n, d)
    if per_row_mod:
        tm = n
        rows_per_mod = n
        mods = [jnp.repeat(m, t, axis=0).reshape(1, n, d) for m in mods]
    else:
        tm = _pick_tile(t, 512)
        rows_per_mod = t
        mods = [m.reshape(b, 1, d) for m in mods]
    sh1, sc1, ga1, sh2, sc2, ga2 = mods

    tm_in = tm if per_row_mod else _pick_tile(t, 1024)
    q, k, v, hmain, hk = _in_proj(x2d, n1, sc1, sh1, w_in, gq, gk, hlb, tm_in, rows_per_mod)
    a = attend(q.reshape(b, t, -1), k.reshape(b, t, -1), v.reshape(b, t, -1))
    tpad = -(-t // hg_chunk) * hg_chunk
    hm3, hk3 = hmain.reshape(b, t, -1), hk.reshape(b, t, -1)
    if tpad != t:
        hm3 = jnp.pad(hm3, ((0, 0), (0, tpad - t), (0, 0)))
        hk3 = jnp.pad(hk3, ((0, 0), (0, tpad - t), (0, 0)))
    m, s_new = _hgrn2(hm3, hk3, hg_g, s0, hg_chunk, _pick_tile(tpad, 2 * hg_chunk), hg_heads_per_step)
    m = m[:, :t]
    x1 = _out_proj(a.reshape(n, -1), m.reshape(n, -1), w_out, x2d, ga1, tm, rows_per_mod)
    if per_row_mod:
        h2, comb = _router(x1, n2, sc2, sh2, w_route, b_route, tm, rows_per_mod, False)
        y = _moe(h2, comb, w_gate, w_up, w_down, x1, ga2, tm, rows_per_mod)
    else:
        hp, idx, rw, cnt = _router(x1, n2, sc2, sh2, w_route, b_route, tm, rows_per_mod, True)
        y = _grouped_moe(hp, idx, rw, cnt, w_gate, w_up, w_down, x1, ga2, rows_per_mod)
    return y.reshape(b, t, d), k, v, s_new


def kernel(x_prompt, x_sample, cache_k, cache_v, state_hgrn, page_table, c_prompt, c_sample, norm1_g, norm2_g, w_ada, b_ada, w_in, q_norm_g, k_norm_g, lambda_q1, lambda_k1, lambda_q2, lambda_k2, subln_g, hg_lower_bound, hg_norm_g, w_out, w_router_group, b_router_group, w_router_expert, b_router_expert, w_exp_gate, w_exp_up, w_exp_down):
    depth = norm1_g.shape[0]
    assert depth == 1, "single-layer trunk"
    l = 0
    lam_init = 0.8 - 0.6 * math.exp(-0.3 * l)
    bp, tp, d = x_prompt.shape
    bs, ts, _ = x_sample.shape

    c_all = jnp.concatenate([c_prompt, c_sample], axis=0)
    rpad = -(-c_all.shape[0] // 8) * 8
    c_all = jnp.pad(c_all, ((0, rpad - c_all.shape[0]), (0, 0)))
    mod = _adaln(c_all, w_ada[l], b_ada[l])
    mods_p = [mod[:bp, i * d:(i + 1) * d] for i in range(6)]
    mods_s = [mod[bp:bp + bs, i * d:(i + 1) * d] for i in range(6)]

    w_re = jnp.transpose(w_router_expert[l], (1, 0, 2)).reshape(d, N_EXPERTS)
    w_route = jnp.pad(jnp.concatenate([w_router_group[l], w_re], axis=1), ((0, 0), (0, LANES - N_GROUPS - N_EXPERTS)))
    b_route = jnp.pad(jnp.concatenate([b_router_group[l], b_router_expert[l].reshape(-1)]),
                      (0, LANES - N_GROUPS - N_EXPERTS)).reshape(1, LANES)
    lam4 = jnp.stack([lambda_q1[l], lambda_k1[l], lambda_q2[l], lambda_k2[l]], axis=0)

    lw = (norm1_g[l], norm2_g[l], w_in[l], q_norm_g[l], k_norm_g[l], subln_g[l], lam4, hg_lower_bound,
          hg_norm_g[l], w_out[l], w_route, b_route, w_exp_gate[l], w_exp_up[l], w_exp_down[l])

    def attend_prompt(q, k, v):
        return _prompt_attention(q, k, v, lam4, subln_g[l], lam_init)

    def attend_sample(q, k, v):
        pad = lambda a: jnp.pad(a, ((0, 0), (0, -ts % 8), (0, 0)))
        o = _sample_attention(pad(q), pad(k), pad(v), ts, cache_k[l], cache_v[l], page_table, lam4,
                              subln_g[l], lam_init)
        return o[:, :ts]

    yp, kp, vp, sp = _layer(x_prompt, mods_p, None, attend_prompt, lw, lam_init, 128, 4, False)
    ys, kn, vn, sn = _layer(x_sample, mods_s, state_hgrn[l], attend_sample, lw, lam_init, 8, HG_HEADS, True)

    kv_shape = lambda b, t: (1, b, t, ATTN_KV_HEADS, 2 * QK_DIM)
    return (yp, ys,
            kp.reshape(kv_shape(bp, tp)), vp.reshape(kv_shape(bp, tp)),
            kn.reshape(kv_shape(bs, ts)), vn.reshape(kv_shape(bs, ts)),
            sp[None], sn[None])
```

```python
import functools
import math

import jax
import jax.numpy as jnp
import numpy as np
from jax import lax
from jax.experimental import pallas as pl
from jax.experimental.pallas import tpu as pltpu

F32 = jnp.float32
BF16 = jnp.bfloat16

QK_DIM = 64
V_DIM = 128
ATTN_KV_HEADS = 4
ATTN_GROUP = 2
ATTN_HEADS = ATTN_KV_HEADS * ATTN_GROUP
HG_HEADS = 8
HG_DIM = 128
N_GROUPS = 4
EXPERTS_PER_GROUP = 4
N_EXPERTS = N_GROUPS * EXPERTS_PER_GROUP
NORM_EPS = 1e-6
ALIBI_MAX_BIAS = 8.0
LANES = 128
MXU_DIM = 256
VMEM_LIMIT = 60 * 1024 * 1024
ROUTE_LANE0 = N_GROUPS
HG_TABLE_LEVELS = 2


def _cparams(sem):
    return pltpu.CompilerParams(dimension_semantics=sem, vmem_limit_bytes=VMEM_LIMIT)


def _dot(a, b):
    return jnp.dot(a, b, preferred_element_type=F32)


def _dot_nt(a, b):
    return lax.dot_general(a, b, (((1,), (1,)), ((), ())), preferred_element_type=F32)


def _dot_tn(a, b):
    return lax.dot_general(a, b, (((0,), (0,)), ((), ())), preferred_element_type=F32)


def _split3(x):
    hi = x.astype(BF16)
    r = x - hi.astype(F32)
    mid = r.astype(BF16)
    lo = (r - mid.astype(F32)).astype(BF16)
    return hi, mid, lo


def _sigmoid(x):
    return 1.0 / (1.0 + jnp.exp(-x))


def _rms(x, g):
    return x * lax.rsqrt(jnp.mean(x * x, axis=-1, keepdims=True) + NORM_EPS) * g


def _ada_kernel(c_ref, w_ref, b_ref, o_ref):
    c = c_ref[...]
    s = c * _sigmoid(c)
    o_ref[...] = _dot(s.astype(BF16), w_ref[...].astype(BF16)) + b_ref[...]


def _adaln(c_all, w_ada, b_ada, tn=1024):
    r, d = c_all.shape
    n = w_ada.shape[1]
    return pl.pallas_call(
        _ada_kernel,
        grid=(n // tn,),
        in_specs=[
            pl.BlockSpec((r, d), lambda j: (0, 0)),
            pl.BlockSpec((d, tn), lambda j: (0, j)),
            pl.BlockSpec((1, tn), lambda j: (0, j)),
        ],
        out_specs=pl.BlockSpec((r, tn), lambda j: (0, j)),
        out_shape=jax.ShapeDtypeStruct((r, n), F32),
        compiler_params=_cparams(("arbitrary",)),
        name="adaln",
    )(c_all, w_ada, b_ada.reshape(1, n))


IN_TN = 512


def _group_sumsq(p, bd):
    x2 = p * p
    hi = x2.astype(BF16)
    lo = (x2 - hi.astype(F32)).astype(BF16)
    outs = []
    for c in range(p.shape[1] // MXU_DIM):
        sl = slice(c * MXU_DIM, (c + 1) * MXU_DIM)
        outs.append(_dot(hi[:, sl], bd) + _dot(lo[:, sl], bd))
    return jnp.concatenate(outs, axis=1)


def _in_kernel(x_ref, n1_ref, sc_ref, sh_ref, w_ref, gq_ref, gk_ref, lb_ref, bd_ref,
               q_ref, k_ref, v_ref, hm_ref, hk_ref, h_scr):
    j = pl.program_id(1)

    @pl.when(j == 0)
    def _():
        y = _rms(x_ref[...], n1_ref[...])
        h_scr[...] = (y * (1.0 + sc_ref[...]) + sh_ref[...]).astype(BF16)

    p = _dot(h_scr[...], w_ref[...].astype(BF16))

    def qk_norm(g):
        ms = _group_sumsq(p, bd_ref[...]) * (1.0 / QK_DIM)
        return p * lax.rsqrt(ms + NORM_EPS) * g

    @pl.when(j < 2)
    def _():
        q_ref[...] = (qk_norm(gq_ref[...]) * (QK_DIM ** -0.5)).astype(BF16)

    @pl.when(j == 2)
    def _():
        k_ref[...] = qk_norm(gk_ref[...])

    @pl.when(j == 3)
    def _():
        v_ref[...] = p

    @pl.when((j == 4) | (j == 5) | (j >= 10))
    def _():
        hm_ref[...] = p * _sigmoid(p)

    @pl.when((j == 6) | (j == 7))
    def _():
        a = lb_ref[...]
        e = jnp.exp(a - jnp.max(a, axis=0, keepdims=True))
        lb = e[0:1] / jnp.sum(e, axis=0, keepdims=True)
        hm_ref[...] = jnp.log(lb + (1.0 - lb) * _sigmoid(p))
        hk_ref[...] = (1.0 - lb) * _sigmoid(-p)

    @pl.when((j == 8) | (j == 9))
    def _():
        hm_ref[...] = p


def _in_proj(x2d, n1, sc, sh, w_in, gq, gk, hlb, tm, rows_per_mod):
    n, d = x2d.shape
    cols = w_in.shape[1]
    nj = cols // IN_TN
    assert nj == 12 and n % tm == 0
    r = sc.shape[1]
    tiles_per_mod = rows_per_mod // tm
    bd = jnp.asarray(np.kron(np.eye(MXU_DIM // QK_DIM), np.ones((QK_DIM, QK_DIM))), BF16)
    clamp = lambda j, lo: jnp.clip(j - lo, 0, 1)
    mod_spec = pl.BlockSpec((None, r, d), lambda i, j: (i // tiles_per_mod, 0, 0))
    return pl.pallas_call(
        _in_kernel,
        grid=(n // tm, nj),
        in_specs=[
            pl.BlockSpec((tm, d), lambda i, j: (i, 0)),
            pl.BlockSpec((1, d), lambda i, j: (0, 0)),
            mod_spec, mod_spec,
            pl.BlockSpec((d, IN_TN), lambda i, j: (0, j)),
            pl.BlockSpec((1, IN_TN), lambda i, j: (0, 0)),
            pl.BlockSpec((1, IN_TN), lambda i, j: (0, 0)),
            pl.BlockSpec((hlb.shape[0], IN_TN), lambda i, j: (0, clamp(j, 6))),
            pl.BlockSpec((MXU_DIM, MXU_DIM), lambda i, j: (0, 0)),
        ],
        out_specs=[
            pl.BlockSpec((tm, IN_TN), lambda i, j: (i, clamp(j, 0))),
            pl.BlockSpec((tm, IN_TN), lambda i, j: (i, 0)),
            pl.BlockSpec((tm, IN_TN), lambda i, j: (i, 0)),
            pl.BlockSpec((tm, IN_TN), lambda i, j: (i, jnp.clip(j - 4, 0, 7))),
            pl.BlockSpec((tm, IN_TN), lambda i, j: (i, clamp(j, 6))),
        ],
        out_shape=[
            jax.ShapeDtypeStruct((n, 2 * IN_TN), BF16),
            jax.ShapeDtypeStruct((n, IN_TN), F32),
            jax.ShapeDtypeStruct((n, IN_TN), F32),
            jax.ShapeDtypeStruct((n, 8 * IN_TN), F32),
            jax.ShapeDtypeStruct((n, 2 * IN_TN), F32),
        ],
        scratch_shapes=[pltpu.VMEM((tm, d), BF16)],
        compiler_params=_cparams(("arbitrary", "arbitrary")),
        name="in_proj",
    )(x2d, n1.reshape(1, d), sc, sh, w_in,
      jnp.tile(gq.reshape(1, QK_DIM), (1, IN_TN // QK_DIM)),
      jnp.tile(gk.reshape(1, QK_DIM), (1, IN_TN // QK_DIM)), hlb, bd)


def _alibi_slopes_np():
    h = np.arange(1, ATTN_HEADS + 1, dtype=np.float64)
    return np.exp2(-ALIBI_MAX_BIAS * h / ATTN_HEADS).reshape(ATTN_KV_HEADS, ATTN_GROUP).astype(np.float32)


def _alibi_slopes():
    out = np.zeros((ATTN_KV_HEADS, 1, LANES), np.float32)
    out[:, 0, :ATTN_GROUP] = _alibi_slopes_np()
    return jnp.asarray(out)


def _lambda(lam_ref, lam_init):
    l = lam_ref[...]
    s1 = jnp.sum(l[0:1] * l[1:2], axis=-1, keepdims=True)
    s2 = jnp.sum(l[2:3] * l[3:4], axis=-1, keepdims=True)
    return jnp.exp(s1) - jnp.exp(s2) + lam_init


def _diff_finish(o_all, t, lam, subln, lam_init, o_ref):
    for g in range(ATTN_GROUP):
        o = o_all[(2 * g) * t:(2 * g + 1) * t] - lam * o_all[(2 * g + 1) * t:(2 * g + 2) * t]
        y = _rms(o, subln) * (1.0 - lam_init)
        o_ref[:, g * LANES:(g + 1) * LANES] = y.astype(o_ref.dtype)


def _softmax_step_t(s, m_old, l_old):
    m_new = jnp.maximum(m_old, jnp.max(s, axis=0, keepdims=True))
    alpha = jnp.exp(m_old - m_new)
    p = jnp.exp(s - m_new)
    return m_new, alpha, alpha * l_old + jnp.sum(p, axis=0, keepdims=True), p.astype(BF16)


def _pattn_kernel(sl_ref, lam_ref, subln_ref, q_ref, k_ref, v_ref, o_ref, kb, vt, qs, s_scr, m_scr, l_scr, acc,
                  *, tq, lam_init):
    qi = pl.program_id(2)

    @pl.when(qi == 0)
    def _():
        t = k_ref.shape[0]
        pos = lax.broadcasted_iota(jnp.int32, (t, LANES), 0)
        ln = lax.broadcasted_iota(jnp.int32, (t, LANES), 1)
        aug = jnp.where(ln == 0, pos & ~(LANES - 1), jnp.where(ln == 1, pos & (LANES - 1), 0))
        kb[:, :LANES] = k_ref[...].astype(BF16)
        kb[:, LANES:] = aug.astype(F32).astype(BF16)
        for jb in range(t // tq):
            vt[jb] = v_ref[jb * tq:(jb + 1) * tq, :].T.astype(BF16)

    rows = 4 * tq
    half = 2 * tq
    q = q_ref[...]
    lane = lax.broadcasted_iota(jnp.int32, (tq, LANES), 1)
    zero = jnp.zeros((tq, LANES), BF16)
    for g in range(ATTN_GROUP):
        qg = q[:, g * LANES:(g + 1) * LANES]
        aug = jnp.where(lane < 2, sl_ref[0:1, g:g + 1], 0.0).astype(BF16)
        qs[(2 * g) * tq:(2 * g + 1) * tq, :] = jnp.concatenate([jnp.where(lane < QK_DIM, qg, zero), aug], axis=1)
        qs[(2 * g + 1) * tq:(2 * g + 2) * tq, :] = jnp.concatenate([jnp.where(lane >= QK_DIM, qg, zero), aug], axis=1)
    t_q = lax.broadcasted_iota(jnp.int32, (1, half), 1) % tq
    m_scr[...] = jnp.full(m_scr.shape, -jnp.inf, F32)
    l_scr[...] = jnp.zeros(l_scr.shape, F32)
    acc[...] = jnp.zeros(acc.shape, F32)

    def logits(jb, slot):
        s_scr[slot] = _dot_nt(kb[pl.ds(pl.multiple_of(jb * tq, tq), tq), :], qs[...])

    def attend(jb, slot, masked):
        for g in range(ATTN_GROUP):
            cs = slice(g * half, (g + 1) * half)
            s = s_scr[slot, :, cs]
            if masked:
                key = lax.broadcasted_iota(jnp.int32, (tq, 1), 0)
                s = jnp.where(key <= t_q, s, -jnp.inf)
            m_new, alpha, l_new, p = _softmax_step_t(s, m_scr[:, cs], l_scr[:, cs])
            acc[:, cs] = alpha * acc[:, cs] + _dot(vt[jb], p)
            m_scr[:, cs] = m_new
            l_scr[:, cs] = l_new

    logits(0, 0)

    def body(j, carry):
        logits(j + 1, (j + 1) % 2)
        attend(j, j % 2, False)
        return carry

    lax.fori_loop(0, qi, body, 0)
    attend(qi, qi % 2, True)

    o_t = acc[...] * (1.0 / l_scr[...])
    lam = _lambda(lam_ref, lam_init)
    for g in range(ATTN_GROUP):
        d = o_t[:, (2 * g) * tq:(2 * g + 1) * tq] - lam * o_t[:, (2 * g + 1) * tq:(2 * g + 2) * tq]
        y = _rms(d.T, subln_ref[...]) * (1.0 - lam_init)
        o_ref[:, g * LANES:(g + 1) * LANES] = y.astype(o_ref.dtype)


def _prompt_attention(q, k, v, lam4, subln, lam_init, tq=512):
    b, t, _ = q.shape
    tq = min(tq, t)
    assert t % tq == 0 and t < LANES * 256
    rows = 4 * tq
    kern = functools.partial(_pattn_kernel, tq=tq, lam_init=lam_init)
    return pl.pallas_call(
        kern,
        grid=(b, ATTN_KV_HEADS, t // tq),
        in_specs=[
            pl.BlockSpec((None, 1, LANES), lambda bi, h, qi: (h, 0, 0)),
            pl.BlockSpec((4, QK_DIM), lambda bi, h, qi: (0, 0)),
            pl.BlockSpec((1, V_DIM), lambda bi, h, qi: (0, 0)),
            pl.BlockSpec((None, tq, 2 * LANES), lambda bi, h, qi: (bi, qi, h)),
            pl.BlockSpec((None, t, LANES), lambda bi, h, qi: (bi, 0, h)),
            pl.BlockSpec((None, t, LANES), lambda bi, h, qi: (bi, 0, h)),
        ],
        out_specs=pl.BlockSpec((None, tq, 2 * LANES), lambda bi, h, qi: (bi, qi, h)),
        out_shape=jax.ShapeDtypeStruct((b, t, ATTN_HEADS * V_DIM), BF16),
        scratch_shapes=[
            pltpu.VMEM((t, 2 * LANES), BF16), pltpu.VMEM((t // tq, V_DIM, tq), BF16),
            pltpu.VMEM((rows, 2 * LANES), BF16), pltpu.VMEM((2, tq, rows), F32),
            pltpu.VMEM((1, rows), F32), pltpu.VMEM((1, rows), F32), pltpu.VMEM((V_DIM, rows), F32),
        ],
        compiler_params=_cparams(("arbitrary", "arbitrary", "arbitrary")),
        name="prompt_attention",
    )(_alibi_slopes(), lam4, subln.reshape(1, V_DIM), q, k, v)


def _to_col(row):
    n = row.shape[1]
    eye = lax.broadcasted_iota(jnp.int32, (n, n), 0) == lax.broadcasted_iota(jnp.int32, (n, n), 1)
    return jnp.sum(jnp.where(eye, row, 0.0), axis=1, keepdims=True)


def _sattn_kernel(pt_ref, srow_ref, lam_ref, subln_ref, qbd_ref, kn_ref, vn_ref, *rest,
                  npg, page, t, t_real, lam_init):
    del pt_ref
    kp_refs, vp_refs = rest[:npg], rest[npg:2 * npg]
    o_ref = rest[2 * npg]
    kc, vc, m_scr, l_scr, acc = rest[2 * npg + 1:]
    j = pl.program_id(1)
    nj = pl.num_programs(1)
    ntok = npg * page
    past = nj * ntok
    nq = qbd_ref.shape[0]
    width = ATTN_KV_HEADS * LANES

    @pl.when(j == 0)
    def _():
        m_scr[...] = jnp.full(m_scr.shape, -jnp.inf, F32)
        l_scr[...] = jnp.zeros(l_scr.shape, F32)
        acc[...] = jnp.zeros(acc.shape, F32)

    qbd = qbd_ref[...]
    srow = srow_ref[...]
    for p in range(npg):
        kc[p * page:(p + 1) * page, :] = kp_refs[p][...].reshape(page, width).astype(BF16)
        vc[p * page:(p + 1) * page, :] = vp_refs[p][...].reshape(page, width).astype(BF16)
    kpos = lax.broadcasted_iota(jnp.int32, (ntok, nq), 0) + j * ntok
    s = _dot_nt(kc[...], qbd) + kpos.astype(F32) * srow
    m_new, alpha, l_new, p = _softmax_step_t(s, m_scr[...], l_scr[...])
    acc[...] = _to_col(alpha) * acc[...] + _dot_tn(p, vc[...])
    m_scr[...] = m_new
    l_scr[...] = l_new

    @pl.when(j == nj - 1)
    def _():
        t_q = lax.broadcasted_iota(jnp.int32, (1, nq), 1) % t
        c = lax.broadcasted_iota(jnp.int32, (t, 1), 0)
        sn = _dot_nt(kn_ref[...].astype(BF16), qbd) + (past + c).astype(F32) * srow
        sn = jnp.where((c <= t_q) & (c < t_real), sn, -jnp.inf)
        m_fin, alpha_n, l_fin, _ = _softmax_step_t(sn, m_scr[...], l_scr[...])
        pn = jnp.exp(sn - m_fin)
        o = _to_col(alpha_n) * acc[...]
        vn = vn_ref[...]
        for u in range(t_real):
            o = o + _to_col(pn[u:u + 1]) * vn[u:u + 1]
        o = o * _to_col(1.0 / l_fin)
        lam = _lambda(lam_ref, lam_init)
        rq = nq // ATTN_KV_HEADS
        for h in range(ATTN_KV_HEADS):
            o_h = o[h * rq:(h + 1) * rq, h * LANES:(h + 1) * LANES]
            _diff_finish(o_h, t, lam, subln_ref[...], lam_init, o_ref.at[:, 2 * h * LANES:(2 * h + 2) * LANES])


def _sample_attention(q, k_new, v_new, t_real, cache_k, cache_v, page_table, lam4, subln, lam_init, npg=16):
    b, t, _ = q.shape
    tpad = t
    page = cache_k.shape[1]
    n_pages = page_table.shape[1]
    npg = min(npg, n_pages)
    assert n_pages % npg == 0
    ntok = npg * page
    width = ATTN_KV_HEADS * LANES
    nq = ATTN_KV_HEADS * ATTN_GROUP * 2 * t
    kern = functools.partial(_sattn_kernel, npg=npg, page=page, t=t, t_real=t_real, lam_init=lam_init)
    q6 = q.reshape(b, t, ATTN_KV_HEADS, ATTN_GROUP, 2, QK_DIM).transpose(0, 2, 3, 4, 1, 5)
    eye_h = jnp.eye(ATTN_KV_HEADS, dtype=q.dtype)[None, :, None, None, None, :, None, None]
    eye_m = jnp.eye(2, dtype=q.dtype)[None, None, None, :, None, None, :, None]
    qbd = (q6[:, :, :, :, :, None, None, :] * eye_h * eye_m).reshape(b, nq, width)
    slope_row = jnp.asarray(np.repeat(_alibi_slopes_np().reshape(-1), 2 * t)[None, :])

    def page_spec(p):
        return pl.BlockSpec((None, page, ATTN_KV_HEADS, LANES),
                            lambda bi, j, pt: (pt[bi * n_pages + j * npg + p], 0, 0, 0))

    grid_spec = pltpu.PrefetchScalarGridSpec(
        num_scalar_prefetch=1,
        grid=(b, n_pages // npg),
        in_specs=[
            pl.BlockSpec((1, nq), lambda bi, j, pt: (0, 0)),
            pl.BlockSpec((4, QK_DIM), lambda bi, j, pt: (0, 0)),
            pl.BlockSpec((1, V_DIM), lambda bi, j, pt: (0, 0)),
            pl.BlockSpec((None, nq, width), lambda bi, j, pt: (bi, 0, 0)),
            pl.BlockSpec((None, tpad, width), lambda bi, j, pt: (bi, 0, 0)),
            pl.BlockSpec((None, tpad, width), lambda bi, j, pt: (bi, 0, 0)),
        ] + [page_spec(p) for p in range(npg)] * 2,
        out_specs=pl.BlockSpec((None, t, ATTN_HEADS * V_DIM), lambda bi, j, pt: (bi, 0, 0)),
        scratch_shapes=[
            pltpu.VMEM((ntok, width), BF16), pltpu.VMEM((ntok, width), BF16),
            pltpu.VMEM((1, nq), F32), pltpu.VMEM((1, nq), F32), pltpu.VMEM((nq, width), F32),
        ],
    )
    return pl.pallas_call(
        kern,
        grid_spec=grid_spec,
        out_shape=jax.ShapeDtypeStruct((b, t, ATTN_HEADS * V_DIM), BF16),
        compiler_params=_cparams(("arbitrary", "arbitrary")),
        name="sample_attention",
    )(page_table.reshape(-1), slope_row, lam4, subln.reshape(1, V_DIM), qbd, k_new, v_new,
      *([cache_k] * npg), *([cache_v] * npg))


def _hgrn_tables(c):
    levels = int(math.log2(c))
    tril = np.tril(np.ones((c, c), np.float32))
    w = [tril]
    mask = [np.eye(c, dtype=np.float32)]
    idx = np.arange(c)
    for l in range(levels):
        bs = 2 << l
        mid = (idx // bs) * bs + bs // 2
        if l < HG_TABLE_LEVELS:
            w.append(tril - tril[mid])
        upper = (idx % bs) >= bs // 2
        same = (idx[:, None] // bs) == (idx[None, :] // bs)
        mask.append((same & upper[:, None] & ~upper[None, :]).astype(np.float32))
    return jnp.asarray(np.concatenate(w, 0), BF16), jnp.asarray(np.stack(mask, 0)), levels


def _minus_mid_rows(g, bs):
    c, n = g.shape
    g3 = g.reshape(c // bs, bs, n)
    return (g3 - g3[:, bs // 2:bs // 2 + 1, :]).reshape(c, n)


def _hgrn_kernel(*refs, c, levels, nchunk, hps, has_s0):
    if has_s0:
        s0_ref, refs = refs[0], refs[1:]
    (w_ref, mask_ref, g_ref, q_ref, lf_ref, v_ref, gate_ref, k_ref, o_ref, sn_ref, st) = refs
    ci = pl.program_id(2)

    @pl.when(ci == 0)
    def _():
        for hh in range(hps):
            st[hh] = s0_ref[hh].T if has_s0 else jnp.zeros((HG_DIM, HG_DIM), F32)

    w = w_ref[...]
    pair = 2 * HG_DIM
    for n in range(nchunk):
        rs = slice(n * c, (n + 1) * c)
        for hh in range(hps):
            hs = slice(hh * HG_DIM, (hh + 1) * HG_DIM)
            if hh % 2 == 0:
                ps = slice(hh * HG_DIM, hh * HG_DIM + pair)
                l_hi, l_mid, l_lo = _split3(lf_ref[rs, ps])
                gx2 = _dot(w, l_hi) + _dot(w, l_mid) + _dot(w, l_lo)
            gx = gx2[:, (hh % 2) * HG_DIM:(hh % 2 + 1) * HG_DIM]
            qs, kk, v = q_ref[rs, hs], k_ref[rs, hs], v_ref[rs, hs]
            g = gx[0:c]
            a = mask_ref[0] * _dot_nt(qs.astype(BF16), kk.astype(BF16))
            for l in range(levels):
                x = gx[(l + 1) * c:(l + 2) * c] if l < HG_TABLE_LEVELS else _minus_mid_rows(g, 2 << l)
                qa = qs * jnp.exp(jnp.minimum(x, 0.0))
                kb = kk * jnp.exp(jnp.minimum(-x, 0.0))
                a = a + mask_ref[l + 1] * _dot_nt(qa.astype(BF16), kb.astype(BF16))
            vb = v.astype(BF16)
            st_old = st[hh]
            o = _dot_nt((qs * jnp.exp(g)).astype(BF16), st_old.astype(BF16)) + _dot(a.astype(BF16), vb)
            g_end = g[c - 1:c]
            kd = kk * jnp.exp(g_end - g)
            st[hh] = st_old * jnp.exp(g_end) + _dot_tn(vb, kd.astype(BF16))
            o_ref[rs, hs] = (_rms(o, g_ref[...]) * gate_ref[rs, hs]).astype(o_ref.dtype)

    @pl.when(ci == pl.num_programs(2) - 1)
    def _():
        for hh in range(hps):
            sn_ref[hh] = st[hh].T


def _hgrn2(hmain, hk, hg_norm_g, s0, c, tc, hps):
    b, t, _ = hmain.shape
    w, mask, levels = _hgrn_tables(c)
    nl = levels + 1
    has_s0 = s0 is not None
    kern = functools.partial(_hgrn_kernel, c=c, levels=levels, nchunk=tc // c, hps=hps, has_s0=has_s0)
    nhg = HG_HEADS // hps
    col = lambda off: pl.BlockSpec((None, tc, hps * HG_DIM), lambda bi, h, ci: (bi, ci, off * nhg + h))
    state_spec = pl.BlockSpec((None, hps, HG_DIM, HG_DIM), lambda bi, h, ci: (bi, h, 0, 0))
    in_specs = [
        pl.BlockSpec(w.shape, lambda bi, h, ci: (0, 0)),
        pl.BlockSpec((nl, c, c), lambda bi, h, ci: (0, 0, 0)),
        pl.BlockSpec((1, HG_DIM), lambda bi, h, ci: (0, 0)),
        col(0), col(1), col(2), col(3), col(0),
    ]
    args = [w, mask, hg_norm_g.reshape(1, HG_DIM), hmain, hmain, hmain, hmain, hk]
    if has_s0:
        in_specs, args = [state_spec] + in_specs, [s0] + args
    return pl.pallas_call(
        kern,
        grid=(b, nhg, t // tc),
        in_specs=in_specs,
        out_specs=[col(0), state_spec],
        out_shape=[jax.ShapeDtypeStruct((b, t, HG_HEADS * HG_DIM), BF16),
                   jax.ShapeDtypeStruct((b, HG_HEADS, HG_DIM, HG_DIM), F32)],
        scratch_shapes=[pltpu.VMEM((hps, HG_DIM, HG_DIM), F32)],
        compiler_params=_cparams(("arbitrary", "arbitrary", "arbitrary")),
        name="hgrn2",
    )(*args)


def _out_kernel(a_ref, m_ref, w_ref, x_ref, ga_ref, o_ref):
    half = a_ref.shape[1]
    w = w_ref[...].astype(BF16)
    mix = _dot(a_ref[...], w[:half]) + _dot(m_ref[...], w[half:])
    o_ref[...] = x_ref[...] + ga_ref[...] * mix


def _out_proj(a2d, m2d, w_out, x2d, ga, tm, rows_per_mod, tn=512):
    n, d = x2d.shape
    half = a2d.shape[1]
    r = ga.shape[1]
    tiles_per_mod = rows_per_mod // tm
    return pl.pallas_call(
        _out_kernel,
        grid=(n // tm, d // tn),
        in_specs=[
            pl.BlockSpec((tm, half), lambda i, j: (i, 0)),
            pl.BlockSpec((tm, half), lambda i, j: (i, 0)),
            pl.BlockSpec((2 * half, tn), lambda i, j: (0, j)),
            pl.BlockSpec((tm, tn), lambda i, j: (i, j)),
            pl.BlockSpec((None, r, tn), lambda i, j: (i // tiles_per_mod, 0, j)),
        ],
        out_specs=pl.BlockSpec((tm, tn), lambda i, j: (i, j)),
        out_shape=jax.ShapeDtypeStruct((n, d), F32),
        compiler_params=_cparams(("arbitrary", "arbitrary")),
        name="out_proj",
    )(a2d, m2d, w_out, x2d, ga)


def _router_kernel(x_ref, n2_ref, sc_ref, sh_ref, wr_ref, br_ref, *rest, grouped):
    h = _rms(x_ref[...], n2_ref[...]) * (1.0 + sc_ref[...]) + sh_ref[...]
    if grouped:
        tril_ref, h_ref, idx_ref, rw_ref, cnt_ref, cnt_scr = rest
        h_ref[...] = h
    else:
        h_ref, comb_ref = rest
        h_ref[...] = h.astype(BF16)
    h_hi = h.astype(BF16)
    h_lo = (h - h_hi.astype(F32)).astype(BF16)
    w = wr_ref[...]
    w_hi = w.astype(BF16)
    w_lo = (w - w_hi.astype(F32)).astype(BF16)
    lg = _dot(h_hi, w_hi) + _dot(h_hi, w_lo) + _dot(h_lo, w_hi) + br_ref[...]

    lane = lax.broadcasted_iota(jnp.int32, lg.shape, 1)
    neg = -jnp.inf
    big = jnp.int32(LANES)

    def top(valid):
        m = jnp.max(jnp.where(valid, lg, neg), axis=-1, keepdims=True)
        idx = jnp.min(jnp.where(valid & (lg == m), lane, big), axis=-1, keepdims=True)
        return m, idx

    is_group = lane < N_GROUPS
    gmax, gidx = top(is_group)
    pg_top = 1.0 / jnp.sum(jnp.where(is_group, jnp.exp(lg - gmax), 0.0), axis=-1, keepdims=True)
    lo = ROUTE_LANE0 + EXPERTS_PER_GROUP * gidx
    in_group = (lane >= lo) & (lane < lo + EXPERTS_PER_GROUP)
    l0, e0 = top(in_group)
    l1, e1 = top(in_group & (lane != e0))
    r = jnp.exp(l1 - l0)
    w0 = pg_top / (1.0 + r)
    w1 = w0 * r
    if not grouped:
        comb_ref[...] = jnp.where(lane == e0, w0, 0.0) + jnp.where(lane == e1, w1, 0.0)
        return

    @pl.when(pl.program_id(0) == 0)
    def _():
        cnt_scr[...] = jnp.zeros(cnt_scr.shape, F32)

    is0, is1 = lane == e0, lane == e1
    onehot = jnp.where(is0 | is1, 1.0, 0.0)
    before = _dot(tril_ref[...], onehot.astype(BF16)) + cnt_scr[...]
    rank0 = jnp.sum(jnp.where(is0, before, 0.0), axis=-1, keepdims=True).astype(jnp.int32)
    rank1 = jnp.sum(jnp.where(is1, before, 0.0), axis=-1, keepdims=True).astype(jnp.int32)
    cnt_scr[...] = cnt_scr[...] + jnp.sum(onehot, axis=0, keepdims=True)
    idx_ref[...] = jnp.where(lane == 0, e0 - ROUTE_LANE0, jnp.where(lane == 1, e1 - ROUTE_LANE0,
                             jnp.where(lane == 2, rank0, jnp.where(lane == 3, rank1, 0))))
    rw_ref[...] = jnp.where(lane == 0, w0, jnp.where(lane == 1, w1, 0.0))
    cnt_ref[...] = cnt_scr[...].astype(jnp.int32)


def _router(x1, n2, sc, sh, w_route, b_route, tm, rows_per_mod, grouped):
    n, d = x1.shape
    r = sc.shape[1]
    tiles_per_mod = rows_per_mod // tm
    mod_spec = pl.BlockSpec((None, r, d), lambda i: (i // tiles_per_mod, 0, 0))
    row_spec = lambda w: pl.BlockSpec((tm, w), lambda i: (i, 0))
    in_specs = [row_spec(d), pl.BlockSpec((1, d), lambda i: (0, 0)), mod_spec, mod_spec,
                pl.BlockSpec((d, LANES), lambda i: (0, 0)), pl.BlockSpec((1, LANES), lambda i: (0, 0))]
    args = [x1, n2.reshape(1, d), sc, sh, w_route, b_route]
    if grouped:
        in_specs.append(pl.BlockSpec((tm, tm), lambda i: (0, 0)))
        args.append(jnp.asarray(np.tril(np.ones((tm, tm), np.float32), -1), BF16))
        out_specs = [row_spec(d), row_spec(LANES), row_spec(LANES), pl.BlockSpec((1, LANES), lambda i: (0, 0))]
        out_shape = [jax.ShapeDtypeStruct((n, d), F32), jax.ShapeDtypeStruct((n, LANES), jnp.int32),
                     jax.ShapeDtypeStruct((n, LANES), F32), jax.ShapeDtypeStruct((1, LANES), jnp.int32)]
        scratch = [pltpu.VMEM((1, LANES), F32)]
    else:
        out_specs = [row_spec(d), row_spec(LANES)]
        out_shape = [jax.ShapeDtypeStruct((n, d), BF16), jax.ShapeDtypeStruct((n, LANES), F32)]
        scratch = []
    return pl.pallas_call(
        functools.partial(_router_kernel, grouped=grouped),
        grid=(n // tm,),
        in_specs=in_specs,
        out_specs=out_specs,
        out_shape=out_shape,
        scratch_shapes=scratch,
        compiler_params=_cparams(("arbitrary",)),
        name="router",
    )(*args)


def _dispatch_kernel(pos_ref, h_ref, xs_init, xs_hbm, sem, *, tm):
    del xs_init
    base = pl.program_id(0) * tm

    def body(r, c):
        t = base + r
        src = h_ref.at[pl.ds(r, 1)]
        pltpu.make_async_copy(src, xs_hbm.at[pl.ds(pos_ref[2 * t], 1)], sem).start()
        pltpu.make_async_copy(src, xs_hbm.at[pl.ds(pos_ref[2 * t + 1], 1)], sem).start()
        return c

    lax.fori_loop(0, tm, body, 0)
    for _ in range(2):
        pltpu.make_async_copy(h_ref, xs_hbm.at[pl.ds(0, tm)], sem).wait()


def _dispatch(pos, h, n_rows, tm=512):
    n, w = h.shape
    tm = min(tm, n)
    assert n % tm == 0
    any_spec = pl.BlockSpec(memory_space=pl.ANY)
    return pl.pallas_call(
        functools.partial(_dispatch_kernel, tm=tm),
        grid_spec=pltpu.PrefetchScalarGridSpec(
            num_scalar_prefetch=1, grid=(n // tm,),
            in_specs=[pl.BlockSpec((tm, w), lambda i, p: (i, 0)), any_spec], out_specs=any_spec,
            scratch_shapes=[pltpu.SemaphoreType.DMA(())]),
        out_shape=jax.ShapeDtypeStruct((n_rows, w), F32),
        input_output_aliases={2: 0},
        compiler_params=_cparams(("arbitrary",)),
        name="moe_dispatch",
    )(pos, h, jnp.zeros((n_rows, w), F32))


def _expert_kernel(te_ref, nv_ref, x_ref, wg_ref, wu_ref, wd_ref, y_ref):
    del te_ref
    i = pl.program_id(0)

    @pl.when(i < nv_ref[0])
    def _():
        x = x_ref[...].astype(BF16)
        a = _dot(x, wg_ref[...].astype(BF16))
        u = _dot(x, wu_ref[...].astype(BF16))
        hid = a * _sigmoid(a) * u
        y_ref[...] = _dot(hid.astype(BF16), wd_ref[...].astype(BF16))

    @pl.when(i >= nv_ref[0])
    def _():
        y_ref[...] = jnp.zeros(y_ref.shape, F32)


def _experts(tile_expert, n_valid, xs, w_gate, w_up, w_down, tm):
    rows, half = xs.shape
    ne, d, f = w_gate.shape
    w_spec = lambda a, b: pl.BlockSpec((None, a, b), lambda i, te, nv: (te[i], 0, 0))
    return pl.pallas_call(
        _expert_kernel,
        grid_spec=pltpu.PrefetchScalarGridSpec(
            num_scalar_prefetch=2, grid=(rows // tm,),
            in_specs=[pl.BlockSpec((tm, half), lambda i, te, nv: (i, 0)), w_spec(d, f), w_spec(d, f), w_spec(f, d)],
            out_specs=pl.BlockSpec((tm, d), lambda i, te, nv: (i, 0))),
        out_shape=jax.ShapeDtypeStruct((rows, d), F32),
        compiler_params=_cparams(("arbitrary",)),
        name="moe_experts",
    )(tile_expert, n_valid, xs, w_gate, w_up, w_down)


def _combine_kernel(pos_ref, ys_hbm, x_ref, ga_ref, rw_ref, o_ref, buf, sem, *, tm):
    i = pl.program_id(0)
    n = pl.num_programs(0)

    def issue(tile, slot):
        def body(r, c):
            t = tile * tm + r
            pltpu.make_async_copy(ys_hbm.at[pl.ds(pos_ref[2 * t], 1)], buf.at[slot, pl.ds(r, 1)],
                                  sem.at[slot]).start()
            pltpu.make_async_copy(ys_hbm.at[pl.ds(pos_ref[2 * t + 1], 1)], buf.at[slot, pl.ds(tm + r, 1)],
                                  sem.at[slot]).start()
            return c

        lax.fori_loop(0, tm, body, 0)

    @pl.when(i == 0)
    def _():
        issue(0, 0)

    @pl.when(i + 1 < n)
    def _():
        issue(i + 1, (i + 1) % 2)

    slot = i % 2
    pltpu.make_async_copy(ys_hbm.at[pl.ds(0, 2 * tm)], buf.at[slot], sem.at[slot]).wait()
    w = rw_ref[...]
    y = w[:, 0:1] * buf[slot, 0:tm] + w[:, 1:2] * buf[slot, tm:2 * tm]
    o_ref[...] = x_ref[...] + ga_ref[...] * y


def _combine(pos, ys, x1, ga, rw, tm, rows_per_mod):
    n, d = x1.shape
    r = ga.shape[1]
    tiles_per_mod = rows_per_mod // tm
    return pl.pallas_call(
        functools.partial(_combine_kernel, tm=tm),
        grid_spec=pltpu.PrefetchScalarGridSpec(
            num_scalar_prefetch=1, grid=(n // tm,),
            in_specs=[pl.BlockSpec(memory_space=pl.ANY),
                      pl.BlockSpec((tm, d), lambda i, p: (i, 0)),
                      pl.BlockSpec((None, r, d), lambda i, p: (i // tiles_per_mod, 0, 0)),
                      pl.BlockSpec((tm, LANES), lambda i, p: (i, 0))],
            out_specs=pl.BlockSpec((tm, d), lambda i, p: (i, 0)),
            scratch_shapes=[pltpu.VMEM((2, 2 * tm, d), F32), pltpu.SemaphoreType.DMA((2,))]),
        out_shape=jax.ShapeDtypeStruct((n, d), F32),
        compiler_params=_cparams(("arbitrary",)),
        name="moe_combine",
    )(pos, ys, x1, ga, rw)


def _grouped_moe(hp, idx, rw, cnt, w_gate, w_up, w_down, x1, ga, rows_per_mod, tm_e=512, tm_c=256):
    n = hp.shape[0]
    ne = w_gate.shape[0]
    tm_e = min(tm_e, n)
    counts = cnt[0, ROUTE_LANE0:ROUTE_LANE0 + ne]
    tiles = (counts + tm_e - 1) // tm_e
    tile_end = jnp.cumsum(tiles)
    start_row = (tile_end - tiles) * tm_e
    n_tiles = (2 * n) // tm_e + ne
    tile_expert = jnp.minimum(jnp.sum(tile_end[None, :] <= jnp.arange(n_tiles)[:, None], axis=1), ne - 1)
    pos = (jnp.take(start_row, idx[:, 0:2], axis=0) + idx[:, 2:4]).reshape(-1).astype(jnp.int32)
    xs = _dispatch(pos, hp, n_tiles * tm_e)
    ys = _experts(tile_expert.astype(jnp.int32), tile_end[ne - 1:].astype(jnp.int32), xs, w_gate, w_up, w_down, tm_e)
    return _combine(pos, ys, x1, ga, rw, min(tm_c, n), rows_per_mod)


def _moe_kernel(h_ref, comb_ref, wg_ref, wu_ref, wd_ref, x_ref, ga_ref, o_ref, acc):
    e = pl.program_id(1)

    @pl.when(e == 0)
    def _():
        acc[...] = jnp.zeros(acc.shape, F32)

    h = h_ref[...]
    a = _dot(h, wg_ref[...].astype(BF16))
    u = _dot(h, wu_ref[...].astype(BF16))
    lane = lax.broadcasted_iota(jnp.int32, comb_ref.shape, 1)
    cw = jnp.sum(jnp.where(lane == e + ROUTE_LANE0, comb_ref[...], 0.0), axis=-1, keepdims=True)
    hid = a * _sigmoid(a) * u * cw
    acc[...] += _dot(hid.astype(BF16), wd_ref[...].astype(BF16))

    @pl.when(e == pl.num_programs(1) - 1)
    def _():
        o_ref[...] = x_ref[...] + ga_ref[...] * acc[...]


def _moe(h2, comb, w_gate, w_up, w_down, x1, ga, tm, rows_per_mod):
    n, d = x1.shape
    ne, _, f = w_gate.shape
    r = ga.shape[1]
    tiles_per_mod = rows_per_mod // tm
    return pl.pallas_call(
        _moe_kernel,
        grid=(n // tm, ne),
        in_specs=[
            pl.BlockSpec((tm, d), lambda i, e: (i, 0)),
            pl.BlockSpec((tm, LANES), lambda i, e: (i, 0)),
            pl.BlockSpec((None, d, f), lambda i, e: (e, 0, 0)),
            pl.BlockSpec((None, d, f), lambda i, e: (e, 0, 0)),
            pl.BlockSpec((None, f, d), lambda i, e: (e, 0, 0)),
            pl.BlockSpec((tm, d), lambda i, e: (i, 0)),
            pl.BlockSpec((None, r, d), lambda i, e: (i // tiles_per_mod, 0, 0)),
        ],
        out_specs=pl.BlockSpec((tm, d), lambda i, e: (i, 0)),
        out_shape=jax.ShapeDtypeStruct((n, d), F32),
        scratch_shapes=[pltpu.VMEM((tm, d), F32)],
        compiler_params=_cparams(("arbitrary", "arbitrary")),
        name="moe",
    )(h2, comb, w_gate, w_up, w_down, x1, ga)


def _pick_tile(n, pref):
    t = min(pref, n)
    while n % t:
        t //= 2
    return t


def _layer(x, mods, s0, attend, lw, lam_init, hg_chunk, hg_heads_per_step, per_row_mod):
    (n1, n2, w_in, gq, gk, subln, lam4, hlb, hg_g, w_out, w_route, b_route, w_gate, w_up, w_down) = lw
    b, t, d = x.shape
    n = b * t
    x2d = x.reshape(n, d)
    if per_row_mod:
        tm = n
        rows_per_mod = n
        mods = [jnp.repeat(m, t, axis=0).reshape(1, n, d) for m in mods]
    else:
        tm = _pick_tile(t, 512)
        rows_per_mod = t
        mods = [m.reshape(b, 1, d) for m in mods]
    sh1, sc1, ga1, sh2, sc2, ga2 = mods

    tm_in = tm if per_row_mod else _pick_tile(t, 1024)
    q, k, v, hmain, hk = _in_proj(x2d, n1, sc1, sh1, w_in, gq, gk, hlb, tm_in, rows_per_mod)
    a = attend(q.reshape(b, t, -1), k.reshape(b, t, -1), v.reshape(b, t, -1))
    tpad = -(-t // hg_chunk) * hg_chunk
    hm3, hk3 = hmain.reshape(b, t, -1), hk.reshape(b, t, -1)
    if tpad != t:
        hm3 = jnp.pad(hm3, ((0, 0), (0, tpad - t), (0, 0)))
        hk3 = jnp.pad(hk3, ((0, 0), (0, tpad - t), (0, 0)))
    m, s_new = _hgrn2(hm3, hk3, hg_g, s0, hg_chunk, _pick_tile(tpad, 2 * hg_chunk), hg_heads_per_step)
    m = m[:, :t]
    x1 = _out_proj(a.reshape(n, -1), m.reshape(n, -1), w_out, x2d, ga1, tm, rows_per_mod)
    if per_row_mod:
        h2, comb = _router(x1, n2, sc2, sh2, w_route, b_route, tm, rows_per_mod, False)
        y = _moe(h2, comb, w_gate, w_up, w_down, x1, ga2, tm, rows_per_mod)
    else:
        hp, idx, rw, cnt = _router(x1, n2, sc2, sh2, w_route, b_route, tm, rows_per_mod, True)
        y = _grouped_moe(hp, idx, rw, cnt, w_gate, w_up, w_down, x1, ga2, rows_per_mod)
    return y.reshape(b, t, d), k, v, s_new


def kernel(x_prompt, x_sample, cache_k, cache_v, state_hgrn, page_table, c_prompt, c_sample, norm1_g, norm2_g, w_ada, b_ada, w_in, q_norm_g, k_norm_g, lambda_q1, lambda_k1, lambda_q2, lambda_k2, subln_g, hg_lower_bound, hg_norm_g, w_out, w_router_group, b_router_group, w_router_expert, b_router_expert, w_exp_gate, w_exp_up, w_exp_down):
    depth = norm1_g.shape[0]
    assert depth == 1, "single-layer trunk"
    l = 0
    lam_init = 0.8 - 0.6 * math.exp(-0.3 * l)
    bp, tp, d = x_prompt.shape
    bs, ts, _ = x_sample.shape

    c_all = jnp.concatenate([c_prompt, c_sample], axis=0)
    rpad = -(-c_all.shape[0] // 8) * 8
    c_all = jnp.pad(c_all, ((0, rpad - c_all.shape[0]), (0, 0)))
    mod = _adaln(c_all, w_ada[l], b_ada[l])
    mods_p = [mod[:bp, i * d:(i + 1) * d] for i in range(6)]
    mods_s = [mod[bp:bp + bs, i * d:(i + 1) * d] for i in range(6)]

    w_re = jnp.transpose(w_router_expert[l], (1, 0, 2)).reshape(d, N_EXPERTS)
    w_route = jnp.pad(jnp.concatenate([w_router_group[l], w_re], axis=1), ((0, 0), (0, LANES - N_GROUPS - N_EXPERTS)))
    b_route = jnp.pad(jnp.concatenate([b_router_group[l], b_router_expert[l].reshape(-1)]),
                      (0, LANES - N_GROUPS - N_EXPERTS)).reshape(1, LANES)
    lam4 = jnp.stack([lambda_q1[l], lambda_k1[l], lambda_q2[l], lambda_k2[l]], axis=0)

    lw = (norm1_g[l], norm2_g[l], w_in[l], q_norm_g[l], k_norm_g[l], subln_g[l], lam4, hg_lower_bound,
          hg_norm_g[l], w_out[l], w_route, b_route, w_exp_gate[l], w_exp_up[l], w_exp_down[l])

    def attend_prompt(q, k, v):
        return _prompt_attention(q, k, v, lam4, subln_g[l], lam_init)

    def attend_sample(q, k, v):
        pad = lambda a: jnp.pad(a, ((0, 0), (0, -ts % 8), (0, 0)))
        o = _sample_attention(pad(q), pad(k), pad(v), ts, cache_k[l], cache_v[l], page_table, lam4,
                              subln_g[l], lam_init)
        return o[:, :ts]

    yp, kp, vp, sp = _layer(x_prompt, mods_p, None, attend_prompt, lw, lam_init, 128, 4, False)
    ys, kn, vn, sn = _layer(x_sample, mods_s, state_hgrn[l], attend_sample, lw, lam_init, 8, HG_HEADS, True)

    kv_shape = lambda b, t: (1, b, t, ATTN_KV_HEADS, 2 * QK_DIM)
    return (yp, ys,
            kp.reshape(kv_shape(bp, tp)), vp.reshape(kv_shape(bp, tp)),
            kn.reshape(kv_shape(bs, ts)), vn.reshape(kv_shape(bs, ts)),
            sp[None], sn[None])
```

```python
import functools
import math

import jax
import jax.numpy as jnp
import numpy as np
from jax import lax
from jax.experimental import pallas as pl
from jax.experimental.pallas import tpu as pltpu

F32 = jnp.float32
BF16 = jnp.bfloat16

QK_DIM = 64
V_DIM = 128
ATTN_KV_HEADS = 4
ATTN_GROUP = 2
ATTN_HEADS = ATTN_KV_HEADS * ATTN_GROUP
HG_HEADS = 8
HG_DIM = 128
N_GROUPS = 4
EXPERTS_PER_GROUP = 4
N_EXPERTS = N_GROUPS * EXPERTS_PER_GROUP
NORM_EPS = 1e-6
ALIBI_MAX_BIAS = 8.0
LANES = 128
MXU_DIM = 256
VMEM_LIMIT = 60 * 1024 * 1024
ROUTE_LANE0 = N_GROUPS
HG_TABLE_LEVELS = 2


def _cparams(sem):
    return pltpu.CompilerParams(dimension_semantics=sem, vmem_limit_bytes=VMEM_LIMIT)


def _dot(a, b):
    return jnp.dot(a, b, preferred_element_type=F32)


def _dot_nt(a, b):
    return lax.dot_general(a, b, (((1,), (1,)), ((), ())), preferred_element_type=F32)


def _dot_tn(a, b):
    return lax.dot_general(a, b, (((0,), (0,)), ((), ())), preferred_element_type=F32)


def _split3(x):
    hi = x.astype(BF16)
    r = x - hi.astype(F32)
    mid = r.astype(BF16)
    lo = (r - mid.astype(F32)).astype(BF16)
    return hi, mid, lo


def _sigmoid(x):
    return 1.0 / (1.0 + jnp.exp(-x))


def _rms(x, g):
    return x * lax.rsqrt(jnp.mean(x * x, axis=-1, keepdims=True) + NORM_EPS) * g


def _ada_kernel(c_ref, w_ref, b_ref, o_ref):
    c = c_ref[...]
    s = c * _sigmoid(c)
    o_ref[...] = _dot(s.astype(BF16), w_ref[...].astype(BF16)) + b_ref[...]


def _adaln(c_all, w_ada, b_ada, tn=1024):
    r, d = c_all.shape
    n = w_ada.shape[1]
    return pl.pallas_call(
        _ada_kernel,
        grid=(n // tn,),
        in_specs=[
            pl.BlockSpec((r, d), lambda j: (0, 0)),
            pl.BlockSpec((d, tn), lambda j: (0, j)),
            pl.BlockSpec((1, tn), lambda j: (0, j)),
        ],
        out_specs=pl.BlockSpec((r, tn), lambda j: (0, j)),
        out_shape=jax.ShapeDtypeStruct((r, n), F32),
        compiler_params=_cparams(("arbitrary",)),
        name="adaln",
    )(c_all, w_ada, b_ada.reshape(1, n))


IN_TN = 512


def _group_sumsq(p, bd):
    x2 = p * p
    hi = x2.astype(BF16)
    lo = (x2 - hi.astype(F32)).astype(BF16)
    outs = []
    for c in range(p.shape[1] // MXU_DIM):
        sl = slice(c * MXU_DIM, (c + 1) * MXU_DIM)
        outs.append(_dot(hi[:, sl], bd) + _dot(lo[:, sl], bd))
    return jnp.concatenate(outs, axis=1)


def _in_kernel(x_ref, n1_ref, sc_ref, sh_ref, w_ref, gq_ref, gk_ref, lb_ref, bd_ref,
               q_ref, k_ref, v_ref, hm_ref, hk_ref, h_scr):
    j = pl.program_id(1)

    @pl.when(j == 0)
    def _():
        y = _rms(x_ref[...], n1_ref[...])
        h_scr[...] = (y * (1.0 + sc_ref[...]) + sh_ref[...]).astype(BF16)

    p = _dot(h_scr[...], w_ref[...].astype(BF16))

    def qk_norm(g):
        ms = _group_sumsq(p, bd_ref[...]) * (1.0 / QK_DIM)
        return p * lax.rsqrt(ms + NORM_EPS) * g

    @pl.when(j < 2)
    def _():
        q_ref[...] = (qk_norm(gq_ref[...]) * (QK_DIM ** -0.5)).astype(BF16)

    @pl.when(j == 2)
    def _():
        k_ref[...] = qk_norm(gk_ref[...])

    @pl.when(j == 3)
    def _():
        v_ref[...] = p

    @pl.when((j == 4) | (j == 5) | (j >= 10))
    def _():
        hm_ref[...] = p * _sigmoid(p)

    @pl.when((j == 6) | (j == 7))
    def _():
        a = lb_ref[...]
        e = jnp.exp(a - jnp.max(a, axis=0, keepdims=True))
        lb = e[0:1] / jnp.sum(e, axis=0, keepdims=True)
        hm_ref[...] = jnp.log(lb + (1.0 - lb) * _sigmoid(p))
        hk_ref[...] = (1.0 - lb) * _sigmoid(-p)

    @pl.when((j == 8) | (j == 9))
    def _():
        hm_ref[...] = p


def _in_proj(x2d, n1, sc, sh, w_in, gq, gk, hlb, tm, rows_per_mod):
    n, d = x2d.shape
    cols = w_in.shape[1]
    nj = cols // IN_TN
    assert nj == 12 and n % tm == 0
    r = sc.shape[1]
    tiles_per_mod = rows_per_mod // tm
    bd = jnp.asarray(np.kron(np.eye(MXU_DIM // QK_DIM), np.ones((QK_DIM, QK_DIM))), BF16)
    clamp = lambda j, lo: jnp.clip(j - lo, 0, 1)
    mod_spec = pl.BlockSpec((None, r, d), lambda i, j: (i // tiles_per_mod, 0, 0))
    return pl.pallas_call(
        _in_kernel,
        grid=(n // tm, nj),
        in_specs=[
            pl.BlockSpec((tm, d), lambda i, j: (i, 0)),
            pl.BlockSpec((1, d), lambda i, j: (0, 0)),
            mod_spec, mod_spec,
            pl.BlockSpec((d, IN_TN), lambda i, j: (0, j)),
            pl.BlockSpec((1, IN_TN), lambda i, j: (0, 0)),
            pl.BlockSpec((1, IN_TN), lambda i, j: (0, 0)),
            pl.BlockSpec((hlb.shape[0], IN_TN), lambda i, j: (0, clamp(j, 6))),
            pl.BlockSpec((MXU_DIM, MXU_DIM), lambda i, j: (0, 0)),
        ],
        out_specs=[
            pl.BlockSpec((tm, IN_TN), lambda i, j: (i, clamp(j, 0))),
            pl.BlockSpec((tm, IN_TN), lambda i, j: (i, 0)),
            pl.BlockSpec((tm, IN_TN), lambda i, j: (i, 0)),
            pl.BlockSpec((tm, IN_TN), lambda i, j: (i, jnp.clip(j - 4, 0, 7))),
            pl.BlockSpec((tm, IN_TN), lambda i, j: (i, clamp(j, 6))),
        ],
        out_shape=[
            jax.ShapeDtypeStruct((n, 2 * IN_TN), BF16),
            jax.ShapeDtypeStruct((n, IN_TN), F32),
            jax.ShapeDtypeStruct((n, IN_TN), F32),
            jax.ShapeDtypeStruct((n, 8 * IN_TN), F32),
            jax.ShapeDtypeStruct((n, 2 * IN_TN), F32),
        ],
        scratch_shapes=[pltpu.VMEM((tm, d), BF16)],
        compiler_params=_cparams(("arbitrary", "arbitrary")),
        name="in_proj",
    )(x2d, n1.reshape(1, d), sc, sh, w_in,
      jnp.tile(gq.reshape(1, QK_DIM), (1, IN_TN // QK_DIM)),
      jnp.tile(gk.reshape(1, QK_DIM), (1, IN_TN // QK_DIM)), hlb, bd)


def _alibi_slopes_np():
    h = np.arange(1, ATTN_HEADS + 1, dtype=np.float64)
    return np.exp2(-ALIBI_MAX_BIAS * h / ATTN_HEADS).reshape(ATTN_KV_HEADS, ATTN_GROUP).astype(np.float32)


def _alibi_slopes():
    out = np.zeros((ATTN_KV_HEADS, 1, LANES), np.float32)
    out[:, 0, :ATTN_GROUP] = _alibi_slopes_np()
    return jnp.asarray(out)


def _lambda(lam_ref, lam_init):
    l = lam_ref[...]
    s1 = jnp.sum(l[0:1] * l[1:2], axis=-1, keepdims=True)
    s2 = jnp.sum(l[2:3] * l[3:4], axis=-1, keepdims=True)
    return jnp.exp(s1) - jnp.exp(s2) + lam_init


def _diff_finish(o_all, t, lam, subln, lam_init, o_ref):
    for g in range(ATTN_GROUP):
        o = o_all[(2 * g) * t:(2 * g + 1) * t] - lam * o_all[(2 * g + 1) * t:(2 * g + 2) * t]
        y = _rms(o, subln) * (1.0 - lam_init)
        o_ref[:, g * LANES:(g + 1) * LANES] = y.astype(o_ref.dtype)


def _softmax_step_t(s, m_old, l_old):
    m_new = jnp.maximum(m_old, jnp.max(s, axis=0, keepdims=True))
    alpha = jnp.exp(m_old - m_new)
    p = jnp.exp(s - m_new)
    return m_new, alpha, alpha * l_old + jnp.sum(p, axis=0, keepdims=True), p.astype(BF16)


def _pattn_kernel(sl_ref, lam_ref, subln_ref, q_ref, k_ref, v_ref, o_ref, kb, vt, qs, s_a, s_b, m_scr, l_scr, acc,
                  *, tq, nblk, lam_init):
    qi = pl.program_id(2)

    @pl.when(qi == 0)
    def _():
        t = k_ref.shape[0]
        pos = lax.broadcasted_iota(jnp.int32, (t, LANES), 0)
        ln = lax.broadcasted_iota(jnp.int32, (t, LANES), 1)
        aug = jnp.where(ln == 0, pos & ~(LANES - 1), jnp.where(ln == 1, pos & (LANES - 1), 0))
        kb[:, :LANES] = k_ref[...].astype(BF16)
        kb[:, LANES:] = aug.astype(F32).astype(BF16)
        for jb in range(t // tq):
            vt[jb] = v_ref[jb * tq:(jb + 1) * tq, :].T.astype(BF16)

    rows = 4 * tq
    half = 2 * tq
    q = q_ref[...]
    lane = lax.broadcasted_iota(jnp.int32, (tq, LANES), 1)
    zero = jnp.zeros((tq, LANES), BF16)
    for g in range(ATTN_GROUP):
        qg = q[:, g * LANES:(g + 1) * LANES]
        aug = jnp.where(lane < 2, sl_ref[0:1, g:g + 1], 0.0).astype(BF16)
        qs[(2 * g) * tq:(2 * g + 1) * tq, :] = jnp.concatenate([jnp.where(lane < QK_DIM, qg, zero), aug], axis=1)
        qs[(2 * g + 1) * tq:(2 * g + 2) * tq, :] = jnp.concatenate([jnp.where(lane >= QK_DIM, qg, zero), aug], axis=1)
    t_q = lax.broadcasted_iota(jnp.int32, (1, half), 1) % tq
    m_scr[...] = jnp.full(m_scr.shape, -jnp.inf, F32)
    l_scr[...] = jnp.zeros(l_scr.shape, F32)
    acc[...] = jnp.zeros(acc.shape, F32)

    s_bufs = (s_a, s_b)

    def logits(jb):
        s_bufs[jb % 2][...] = _dot_nt(kb[jb * tq:(jb + 1) * tq, :], qs[...])

    def attend(jb, masked):
        for g in range(ATTN_GROUP):
            cs = slice(g * half, (g + 1) * half)
            s = s_bufs[jb % 2][:, cs]
            if masked:
                key = lax.broadcasted_iota(jnp.int32, (tq, 1), 0)
                s = jnp.where(key <= t_q, s, -jnp.inf)
            m_new, alpha, l_new, p = _softmax_step_t(s, m_scr[:, cs], l_scr[:, cs])
            acc[:, cs] = alpha * acc[:, cs] + _dot(vt[jb], p)
            m_scr[:, cs] = m_new
            l_scr[:, cs] = l_new

    logits(0)
    for jb in range(nblk):
        if jb + 1 < nblk:
            @pl.when(jb < qi)
            def _(jb=jb):
                logits(jb + 1)
                attend(jb, False)

        @pl.when(jb == qi)
        def _(jb=jb):
            attend(jb, True)

    o_t = acc[...] * (1.0 / l_scr[...])
    lam = _lambda(lam_ref, lam_init)
    for g in range(ATTN_GROUP):
        d = o_t[:, (2 * g) * tq:(2 * g + 1) * tq] - lam * o_t[:, (2 * g + 1) * tq:(2 * g + 2) * tq]
        y = _rms(d.T, subln_ref[...]) * (1.0 - lam_init)
        o_ref[:, g * LANES:(g + 1) * LANES] = y.astype(o_ref.dtype)


def _prompt_attention(q, k, v, lam4, subln, lam_init, tq=512):
    b, t, _ = q.shape
    tq = min(tq, t)
    assert t % tq == 0 and t < LANES * 256
    rows = 4 * tq
    kern = functools.partial(_pattn_kernel, tq=tq, nblk=t // tq, lam_init=lam_init)
    return pl.pallas_call(
        kern,
        grid=(b, ATTN_KV_HEADS, t // tq),
        in_specs=[
            pl.BlockSpec((None, 1, LANES), lambda bi, h, qi: (h, 0, 0)),
            pl.BlockSpec((4, QK_DIM), lambda bi, h, qi: (0, 0)),
            pl.BlockSpec((1, V_DIM), lambda bi, h, qi: (0, 0)),
            pl.BlockSpec((None, tq, 2 * LANES), lambda bi, h, qi: (bi, qi, h)),
            pl.BlockSpec((None, t, LANES), lambda bi, h, qi: (bi, 0, h)),
            pl.BlockSpec((None, t, LANES), lambda bi, h, qi: (bi, 0, h)),
        ],
        out_specs=pl.BlockSpec((None, tq, 2 * LANES), lambda bi, h, qi: (bi, qi, h)),
        out_shape=jax.ShapeDtypeStruct((b, t, ATTN_HEADS * V_DIM), BF16),
        scratch_shapes=[
            pltpu.VMEM((t, 2 * LANES), BF16), pltpu.VMEM((t // tq, V_DIM, tq), BF16),
            pltpu.VMEM((rows, 2 * LANES), BF16), pltpu.VMEM((tq, rows), F32), pltpu.VMEM((tq, rows), F32),
            pltpu.VMEM((1, rows), F32), pltpu.VMEM((1, rows), F32), pltpu.VMEM((V_DIM, rows), F32),
        ],
        compiler_params=_cparams(("arbitrary", "arbitrary", "arbitrary")),
        name="prompt_attention",
    )(_alibi_slopes(), lam4, subln.reshape(1, V_DIM), q, k, v)


def _to_col(row):
    n = row.shape[1]
    eye = lax.broadcasted_iota(jnp.int32, (n, n), 0) == lax.broadcasted_iota(jnp.int32, (n, n), 1)
    return jnp.sum(jnp.where(eye, row, 0.0), axis=1, keepdims=True)


def _sattn_kernel(pt_ref, srow_ref, lam_ref, subln_ref, qbd_ref, kn_ref, vn_ref, *rest,
                  npg, nj, page, t, t_real, lam_init):
    del pt_ref
    kp_refs, vp_refs = rest[:npg], rest[npg:2 * npg]
    o_ref = rest[2 * npg]
    kc, vc_a, vc_b, s_a, s_b, m_scr, l_scr, acc = rest[2 * npg + 1:]
    bufs = ((vc_a, s_a), (vc_b, s_b))
    j = pl.program_id(1)
    ntok = npg * page
    past = nj * ntok
    nq = qbd_ref.shape[0]
    width = ATTN_KV_HEADS * LANES
    qbd = qbd_ref[...]
    srow = srow_ref[...]

    def logits(vc, s_buf):
        for p in range(npg):
            kc[p * page:(p + 1) * page, :] = kp_refs[p][...].reshape(page, width).astype(BF16)
            vc[p * page:(p + 1) * page, :] = vp_refs[p][...].reshape(page, width).astype(BF16)
        kpos = lax.broadcasted_iota(jnp.int32, (ntok, nq), 0) + j * ntok
        s_buf[...] = _dot_nt(kc[...], qbd) + kpos.astype(F32) * srow

    def attend(vc, s_buf):
        m_new, alpha, l_new, p = _softmax_step_t(s_buf[...], m_scr[...], l_scr[...])
        acc[...] = _to_col(alpha) * acc[...] + _dot_tn(p, vc[...])
        m_scr[...] = m_new
        l_scr[...] = l_new

    @pl.when(j == 0)
    def _():
        m_scr[...] = jnp.full(m_scr.shape, -jnp.inf, F32)
        l_scr[...] = jnp.zeros(l_scr.shape, F32)
        acc[...] = jnp.zeros(acc.shape, F32)
        logits(*bufs[0])

    for parity in range(2):
        @pl.when((j > 0) & (j < nj) & (j % 2 == parity))
        def _(parity=parity):
            logits(*bufs[parity])
            attend(*bufs[1 - parity])

    @pl.when(j == nj)
    def _():
        attend(*bufs[(nj - 1) % 2])
        t_q = lax.broadcasted_iota(jnp.int32, (1, nq), 1) % t
        c = lax.broadcasted_iota(jnp.int32, (t, 1), 0)
        sn = _dot_nt(kn_ref[...].astype(BF16), qbd) + (past + c).astype(F32) * srow
        sn = jnp.where((c <= t_q) & (c < t_real), sn, -jnp.inf)
        m_fin, alpha_n, l_fin, _ = _softmax_step_t(sn, m_scr[...], l_scr[...])
        pn = jnp.exp(sn - m_fin)
        o = _to_col(alpha_n) * acc[...]
        vn = vn_ref[...]
        for u in range(t_real):
            o = o + _to_col(pn[u:u + 1]) * vn[u:u + 1]
        o = o * _to_col(1.0 / l_fin)
        lam = _lambda(lam_ref, lam_init)
        rq = nq // ATTN_KV_HEADS
        for h in range(ATTN_KV_HEADS):
            o_h = o[h * rq:(h + 1) * rq, h * LANES:(h + 1) * LANES]
            _diff_finish(o_h, t, lam, subln_ref[...], lam_init, o_ref.at[:, 2 * h * LANES:(2 * h + 2) * LANES])


def _sample_attention(q, k_new, v_new, t_real, cache_k, cache_v, page_table, lam4, subln, lam_init, npg=16):
    b, t, _ = q.shape
    tpad = t
    page = cache_k.shape[1]
    n_pages = page_table.shape[1]
    npg = min(npg, n_pages)
    assert n_pages % npg == 0
    ntok = npg * page
    width = ATTN_KV_HEADS * LANES
    nq = ATTN_KV_HEADS * ATTN_GROUP * 2 * t
    nj = n_pages // npg
    kern = functools.partial(_sattn_kernel, npg=npg, nj=nj, page=page, t=t, t_real=t_real, lam_init=lam_init)
    q6 = q.reshape(b, t, ATTN_KV_HEADS, ATTN_GROUP, 2, QK_DIM).transpose(0, 2, 3, 4, 1, 5)
    eye_h = jnp.eye(ATTN_KV_HEADS, dtype=q.dtype)[None, :, None, None, None, :, None, None]
    eye_m = jnp.eye(2, dtype=q.dtype)[None, None, None, :, None, None, :, None]
    qbd = (q6[:, :, :, :, :, None, None, :] * eye_h * eye_m).reshape(b, nq, width)
    slope_row = jnp.asarray(np.repeat(_alibi_slopes_np().reshape(-1), 2 * t)[None, :])

    def page_spec(p):
        return pl.BlockSpec((None, page, ATTN_KV_HEADS, LANES),
                            lambda bi, j, pt: (pt[bi * n_pages + jnp.minimum(j, nj - 1) * npg + p], 0, 0, 0))

    grid_spec = pltpu.PrefetchScalarGridSpec(
        num_scalar_prefetch=1,
        grid=(b, nj + 1),
        in_specs=[
            pl.BlockSpec((1, nq), lambda bi, j, pt: (0, 0)),
            pl.BlockSpec((4, QK_DIM), lambda bi, j, pt: (0, 0)),
            pl.BlockSpec((1, V_DIM), lambda bi, j, pt: (0, 0)),
            pl.BlockSpec((None, nq, width), lambda bi, j, pt: (bi, 0, 0)),
            pl.BlockSpec((None, tpad, width), lambda bi, j, pt: (bi, 0, 0)),
            pl.BlockSpec((None, tpad, width), lambda bi, j, pt: (bi, 0, 0)),
        ] + [page_spec(p) for p in range(npg)] * 2,
        out_specs=pl.BlockSpec((None, t, ATTN_HEADS * V_DIM), lambda bi, j, pt: (bi, 0, 0)),
        scratch_shapes=[
            pltpu.VMEM((ntok, width), BF16), pltpu.VMEM((ntok, width), BF16), pltpu.VMEM((ntok, width), BF16),
            pltpu.VMEM((ntok, nq), F32), pltpu.VMEM((ntok, nq), F32),
            pltpu.VMEM((1, nq), F32), pltpu.VMEM((1, nq), F32), pltpu.VMEM((nq, width), F32),
        ],
    )
    return pl.pallas_call(
        kern,
        grid_spec=grid_spec,
        out_shape=jax.ShapeDtypeStruct((b, t, ATTN_HEADS * V_DIM), BF16),
        compiler_params=_cparams(("arbitrary", "arbitrary")),
        name="sample_attention",
    )(page_table.reshape(-1), slope_row, lam4, subln.reshape(1, V_DIM), qbd, k_new, v_new,
      *([cache_k] * npg), *([cache_v] * npg))


def _hgrn_tables(c):
    levels = int(math.log2(c))
    tril = np.tril(np.ones((c, c), np.float32))
    w = [tril]
    mask = [np.eye(c, dtype=np.float32)]
    idx = np.arange(c)
    for l in range(levels):
        bs = 2 << l
        mid = (idx // bs) * bs + bs // 2
        if l < HG_TABLE_LEVELS:
            w.append(tril - tril[mid])
        upper = (idx % bs) >= bs // 2
        same = (idx[:, None] // bs) == (idx[None, :] // bs)
        mask.append((same & upper[:, None] & ~upper[None, :]).astype(np.float32))
    return jnp.asarray(np.concatenate(w, 0), BF16), jnp.asarray(np.stack(mask, 0)), levels


def _minus_mid_rows(g, bs):
    c, n = g.shape
    g3 = g.reshape(c // bs, bs, n)
    return (g3 - g3[:, bs // 2:bs // 2 + 1, :]).reshape(c, n)


def _hgrn_kernel(*refs, c, levels, nchunk, hps, has_s0):
    if has_s0:
        s0_ref, refs = refs[0], refs[1:]
    (w_ref, mask_ref, g_ref, q_ref, lf_ref, v_ref, gate_ref, k_ref, o_ref, sn_ref, st) = refs
    ci = pl.program_id(2)

    @pl.when(ci == 0)
    def _():
        for hh in range(hps):
            st[hh] = s0_ref[hh].T if has_s0 else jnp.zeros((HG_DIM, HG_DIM), F32)

    w = w_ref[...]
    pair = 2 * HG_DIM
    for n in range(nchunk):
        rs = slice(n * c, (n + 1) * c)
        for hh in range(hps):
            hs = slice(hh * HG_DIM, (hh + 1) * HG_DIM)
            if hh % 2 == 0:
                ps = slice(hh * HG_DIM, hh * HG_DIM + pair)
                l_hi, l_mid, l_lo = _split3(lf_ref[rs, ps])
                gx2 = _dot(w, l_hi) + _dot(w, l_mid) + _dot(w, l_lo)
            gx = gx2[:, (hh % 2) * HG_DIM:(hh % 2 + 1) * HG_DIM]
            qs, kk, v = q_ref[rs, hs], k_ref[rs, hs], v_ref[rs, hs]
            g = gx[0:c]
            a = mask_ref[0] * _dot_nt(qs.astype(BF16), kk.astype(BF16))
            for l in range(levels):
                x = gx[(l + 1) * c:(l + 2) * c] if l < HG_TABLE_LEVELS else _minus_mid_rows(g, 2 << l)
                qa = qs * jnp.exp(jnp.minimum(x, 0.0))
                kb = kk * jnp.exp(jnp.minimum(-x, 0.0))
                a = a + mask_ref[l + 1] * _dot_nt(qa.astype(BF16), kb.astype(BF16))
            vb = v.astype(BF16)
            st_old = st[hh]
            o = _dot_nt((qs * jnp.exp(g)).astype(BF16), st_old.astype(BF16)) + _dot(a.astype(BF16), vb)
            g_end = g[c - 1:c]
            kd = kk * jnp.exp(g_end - g)
            st[hh] = st_old * jnp.exp(g_end) + _dot_tn(vb, kd.astype(BF16))
            o_ref[rs, hs] = (_rms(o, g_ref[...]) * gate_ref[rs, hs]).astype(o_ref.dtype)

    @pl.when(ci == pl.num_programs(2) - 1)
    def _():
        for hh in range(hps):
            sn_ref[hh] = st[hh].T


def _hgrn2(hmain, hk, hg_norm_g, s0, c, tc, hps):
    b, t, _ = hmain.shape
    w, mask, levels = _hgrn_tables(c)
    nl = levels + 1
    has_s0 = s0 is not None
    kern = functools.partial(_hgrn_kernel, c=c, levels=levels, nchunk=tc // c, hps=hps, has_s0=has_s0)
    nhg = HG_HEADS // hps
    col = lambda off: pl.BlockSpec((None, tc, hps * HG_DIM), lambda bi, h, ci: (bi, ci, off * nhg + h))
    state_spec = pl.BlockSpec((None, hps, HG_DIM, HG_DIM), lambda bi, h, ci: (bi, h, 0, 0))
    in_specs = [
        pl.BlockSpec(w.shape, lambda bi, h, ci: (0, 0)),
        pl.BlockSpec((nl, c, c), lambda bi, h, ci: (0, 0, 0)),
        pl.BlockSpec((1, HG_DIM), lambda bi, h, ci: (0, 0)),
        col(0), col(1), col(2), col(3), col(0),
    ]
    args = [w, mask, hg_norm_g.reshape(1, HG_DIM), hmain, hmain, hmain, hmain, hk]
    if has_s0:
        in_specs, args = [state_spec] + in_specs, [s0] + args
    return pl.pallas_call(
        kern,
        grid=(b, nhg, t // tc),
        in_specs=in_specs,
        out_specs=[col(0), state_spec],
        out_shape=[jax.ShapeDtypeStruct((b, t, HG_HEADS * HG_DIM), BF16),
                   jax.ShapeDtypeStruct((b, HG_HEADS, HG_DIM, HG_DIM), F32)],
        scratch_shapes=[pltpu.VMEM((hps, HG_DIM, HG_DIM), F32)],
        compiler_params=_cparams(("arbitrary", "arbitrary", "arbitrary")),
        name="hgrn2",
    )(*args)


def _out_kernel(a_ref, m_ref, w_ref, x_ref, ga_ref, o_ref):
    half = a_ref.shape[1]
    mix = _dot(a_ref[...], w_ref[:half, :]) + _dot(m_ref[...], w_ref[half:, :])
    o_ref[...] = x_ref[...] + ga_ref[...] * mix


def _out_proj(a2d, m2d, w_out, x2d, ga, tm, rows_per_mod):
    n, d = x2d.shape
    half = a2d.shape[1]
    r = ga.shape[1]
    tiles_per_mod = rows_per_mod // tm
    return pl.pallas_call(
        _out_kernel,
        grid=(n // tm,),
        in_specs=[
            pl.BlockSpec((tm, half), lambda i: (i, 0)),
            pl.BlockSpec((tm, half), lambda i: (i, 0)),
            pl.BlockSpec((2 * half, d), lambda i: (0, 0)),
            pl.BlockSpec((tm, d), lambda i: (i, 0)),
            pl.BlockSpec((None, r, d), lambda i: (i // tiles_per_mod, 0, 0)),
        ],
        out_specs=pl.BlockSpec((tm, d), lambda i: (i, 0)),
        out_shape=jax.ShapeDtypeStruct((n, d), F32),
        compiler_params=_cparams(("arbitrary",)),
        name="out_proj",
    )(a2d, m2d, w_out.astype(BF16), x2d, ga)


def _router_kernel(x_ref, n2_ref, sc_ref, sh_ref, wr_ref, br_ref, *rest, grouped):
    h = _rms(x_ref[...], n2_ref[...]) * (1.0 + sc_ref[...]) + sh_ref[...]
    if grouped:
        tril_ref, h_ref, idx_ref, rw_ref, cnt_ref, cnt_scr = rest
        h_ref[...] = h
    else:
        h_ref, comb_ref = rest
        h_ref[...] = h.astype(BF16)
    h_hi = h.astype(BF16)
    h_lo = (h - h_hi.astype(F32)).astype(BF16)
    w = wr_ref[...]
    w_hi = w.astype(BF16)
    w_lo = (w - w_hi.astype(F32)).astype(BF16)
    lg = _dot(h_hi, w_hi) + _dot(h_hi, w_lo) + _dot(h_lo, w_hi) + br_ref[...]

    lane = lax.broadcasted_iota(jnp.int32, lg.shape, 1)
    neg = -jnp.inf
    big = jnp.int32(LANES)

    def top(valid):
        m = jnp.max(jnp.where(valid, lg, neg), axis=-1, keepdims=True)
        idx = jnp.min(jnp.where(valid & (lg == m), lane, big), axis=-1, keepdims=True)
        return m, idx

    is_group = lane < N_GROUPS
    gmax, gidx = top(is_group)
    pg_top = 1.0 / jnp.sum(jnp.where(is_group, jnp.exp(lg - gmax), 0.0), axis=-1, keepdims=True)
    lo = ROUTE_LANE0 + EXPERTS_PER_GROUP * gidx
    in_group = (lane >= lo) & (lane < lo + EXPERTS_PER_GROUP)
    l0, e0 = top(in_group)
    l1, e1 = top(in_group & (lane != e0))
    r = jnp.exp(l1 - l0)
    w0 = pg_top / (1.0 + r)
    w1 = w0 * r
    if not grouped:
        comb_ref[...] = jnp.where(lane == e0, w0, 0.0) + jnp.where(lane == e1, w1, 0.0)
        return

    @pl.when(pl.program_id(0) == 0)
    def _():
        cnt_scr[...] = jnp.zeros(cnt_scr.shape, F32)

    is0, is1 = lane == e0, lane == e1
    onehot = jnp.where(is0 | is1, 1.0, 0.0)
    before = _dot(tril_ref[...], onehot.astype(BF16)) + cnt_scr[...]
    rank0 = jnp.sum(jnp.where(is0, before, 0.0), axis=-1, keepdims=True).astype(jnp.int32)
    rank1 = jnp.sum(jnp.where(is1, before, 0.0), axis=-1, keepdims=True).astype(jnp.int32)
    cnt_scr[...] = cnt_scr[...] + jnp.sum(onehot, axis=0, keepdims=True)
    idx_ref[...] = jnp.where(lane == 0, e0 - ROUTE_LANE0, jnp.where(lane == 1, e1 - ROUTE_LANE0,
                             jnp.where(lane == 2, rank0, jnp.where(lane == 3, rank1, 0))))
    rw_ref[...] = jnp.where(lane == 0, w0, jnp.where(lane == 1, w1, 0.0))
    cnt_ref[...] = cnt_scr[...].astype(jnp.int32)


def _router(x1, n2, sc, sh, w_route, b_route, tm, rows_per_mod, grouped):
    n, d = x1.shape
    r = sc.shape[1]
    tiles_per_mod = rows_per_mod // tm
    mod_spec = pl.BlockSpec((None, r, d), lambda i: (i // tiles_per_mod, 0, 0))
    row_spec = lambda w: pl.BlockSpec((tm, w), lambda i: (i, 0))
    in_specs = [row_spec(d), pl.BlockSpec((1, d), lambda i: (0, 0)), mod_spec, mod_spec,
                pl.BlockSpec((d, LANES), lambda i: (0, 0)), pl.BlockSpec((1, LANES), lambda i: (0, 0))]
    args = [x1, n2.reshape(1, d), sc, sh, w_route, b_route]
    if grouped:
        in_specs.append(pl.BlockSpec((tm, tm), lambda i: (0, 0)))
        args.append(jnp.asarray(np.tril(np.ones((tm, tm), np.float32), -1), BF16))
        out_specs = [row_spec(d), row_spec(LANES), row_spec(LANES), pl.BlockSpec((1, LANES), lambda i: (0, 0))]
        out_shape = [jax.ShapeDtypeStruct((n, d), F32), jax.ShapeDtypeStruct((n, LANES), jnp.int32),
                     jax.ShapeDtypeStruct((n, LANES), F32), jax.ShapeDtypeStruct((1, LANES), jnp.int32)]
        scratch = [pltpu.VMEM((1, LANES), F32)]
    else:
        out_specs = [row_spec(d), row_spec(LANES)]
        out_shape = [jax.ShapeDtypeStruct((n, d), BF16), jax.ShapeDtypeStruct((n, LANES), F32)]
        scratch = []
    return pl.pallas_call(
        functools.partial(_router_kernel, grouped=grouped),
        grid=(n // tm,),
        in_specs=in_specs,
        out_specs=out_specs,
        out_shape=out_shape,
        scratch_shapes=scratch,
        compiler_params=_cparams(("arbitrary",)),
        name="router",
    )(*args)


def _dispatch_kernel(pos_ref, h_ref, xs_init, xs_hbm, sem, *, tm):
    del xs_init
    base = pl.program_id(0) * tm

    def body(r, c):
        t = base + r
        src = h_ref.at[pl.ds(r, 1)]
        pltpu.make_async_copy(src, xs_hbm.at[pl.ds(pos_ref[2 * t], 1)], sem).start()
        pltpu.make_async_copy(src, xs_hbm.at[pl.ds(pos_ref[2 * t + 1], 1)], sem).start()
        return c

    lax.fori_loop(0, tm, body, 0)
    for _ in range(2):
        pltpu.make_async_copy(h_ref, xs_hbm.at[pl.ds(0, tm)], sem).wait()


def _dispatch(pos, h, n_rows, tm=512):
    n, w = h.shape
    tm = min(tm, n)
    assert n % tm == 0
    any_spec = pl.BlockSpec(memory_space=pl.ANY)
    return pl.pallas_call(
        functools.partial(_dispatch_kernel, tm=tm),
        grid_spec=pltpu.PrefetchScalarGridSpec(
            num_scalar_prefetch=1, grid=(n // tm,),
            in_specs=[pl.BlockSpec((tm, w), lambda i, p: (i, 0)), any_spec], out_specs=any_spec,
            scratch_shapes=[pltpu.SemaphoreType.DMA(())]),
        out_shape=jax.ShapeDtypeStruct((n_rows, w), F32),
        input_output_aliases={2: 0},
        compiler_params=_cparams(("arbitrary",)),
        name="moe_dispatch",
    )(pos, h, jnp.zeros((n_rows, w), F32))


def _expert_kernel(te_ref, nv_ref, x_ref, wg_ref, wu_ref, wd_ref, y_ref):
    del te_ref
    i = pl.program_id(0)

    @pl.when(i < nv_ref[0])
    def _():
        x = x_ref[...].astype(BF16)
        a = _dot(x, wg_ref[...].astype(BF16))
        u = _dot(x, wu_ref[...].astype(BF16))
        hid = a * _sigmoid(a) * u
        y_ref[...] = _dot(hid.astype(BF16), wd_ref[...].astype(BF16))

    @pl.when(i >= nv_ref[0])
    def _():
        y_ref[...] = jnp.zeros(y_ref.shape, F32)


def _experts(tile_expert, n_valid, xs, w_gate, w_up, w_down, tm):
    rows, half = xs.shape
    ne, d, f = w_gate.shape
    w_spec = lambda a, b: pl.BlockSpec((None, a, b), lambda i, te, nv: (te[i], 0, 0))
    return pl.pallas_call(
        _expert_kernel,
        grid_spec=pltpu.PrefetchScalarGridSpec(
            num_scalar_prefetch=2, grid=(rows // tm,),
            in_specs=[pl.BlockSpec((tm, half), lambda i, te, nv: (i, 0)), w_spec(d, f), w_spec(d, f), w_spec(f, d)],
            out_specs=pl.BlockSpec((tm, d), lambda i, te, nv: (i, 0))),
        out_shape=jax.ShapeDtypeStruct((rows, d), F32),
        compiler_params=_cparams(("arbitrary",)),
        name="moe_experts",
    )(tile_expert, n_valid, xs, w_gate, w_up, w_down)


def _combine_kernel(pos_ref, ys_hbm, x_ref, ga_ref, rw_ref, o_ref, buf, sem, *, tm):
    i = pl.program_id(0)
    n = pl.num_programs(0)

    def issue(tile, slot):
        def body(r, c):
            t = tile * tm + r
            pltpu.make_async_copy(ys_hbm.at[pl.ds(pos_ref[2 * t], 1)], buf.at[slot, pl.ds(r, 1)],
                                  sem.at[slot]).start()
            pltpu.make_async_copy(ys_hbm.at[pl.ds(pos_ref[2 * t + 1], 1)], buf.at[slot, pl.ds(tm + r, 1)],
                                  sem.at[slot]).start()
            return c

        lax.fori_loop(0, tm, body, 0)

    @pl.when(i == 0)
    def _():
        issue(0, 0)

    @pl.when(i + 1 < n)
    def _():
        issue(i + 1, (i + 1) % 2)

    slot = i % 2
    pltpu.make_async_copy(ys_hbm.at[pl.ds(0, 2 * tm)], buf.at[slot], sem.at[slot]).wait()
    w = rw_ref[...]
    y = w[:, 0:1] * buf[slot, 0:tm] + w[:, 1:2] * buf[slot, tm:2 * tm]
    o_ref[...] = x_ref[...] + ga_ref[...] * y


def _combine(pos, ys, x1, ga, rw, tm, rows_per_mod):
    n, d = x1.shape
    r = ga.shape[1]
    tiles_per_mod = rows_per_mod // tm
    return pl.pallas_call(
        functools.partial(_combine_kernel, tm=tm),
        grid_spec=pltpu.PrefetchScalarGridSpec(
            num_scalar_prefetch=1, grid=(n // tm,),
            in_specs=[pl.BlockSpec(memory_space=pl.ANY),
                      pl.BlockSpec((tm, d), lambda i, p: (i, 0)),
                      pl.BlockSpec((None, r, d), lambda i, p: (i // tiles_per_mod, 0, 0)),
                      pl.BlockSpec((tm, LANES), lambda i, p: (i, 0))],
            out_specs=pl.BlockSpec((tm, d), lambda i, p: (i, 0)),
            scratch_shapes=[pltpu.VMEM((2, 2 * tm, d), F32), pltpu.SemaphoreType.DMA((2,))]),
        out_shape=jax.ShapeDtypeStruct((n, d), F32),
        compiler_params=_cparams(("arbitrary",)),
        name="moe_combine",
    )(pos, ys, x1, ga, rw)


def _grouped_moe(hp, idx, rw, cnt, w_gate, w_up, w_down, x1, ga, rows_per_mod, tm_e=256, tm_c=256):
    n = hp.shape[0]
    ne = w_gate.shape[0]
    tm_e = min(tm_e, n)
    counts = cnt[0, ROUTE_LANE0:ROUTE_LANE0 + ne]
    tiles = (counts + tm_e - 1) // tm_e
    tile_end = jnp.cumsum(tiles)
    start_row = (tile_end - tiles) * tm_e
    n_tiles = (2 * n) // tm_e + ne
    tile_expert = jnp.minimum(jnp.sum(tile_end[None, :] <= jnp.arange(n_tiles)[:, None], axis=1), ne - 1)
    pos = (jnp.take(start_row, idx[:, 0:2], axis=0) + idx[:, 2:4]).reshape(-1).astype(jnp.int32)
    xs = _dispatch(pos, hp, n_tiles * tm_e)
    ys = _experts(tile_expert.astype(jnp.int32), tile_end[ne - 1:].astype(jnp.int32), xs, w_gate, w_up, w_down, tm_e)
    return _combine(pos, ys, x1, ga, rw, min(tm_c, n), rows_per_mod)


def _moe_kernel(h_ref, comb_ref, wg_ref, wu_ref, wd_ref, x_ref, ga_ref, o_ref, acc):
    e = pl.program_id(1)

    @pl.when(e == 0)
    def _():
        acc[...] = jnp.zeros(acc.shape, F32)

    h = h_ref[...]
    a = _dot(h, wg_ref[...].astype(BF16))
    u = _dot(h, wu_ref[...].astype(BF16))
    lane = lax.broadcasted_iota(jnp.int32, comb_ref.shape, 1)
    cw = jnp.sum(jnp.where(lane == e + ROUTE_LANE0, comb_ref[...], 0.0), axis=-1, keepdims=True)
    hid = a * _sigmoid(a) * u * cw
    acc[...] += _dot(hid.astype(BF16), wd_ref[...].astype(BF16))

    @pl.when(e == pl.num_programs(1) - 1)
    def _():
        o_ref[...] = x_ref[...] + ga_ref[...] * acc[...]


def _moe(h2, comb, w_gate, w_up, w_down, x1, ga, tm, rows_per_mod):
    n, d = x1.shape
    ne, _, f = w_gate.shape
    r = ga.shape[1]
    tiles_per_mod = rows_per_mod // tm
    return pl.pallas_call(
        _moe_kernel,
        grid=(n // tm, ne),
        in_specs=[
            pl.BlockSpec((tm, d), lambda i, e: (i, 0)),
            pl.BlockSpec((tm, LANES), lambda i, e: (i, 0)),
            pl.BlockSpec((None, d, f), lambda i, e: (e, 0, 0)),
            pl.BlockSpec((None, d, f), lambda i, e: (e, 0, 0)),
            pl.BlockSpec((None, f, d), lambda i, e: (e, 0, 0)),
            pl.BlockSpec((tm, d), lambda i, e: (i, 0)),
            pl.BlockSpec((None, r, d), lambda i, e: (i // tiles_per_mod, 0, 0)),
        ],
        out_specs=pl.BlockSpec((tm, d), lambda i, e: (i, 0)),
        out_shape=jax.ShapeDtypeStruct((n, d), F32),
        scratch_shapes=[pltpu.VMEM((tm, d), F32)],
        compiler_params=_cparams(("arbitrary", "arbitrary")),
        name="moe",
    )(h2, comb, w_gate, w_up, w_down, x1, ga)


def _pick_tile(n, pref):
    t = min(pref, n)
    while n % t:
        t //= 2
    return t


def _layer(x, mods, s0, attend, lw, lam_init, hg_chunk, hg_heads_per_step, per_row_mod):
    (n1, n2, w_in, gq, gk, subln, lam4, hlb, hg_g, w_out, w_route, b_route, w_gate, w_up, w_down) = lw
    b, t, d = x.shape
    n = b * t
    x2d = x.reshape(n, d)
    if per_row_mod:
        tm = n
        rows_per_mod = n
        mods = [jnp.repeat(m, t, axis=0).reshape(1, n, d) for m in mods]
    else:
        tm = _pick_tile(t, 512)
        rows_per_mod = t
        mods = [m.reshape(b, 1, d) for m in mods]
    sh1, sc1, ga1, sh2, sc2, ga2 = mods

    tm_in = tm if per_row_mod else _pick_tile(t, 1024)
    q, k, v, hmain, hk = _in_proj(x2d, n1, sc1, sh1, w_in, gq, gk, hlb, tm_in, rows_per_mod)
    a = attend(q.reshape(b, t, -1), k.reshape(b, t, -1), v.reshape(b, t, -1))
    tpad = -(-t // hg_chunk) * hg_chunk
    hm3, hk3 = hmain.reshape(b, t, -1), hk.reshape(b, t, -1)
    if tpad != t:
        hm3 = jnp.pad(hm3, ((0, 0), (0, tpad - t), (0, 0)))
        hk3 = jnp.pad(hk3, ((0, 0), (0, tpad - t), (0, 0)))
    m, s_new = _hgrn2(hm3, hk3, hg_g, s0, hg_chunk, _pick_tile(tpad, 2 * hg_chunk), hg_heads_per_step)
    m = m[:, :t]
    x1 = _out_proj(a.reshape(n, -1), m.reshape(n, -1), w_out, x2d, ga1, tm, rows_per_mod)
    if per_row_mod:
        h2, comb = _router(x1, n2, sc2, sh2, w_route, b_route, tm, rows_per_mod, False)
        y = _moe(h2, comb, w_gate, w_up, w_down, x1, ga2, tm, rows_per_mod)
    else:
        hp, idx, rw, cnt = _router(x1, n2, sc2, sh2, w_route, b_route, tm, rows_per_mod, True)
        y = _grouped_moe(hp, idx, rw, cnt, w_gate, w_up, w_down, x1, ga2, rows_per_mod)
    return y.reshape(b, t, d), k, v, s_new


def kernel(x_prompt, x_sample, cache_k, cache_v, state_hgrn, page_table, c_prompt, c_sample, norm1_g, norm2_g, w_ada, b_ada, w_in, q_norm_g, k_norm_g, lambda_q1, lambda_k1, lambda_q2, lambda_k2, subln_g, hg_lower_bound, hg_norm_g, w_out, w_router_group, b_router_group, w_router_expert, b_router_expert, w_exp_gate, w_exp_up, w_exp_down):
    depth = norm1_g.shape[0]
    assert depth == 1, "single-layer trunk"
    l = 0
    lam_init = 0.8 - 0.6 * math.exp(-0.3 * l)
    bp, tp, d = x_prompt.shape
    bs, ts, _ = x_sample.shape

    c_all = jnp.concatenate([c_prompt, c_sample], axis=0)
    rpad = -(-c_all.shape[0] // 8) * 8
    c_all = jnp.pad(c_all, ((0, rpad - c_all.shape[0]), (0, 0)))
    mod = _adaln(c_all, w_ada[l], b_ada[l])
    mods_p = [mod[:bp, i * d:(i + 1) * d] for i in range(6)]
    mods_s = [mod[bp:bp + bs, i * d:(i + 1) * d] for i in range(6)]

    w_re = jnp.transpose(w_router_expert[l], (1, 0, 2)).reshape(d, N_EXPERTS)
    w_route = jnp.pad(jnp.concatenate([w_router_group[l], w_re], axis=1), ((0, 0), (0, LANES - N_GROUPS - N_EXPERTS)))
    b_route = jnp.pad(jnp.concatenate([b_router_group[l], b_router_expert[l].reshape(-1)]),
                      (0, LANES - N_GROUPS - N_EXPERTS)).reshape(1, LANES)
    lam4 = jnp.stack([lambda_q1[l], lambda_k1[l], lambda_q2[l], lambda_k2[l]], axis=0)

    lw = (norm1_g[l], norm2_g[l], w_in[l], q_norm_g[l], k_norm_g[l], subln_g[l], lam4, hg_lower_bound,
          hg_norm_g[l], w_out[l], w_route, b_route, w_exp_gate[l], w_exp_up[l], w_exp_down[l])

    def attend_prompt(q, k, v):
        return _prompt_attention(q, k, v, lam4, subln_g[l], lam_init)

    def attend_sample(q, k, v):
        pad = lambda a: jnp.pad(a, ((0, 0), (0, -ts % 8), (0, 0)))
        o = _sample_attention(pad(q), pad(k), pad(v), ts, cache_k[l], cache_v[l], page_table, lam4,
                              subln_g[l], lam_init)
        return o[:, :ts]

    yp, kp, vp, sp = _layer(x_prompt, mods_p, None, attend_prompt, lw, lam_init, 128, 4, False)
    ys, kn, vn, sn = _layer(x_sample, mods_s, state_hgrn[l], attend_sample, lw, lam_init, 8, HG_HEADS, True)

    kv_shape = lambda b, t: (1, b, t, ATTN_KV_HEADS, 2 * QK_DIM)
    return (yp, ys,
            kp.reshape(kv_shape(bp, tp)), vp.reshape(kv_shape(bp, tp)),
            kn.reshape(kv_shape(bs, ts)), vn.reshape(kv_shape(bs, ts)),
            sp[None], sn[None])
```

```python
import functools
import math

import jax
import jax.numpy as jnp
import numpy as np
from jax import lax
from jax.experimental import pallas as pl
from jax.experimental.pallas import tpu as pltpu

F32 = jnp.float32
BF16 = jnp.bfloat16

QK_DIM = 64
V_DIM = 128
ATTN_KV_HEADS = 4
ATTN_GROUP = 2
ATTN_HEADS = ATTN_KV_HEADS * ATTN_GROUP
HG_HEADS = 8
HG_DIM = 128
N_GROUPS = 4
EXPERTS_PER_GROUP = 4
N_EXPERTS = N_GROUPS * EXPERTS_PER_GROUP
NORM_EPS = 1e-6
ALIBI_MAX_BIAS = 8.0
LANES = 128
MXU_DIM = 256
VMEM_LIMIT = 60 * 1024 * 1024
ROUTE_LANE0 = N_GROUPS
HG_TABLE_LEVELS = 2


def _cparams(sem):
    return pltpu.CompilerParams(dimension_semantics=sem, vmem_limit_bytes=VMEM_LIMIT)


def _dot(a, b):
    return jnp.dot(a, b, preferred_element_type=F32)


def _dot_nt(a, b):
    return lax.dot_general(a, b, (((1,), (1,)), ((), ())), preferred_element_type=F32)


def _dot_tn(a, b):
    return lax.dot_general(a, b, (((0,), (0,)), ((), ())), preferred_element_type=F32)


def _split3(x):
    hi = x.astype(BF16)
    r = x - hi.astype(F32)
    mid = r.astype(BF16)
    lo = (r - mid.astype(F32)).astype(BF16)
    return hi, mid, lo


def _sigmoid(x):
    return 1.0 / (1.0 + jnp.exp(-x))


def _rms(x, g):
    return x * lax.rsqrt(jnp.mean(x * x, axis=-1, keepdims=True) + NORM_EPS) * g


def _ada_kernel(c_ref, w_ref, b_ref, o_ref):
    c = c_ref[...]
    s = c * _sigmoid(c)
    o_ref[...] = _dot(s.astype(BF16), w_ref[...].astype(BF16)) + b_ref[...]


def _adaln(c_all, w_ada, b_ada, tn=1024):
    r, d = c_all.shape
    n = w_ada.shape[1]
    return pl.pallas_call(
        _ada_kernel,
        grid=(n // tn,),
        in_specs=[
            pl.BlockSpec((r, d), lambda j: (0, 0)),
            pl.BlockSpec((d, tn), lambda j: (0, j)),
            pl.BlockSpec((1, tn), lambda j: (0, j)),
        ],
        out_specs=pl.BlockSpec((r, tn), lambda j: (0, j)),
        out_shape=jax.ShapeDtypeStruct((r, n), F32),
        compiler_params=_cparams(("arbitrary",)),
        name="adaln",
    )(c_all, w_ada, b_ada.reshape(1, n))


IN_TN = 512


def _group_sumsq(p, bd):
    x2 = p * p
    hi = x2.astype(BF16)
    lo = (x2 - hi.astype(F32)).astype(BF16)
    outs = []
    for c in range(p.shape[1] // MXU_DIM):
        sl = slice(c * MXU_DIM, (c + 1) * MXU_DIM)
        outs.append(_dot(hi[:, sl], bd) + _dot(lo[:, sl], bd))
    return jnp.concatenate(outs, axis=1)


def _in_kernel(x_ref, n1_ref, sc_ref, sh_ref, w_ref, gq_ref, gk_ref, lb_ref, bd_ref,
               q_ref, k_ref, v_ref, hm_ref, hk_ref, h_scr):
    j = pl.program_id(1)

    @pl.when(j == 0)
    def _():
        y = _rms(x_ref[...], n1_ref[...])
        h_scr[...] = (y * (1.0 + sc_ref[...]) + sh_ref[...]).astype(BF16)

    p = _dot(h_scr[...], w_ref[...].astype(BF16))

    def qk_norm(g):
        ms = _group_sumsq(p, bd_ref[...]) * (1.0 / QK_DIM)
        return p * lax.rsqrt(ms + NORM_EPS) * g

    @pl.when(j < 2)
    def _():
        q_ref[...] = (qk_norm(gq_ref[...]) * (QK_DIM ** -0.5)).astype(BF16)

    @pl.when(j == 2)
    def _():
        k_ref[...] = qk_norm(gk_ref[...])

    @pl.when(j == 3)
    def _():
        v_ref[...] = p

    @pl.when((j == 4) | (j == 5) | (j >= 10))
    def _():
        hm_ref[...] = p * _sigmoid(p)

    @pl.when((j == 6) | (j == 7))
    def _():
        a = lb_ref[...]
        e = jnp.exp(a - jnp.max(a, axis=0, keepdims=True))
        lb = e[0:1] / jnp.sum(e, axis=0, keepdims=True)
        hm_ref[...] = jnp.log(lb + (1.0 - lb) * _sigmoid(p))
        hk_ref[...] = (1.0 - lb) * _sigmoid(-p)

    @pl.when((j == 8) | (j == 9))
    def _():
        hm_ref[...] = p


def _in_proj(x2d, n1, sc, sh, w_in, gq, gk, hlb, tm, rows_per_mod):
    n, d = x2d.shape
    cols = w_in.shape[1]
    nj = cols // IN_TN
    assert nj == 12 and n % tm == 0
    r = sc.shape[1]
    tiles_per_mod = rows_per_mod // tm
    bd = jnp.asarray(np.kron(np.eye(MXU_DIM // QK_DIM), np.ones((QK_DIM, QK_DIM))), BF16)
    clamp = lambda j, lo: jnp.clip(j - lo, 0, 1)
    mod_spec = pl.BlockSpec((None, r, d), lambda i, j: (i // tiles_per_mod, 0, 0))
    return pl.pallas_call(
        _in_kernel,
        grid=(n // tm, nj),
        in_specs=[
            pl.BlockSpec((tm, d), lambda i, j: (i, 0)),
            pl.BlockSpec((1, d), lambda i, j: (0, 0)),
            mod_spec, mod_spec,
            pl.BlockSpec((d, IN_TN), lambda i, j: (0, j)),
            pl.BlockSpec((1, IN_TN), lambda i, j: (0, 0)),
            pl.BlockSpec((1, IN_TN), lambda i, j: (0, 0)),
            pl.BlockSpec((hlb.shape[0], IN_TN), lambda i, j: (0, clamp(j, 6))),
            pl.BlockSpec((MXU_DIM, MXU_DIM), lambda i, j: (0, 0)),
        ],
        out_specs=[
            pl.BlockSpec((tm, IN_TN), lambda i, j: (i, clamp(j, 0))),
            pl.BlockSpec((tm, IN_TN), lambda i, j: (i, 0)),
            pl.BlockSpec((tm, IN_TN), lambda i, j: (i, 0)),
            pl.BlockSpec((tm, IN_TN), lambda i, j: (i, jnp.clip(j - 4, 0, 7))),
            pl.BlockSpec((tm, IN_TN), lambda i, j: (i, clamp(j, 6))),
        ],
        out_shape=[
            jax.ShapeDtypeStruct((n, 2 * IN_TN), BF16),
            jax.ShapeDtypeStruct((n, IN_TN), F32),
            jax.ShapeDtypeStruct((n, IN_TN), F32),
            jax.ShapeDtypeStruct((n, 8 * IN_TN), F32),
            jax.ShapeDtypeStruct((n, 2 * IN_TN), F32),
        ],
        scratch_shapes=[pltpu.VMEM((tm, d), BF16)],
        compiler_params=_cparams(("arbitrary", "arbitrary")),
        name="in_proj",
    )(x2d, n1.reshape(1, d), sc, sh, w_in,
      jnp.tile(gq.reshape(1, QK_DIM), (1, IN_TN // QK_DIM)),
      jnp.tile(gk.reshape(1, QK_DIM), (1, IN_TN // QK_DIM)), hlb, bd)


def _alibi_slopes_np():
    h = np.arange(1, ATTN_HEADS + 1, dtype=np.float64)
    return np.exp2(-ALIBI_MAX_BIAS * h / ATTN_HEADS).reshape(ATTN_KV_HEADS, ATTN_GROUP).astype(np.float32)


def _alibi_slopes():
    out = np.zeros((ATTN_KV_HEADS, 1, LANES), np.float32)
    out[:, 0, :ATTN_GROUP] = _alibi_slopes_np()
    return jnp.asarray(out)


def _lambda(lam_ref, lam_init):
    l = lam_ref[...]
    s1 = jnp.sum(l[0:1] * l[1:2], axis=-1, keepdims=True)
    s2 = jnp.sum(l[2:3] * l[3:4], axis=-1, keepdims=True)
    return jnp.exp(s1) - jnp.exp(s2) + lam_init


def _diff_finish(o_all, t, lam, subln, lam_init, o_ref):
    for g in range(ATTN_GROUP):
        o = o_all[(2 * g) * t:(2 * g + 1) * t] - lam * o_all[(2 * g + 1) * t:(2 * g + 2) * t]
        y = _rms(o, subln) * (1.0 - lam_init)
        o_ref[:, g * LANES:(g + 1) * LANES] = y.astype(o_ref.dtype)


def _softmax_step_t(s, m_old, l_old):
    m_new = jnp.maximum(m_old, jnp.max(s, axis=0, keepdims=True))
    alpha = jnp.exp(m_old - m_new)
    p = jnp.exp(s - m_new)
    return m_new, alpha, alpha * l_old + jnp.sum(p, axis=0, keepdims=True), p.astype(BF16)


def _pattn_kernel(sl_ref, lam_ref, subln_ref, q_ref, k_ref, v_ref, o_ref, kb, vt, qs, s_a, s_b, m_scr, l_scr, acc,
                  *, tq, nblk, lam_init):
    qi = pl.program_id(2)

    @pl.when(qi == 0)
    def _():
        t = k_ref.shape[0]
        pos = lax.broadcasted_iota(jnp.int32, (t, LANES), 0)
        ln = lax.broadcasted_iota(jnp.int32, (t, LANES), 1)
        aug = jnp.where(ln == 0, pos & ~(LANES - 1), jnp.where(ln == 1, pos & (LANES - 1), 0))
        kb[:, :LANES] = k_ref[...].astype(BF16)
        kb[:, LANES:] = aug.astype(F32).astype(BF16)
        for jb in range(t // tq):
            vt[jb] = v_ref[jb * tq:(jb + 1) * tq, :].T.astype(BF16)

    rows = 4 * tq
    half = 2 * tq
    q = q_ref[...]
    lane = lax.broadcasted_iota(jnp.int32, (tq, LANES), 1)
    zero = jnp.zeros((tq, LANES), BF16)
    for g in range(ATTN_GROUP):
        qg = q[:, g * LANES:(g + 1) * LANES]
        aug = jnp.where(lane < 2, sl_ref[0:1, g:g + 1], 0.0).astype(BF16)
        qs[(2 * g) * tq:(2 * g + 1) * tq, :] = jnp.concatenate([jnp.where(lane < QK_DIM, qg, zero), aug], axis=1)
        qs[(2 * g + 1) * tq:(2 * g + 2) * tq, :] = jnp.concatenate([jnp.where(lane >= QK_DIM, qg, zero), aug], axis=1)
    t_q = lax.broadcasted_iota(jnp.int32, (1, half), 1) % tq
    m_scr[...] = jnp.full(m_scr.shape, -jnp.inf, F32)
    l_scr[...] = jnp.zeros(l_scr.shape, F32)
    acc[...] = jnp.zeros(acc.shape, F32)

    s_bufs = (s_a, s_b)

    def logits(jb):
        s_bufs[jb % 2][...] = _dot_nt(kb[jb * tq:(jb + 1) * tq, :], qs[...])

    def attend(jb, masked):
        for g in range(ATTN_GROUP):
            cs = slice(g * half, (g + 1) * half)
            s = s_bufs[jb % 2][:, cs]
            if masked:
                key = lax.broadcasted_iota(jnp.int32, (tq, 1), 0)
                s = jnp.where(key <= t_q, s, -jnp.inf)
            m_new, alpha, l_new, p = _softmax_step_t(s, m_scr[:, cs], l_scr[:, cs])
            acc[:, cs] = alpha * acc[:, cs] + _dot(vt[jb], p)
            m_scr[:, cs] = m_new
            l_scr[:, cs] = l_new

    logits(0)
    for jb in range(nblk):
        if jb + 1 < nblk:
            @pl.when(jb < qi)
            def _(jb=jb):
                logits(jb + 1)
                attend(jb, False)

        @pl.when(jb == qi)
        def _(jb=jb):
            attend(jb, True)

    o_t = acc[...] * (1.0 / l_scr[...])
    lam = _lambda(lam_ref, lam_init)
    for g in range(ATTN_GROUP):
        d = o_t[:, (2 * g) * tq:(2 * g + 1) * tq] - lam * o_t[:, (2 * g + 1) * tq:(2 * g + 2) * tq]
        y = _rms(d.T, subln_ref[...]) * (1.0 - lam_init)
        o_ref[:, g * LANES:(g + 1) * LANES] = y.astype(o_ref.dtype)


def _prompt_attention(q, k, v, lam4, subln, lam_init, tq=512):
    b, t, _ = q.shape
    tq = min(tq, t)
    assert t % tq == 0 and t < LANES * 256
    rows = 4 * tq
    kern = functools.partial(_pattn_kernel, tq=tq, nblk=t // tq, lam_init=lam_init)
    return pl.pallas_call(
        kern,
        grid=(b, ATTN_KV_HEADS, t // tq),
        in_specs=[
            pl.BlockSpec((None, 1, LANES), lambda bi, h, qi: (h, 0, 0)),
            pl.BlockSpec((4, QK_DIM), lambda bi, h, qi: (0, 0)),
            pl.BlockSpec((1, V_DIM), lambda bi, h, qi: (0, 0)),
            pl.BlockSpec((None, tq, 2 * LANES), lambda bi, h, qi: (bi, qi, h)),
            pl.BlockSpec((None, t, LANES), lambda bi, h, qi: (bi, 0, h)),
            pl.BlockSpec((None, t, LANES), lambda bi, h, qi: (bi, 0, h)),
        ],
        out_specs=pl.BlockSpec((None, tq, 2 * LANES), lambda bi, h, qi: (bi, qi, h)),
        out_shape=jax.ShapeDtypeStruct((b, t, ATTN_HEADS * V_DIM), BF16),
        scratch_shapes=[
            pltpu.VMEM((t, 2 * LANES), BF16), pltpu.VMEM((t // tq, V_DIM, tq), BF16),
            pltpu.VMEM((rows, 2 * LANES), BF16), pltpu.VMEM((tq, rows), F32), pltpu.VMEM((tq, rows), F32),
            pltpu.VMEM((1, rows), F32), pltpu.VMEM((1, rows), F32), pltpu.VMEM((V_DIM, rows), F32),
        ],
        compiler_params=_cparams(("arbitrary", "arbitrary", "arbitrary")),
        name="prompt_attention",
    )(_alibi_slopes(), lam4, subln.reshape(1, V_DIM), q, k, v)


def _to_col(row):
    n = row.shape[1]
    eye = lax.broadcasted_iota(jnp.int32, (n, n), 0) == lax.broadcasted_iota(jnp.int32, (n, n), 1)
    return jnp.sum(jnp.where(eye, row, 0.0), axis=1, keepdims=True)


def _sattn_kernel(pt_ref, srow_ref, lam_ref, subln_ref, qbd_ref, kn_ref, vn_ref, *rest,
                  npg, nj, page, t, t_real, lam_init):
    del pt_ref
    kp_refs, vp_refs = rest[:npg], rest[npg:2 * npg]
    o_ref = rest[2 * npg]
    kc, vc, m_scr, l_scr, acc = rest[2 * npg + 1:]
    j = pl.program_id(1)
    ntok = npg * page
    past = nj * ntok
    nq = qbd_ref.shape[0]
    width = ATTN_KV_HEADS * LANES
    qbd = qbd_ref[...]
    srow = srow_ref[...]

    @pl.when(j == 0)
    def _():
        m_scr[...] = jnp.full(m_scr.shape, -jnp.inf, F32)
        l_scr[...] = jnp.zeros(l_scr.shape, F32)
        acc[...] = jnp.zeros(acc.shape, F32)

    for p in range(npg):
        kc[p * page:(p + 1) * page, :] = kp_refs[p][...].reshape(page, width).astype(BF16)
    kpos = lax.broadcasted_iota(jnp.int32, (ntok, nq), 0) + j * ntok
    hw = ntok // 2
    s = jnp.concatenate([_dot_nt(kc[:hw, :], qbd), _dot_nt(kc[hw:, :], qbd)], axis=0)
    s = s + kpos.astype(F32) * srow
    for p in range(npg):
        vc[p * page:(p + 1) * page, :] = vp_refs[p][...].reshape(page, width).astype(BF16)
    m_new, alpha, l_new, p = _softmax_step_t(s, m_scr[...], l_scr[...])
    pv = _dot_tn(p[:hw], vc[:hw, :]) + _dot_tn(p[hw:], vc[hw:, :])
    acc[...] = _to_col(alpha) * acc[...] + pv
    m_scr[...] = m_new
    l_scr[...] = l_new

    @pl.when(j == nj - 1)
    def _():
        t_q = lax.broadcasted_iota(jnp.int32, (1, nq), 1) % t
        c = lax.broadcasted_iota(jnp.int32, (t, 1), 0)
        sn = _dot_nt(kn_ref[...].astype(BF16), qbd) + (past + c).astype(F32) * srow
        sn = jnp.where((c <= t_q) & (c < t_real), sn, -jnp.inf)
        m_fin, alpha_n, l_fin, _ = _softmax_step_t(sn, m_scr[...], l_scr[...])
        pn = jnp.exp(sn - m_fin)
        o = _to_col(alpha_n) * acc[...]
        vn = vn_ref[...]
        for u in range(t_real):
            o = o + _to_col(pn[u:u + 1]) * vn[u:u + 1]
        o = o * _to_col(1.0 / l_fin)
        lam = _lambda(lam_ref, lam_init)
        rq = nq // ATTN_KV_HEADS
        for h in range(ATTN_KV_HEADS):
            o_h = o[h * rq:(h + 1) * rq, h * LANES:(h + 1) * LANES]
            _diff_finish(o_h, t, lam, subln_ref[...], lam_init, o_ref.at[:, 2 * h * LANES:(2 * h + 2) * LANES])


def _sample_attention(q, k_new, v_new, t_real, cache_k, cache_v, page_table, lam4, subln, lam_init, npg=16):
    b, t, _ = q.shape
    tpad = t
    page = cache_k.shape[1]
    n_pages = page_table.shape[1]
    npg = min(npg, n_pages)
    assert n_pages % npg == 0
    ntok = npg * page
    width = ATTN_KV_HEADS * LANES
    nq = ATTN_KV_HEADS * ATTN_GROUP * 2 * t
    nj = n_pages // npg
    kern = functools.partial(_sattn_kernel, npg=npg, nj=nj, page=page, t=t, t_real=t_real, lam_init=lam_init)
    q6 = q.reshape(b, t, ATTN_KV_HEADS, ATTN_GROUP, 2, QK_DIM).transpose(0, 2, 3, 4, 1, 5)
    eye_h = jnp.eye(ATTN_KV_HEADS, dtype=q.dtype)[None, :, None, None, None, :, None, None]
    eye_m = jnp.eye(2, dtype=q.dtype)[None, None, None, :, None, None, :, None]
    qbd = (q6[:, :, :, :, :, None, None, :] * eye_h * eye_m).reshape(b, nq, width)
    slope_row = jnp.asarray(np.repeat(_alibi_slopes_np().reshape(-1), 2 * t)[None, :])

    def page_spec(p):
        return pl.BlockSpec((None, page, ATTN_KV_HEADS, LANES),
                            lambda bi, j, pt: (pt[bi * n_pages + j * npg + p], 0, 0, 0))

    grid_spec = pltpu.PrefetchScalarGridSpec(
        num_scalar_prefetch=1,
        grid=(b, nj),
        in_specs=[
            pl.BlockSpec((1, nq), lambda bi, j, pt: (0, 0)),
            pl.BlockSpec((4, QK_DIM), lambda bi, j, pt: (0, 0)),
            pl.BlockSpec((1, V_DIM), lambda bi, j, pt: (0, 0)),
            pl.BlockSpec((None, nq, width), lambda bi, j, pt: (bi, 0, 0)),
            pl.BlockSpec((None, tpad, width), lambda bi, j, pt: (bi, 0, 0)),
            pl.BlockSpec((None, tpad, width), lambda bi, j, pt: (bi, 0, 0)),
        ] + [page_spec(p) for p in range(npg)] * 2,
        out_specs=pl.BlockSpec((None, t, ATTN_HEADS * V_DIM), lambda bi, j, pt: (bi, 0, 0)),
        scratch_shapes=[
            pltpu.VMEM((ntok, width), BF16), pltpu.VMEM((ntok, width), BF16),
            pltpu.VMEM((1, nq), F32), pltpu.VMEM((1, nq), F32), pltpu.VMEM((nq, width), F32),
        ],
    )
    return pl.pallas_call(
        kern,
        grid_spec=grid_spec,
        out_shape=jax.ShapeDtypeStruct((b, t, ATTN_HEADS * V_DIM), BF16),
        compiler_params=_cparams(("arbitrary", "arbitrary")),
        name="sample_attention",
    )(page_table.reshape(-1), slope_row, lam4, subln.reshape(1, V_DIM), qbd, k_new, v_new,
      *([cache_k] * npg), *([cache_v] * npg))


def _hgrn_tables(c):
    levels = int(math.log2(c))
    tril = np.tril(np.ones((c, c), np.float32))
    w = [tril]
    mask = [np.eye(c, dtype=np.float32)]
    idx = np.arange(c)
    for l in range(levels):
        bs = 2 << l
        mid = (idx // bs) * bs + bs // 2
        if l < HG_TABLE_LEVELS:
            w.append(tril - tril[mid])
        upper = (idx % bs) >= bs // 2
        same = (idx[:, None] // bs) == (idx[None, :] // bs)
        mask.append((same & upper[:, None] & ~upper[None, :]).astype(np.float32))
    return jnp.asarray(np.concatenate(w, 0), BF16), jnp.asarray(np.stack(mask, 0)), levels


def _minus_mid_rows(g, bs):
    c, n = g.shape
    g3 = g.reshape(c // bs, bs, n)
    return (g3 - g3[:, bs // 2:bs // 2 + 1, :]).reshape(c, n)


def _hgrn_kernel(*refs, c, levels, nchunk, hps, has_s0):
    if has_s0:
        s0_ref, refs = refs[0], refs[1:]
    (w_ref, mask_ref, g_ref, q_ref, lf_ref, v_ref, gate_ref, k_ref, o_ref, sn_ref, st) = refs
    ci = pl.program_id(2)

    @pl.when(ci == 0)
    def _():
        for hh in range(hps):
            st[hh] = s0_ref[hh].T if has_s0 else jnp.zeros((HG_DIM, HG_DIM), F32)

    w = w_ref[...]
    pair = 2 * HG_DIM
    for n in range(nchunk):
        rs = slice(n * c, (n + 1) * c)
        for hh in range(hps):
            hs = slice(hh * HG_DIM, (hh + 1) * HG_DIM)
            if hh % 2 == 0:
                ps = slice(hh * HG_DIM, hh * HG_DIM + pair)
                l_hi, l_mid, l_lo = _split3(lf_ref[rs, ps])
                gx2 = _dot(w, l_hi) + _dot(w, l_mid) + _dot(w, l_lo)
            gx = gx2[:, (hh % 2) * HG_DIM:(hh % 2 + 1) * HG_DIM]
            qs, kk, v = q_ref[rs, hs], k_ref[rs, hs], v_ref[rs, hs]
            g = gx[0:c]
            a = mask_ref[0] * _dot_nt(qs.astype(BF16), kk.astype(BF16))
            for l in range(levels):
                x = gx[(l + 1) * c:(l + 2) * c] if l < HG_TABLE_LEVELS else _minus_mid_rows(g, 2 << l)
                qa = qs * jnp.exp(jnp.minimum(x, 0.0))
                kb = kk * jnp.exp(jnp.minimum(-x, 0.0))
                a = a + mask_ref[l + 1] * _dot_nt(qa.astype(BF16), kb.astype(BF16))
            vb = v.astype(BF16)
            st_old = st[hh]
            o = _dot_nt((qs * jnp.exp(g)).astype(BF16), st_old.astype(BF16)) + _dot(a.astype(BF16), vb)
            g_end = g[c - 1:c]
            kd = kk * jnp.exp(g_end - g)
            st[hh] = st_old * jnp.exp(g_end) + _dot_tn(vb, kd.astype(BF16))
            o_ref[rs, hs] = (_rms(o, g_ref[...]) * gate_ref[rs, hs]).astype(o_ref.dtype)

    @pl.when(ci == pl.num_programs(2) - 1)
    def _():
        for hh in range(hps):
            sn_ref[hh] = st[hh].T


def _hgrn2(hmain, hk, hg_norm_g, s0, c, tc, hps):
    b, t, _ = hmain.shape
    w, mask, levels = _hgrn_tables(c)
    nl = levels + 1
    has_s0 = s0 is not None
    kern = functools.partial(_hgrn_kernel, c=c, levels=levels, nchunk=tc // c, hps=hps, has_s0=has_s0)
    nhg = HG_HEADS // hps
    col = lambda off: pl.BlockSpec((None, tc, hps * HG_DIM), lambda bi, h, ci: (bi, ci, off * nhg + h))
    state_spec = pl.BlockSpec((None, hps, HG_DIM, HG_DIM), lambda bi, h, ci: (bi, h, 0, 0))
    in_specs = [
        pl.BlockSpec(w.shape, lambda bi, h, ci: (0, 0)),
        pl.BlockSpec((nl, c, c), lambda bi, h, ci: (0, 0, 0)),
        pl.BlockSpec((1, HG_DIM), lambda bi, h, ci: (0, 0)),
        col(0), col(1), col(2), col(3), col(0),
    ]
    args = [w, mask, hg_norm_g.reshape(1, HG_DIM), hmain, hmain, hmain, hmain, hk]
    if has_s0:
        in_specs, args = [state_spec] + in_specs, [s0] + args
    return pl.pallas_call(
        kern,
        grid=(b, nhg, t // tc),
        in_specs=in_specs,
        out_specs=[col(0), state_spec],
        out_shape=[jax.ShapeDtypeStruct((b, t, HG_HEADS * HG_DIM), BF16),
                   jax.ShapeDtypeStruct((b, HG_HEADS, HG_DIM, HG_DIM), F32)],
        scratch_shapes=[pltpu.VMEM((hps, HG_DIM, HG_DIM), F32)],
        compiler_params=_cparams(("arbitrary", "arbitrary", "arbitrary")),
        name="hgrn2",
    )(*args)


def _out_kernel(a_ref, m_ref, w_ref, x_ref, ga_ref, o_ref):
    half = a_ref.shape[1]
    mix = _dot(a_ref[...], w_ref[:half, :]) + _dot(m_ref[...], w_ref[half:, :])
    o_ref[...] = x_ref[...] + ga_ref[...] * mix


def _out_proj(a2d, m2d, w_out, x2d, ga, tm, rows_per_mod):
    n, d = x2d.shape
    half = a2d.shape[1]
    r = ga.shape[1]
    tiles_per_mod = rows_per_mod // tm
    return pl.pallas_call(
        _out_kernel,
        grid=(n // tm,),
        in_specs=[
            pl.BlockSpec((tm, half), lambda i: (i, 0)),
            pl.BlockSpec((tm, half), lambda i: (i, 0)),
            pl.BlockSpec((2 * half, d), lambda i: (0, 0)),
            pl.BlockSpec((tm, d), lambda i: (i, 0)),
            pl.BlockSpec((None, r, d), lambda i: (i // tiles_per_mod, 0, 0)),
        ],
        out_specs=pl.BlockSpec((tm, d), lambda i: (i, 0)),
        out_shape=jax.ShapeDtypeStruct((n, d), F32),
        compiler_params=_cparams(("arbitrary",)),
        name="out_proj",
    )(a2d, m2d, w_out.astype(BF16), x2d, ga)


def _router_kernel(x_ref, n2_ref, sc_ref, sh_ref, wr_ref, br_ref, *rest, grouped):
    h = _rms(x_ref[...], n2_ref[...]) * (1.0 + sc_ref[...]) + sh_ref[...]
    if grouped:
        tril_ref, h_ref, idx_ref, rw_ref, cnt_ref, cnt_scr = rest
        h_ref[...] = h
    else:
        h_ref, comb_ref = rest
        h_ref[...] = h.astype(BF16)
    h_hi = h.astype(BF16)
    h_lo = (h - h_hi.astype(F32)).astype(BF16)
    w = wr_ref[...]
    w_hi = w.astype(BF16)
    w_lo = (w - w_hi.astype(F32)).astype(BF16)
    parts = []
    for rs in (slice(0, h.shape[0] // 2), slice(h.shape[0] // 2, h.shape[0])):
        parts.append(_dot(h_hi[rs], w_hi) + _dot(h_hi[rs], w_lo) + _dot(h_lo[rs], w_hi))
    lg = jnp.concatenate(parts, axis=0) + br_ref[...]

    lane = lax.broadcasted_iota(jnp.int32, lg.shape, 1)
    neg = -jnp.inf
    big = jnp.int32(LANES)

    def top(valid):
        m = jnp.max(jnp.where(valid, lg, neg), axis=-1, keepdims=True)
        idx = jnp.min(jnp.where(valid & (lg == m), lane, big), axis=-1, keepdims=True)
        return m, idx

    is_group = lane < N_GROUPS
    gmax, gidx = top(is_group)
    pg_top = 1.0 / jnp.sum(jnp.where(is_group, jnp.exp(lg - gmax), 0.0), axis=-1, keepdims=True)
    lo = ROUTE_LANE0 + EXPERTS_PER_GROUP * gidx
    in_group = (lane >= lo) & (lane < lo + EXPERTS_PER_GROUP)
    l0, e0 = top(in_group)
    l1, e1 = top(in_group & (lane != e0))
    r = jnp.exp(l1 - l0)
    w0 = pg_top / (1.0 + r)
    w1 = w0 * r
    if not grouped:
        comb_ref[...] = jnp.where(lane == e0, w0, 0.0) + jnp.where(lane == e1, w1, 0.0)
        return

    @pl.when(pl.program_id(0) == 0)
    def _():
        cnt_scr[...] = jnp.zeros(cnt_scr.shape, F32)

    is0, is1 = lane == e0, lane == e1
    onehot = jnp.where(is0 | is1, 1.0, 0.0)
    before = _dot(tril_ref[...], onehot.astype(BF16)) + cnt_scr[...]
    rank0 = jnp.sum(jnp.where(is0, before, 0.0), axis=-1, keepdims=True).astype(jnp.int32)
    rank1 = jnp.sum(jnp.where(is1, before, 0.0), axis=-1, keepdims=True).astype(jnp.int32)
    cnt_scr[...] = cnt_scr[...] + jnp.sum(onehot, axis=0, keepdims=True)
    idx_ref[...] = jnp.where(lane == 0, e0 - ROUTE_LANE0, jnp.where(lane == 1, e1 - ROUTE_LANE0,
                             jnp.where(lane == 2, rank0, jnp.where(lane == 3, rank1, 0))))
    rw_ref[...] = jnp.where(lane == 0, w0, jnp.where(lane == 1, w1, 0.0))
    cnt_ref[...] = cnt_scr[...].astype(jnp.int32)


def _router(x1, n2, sc, sh, w_route, b_route, tm, rows_per_mod, grouped):
    n, d = x1.shape
    r = sc.shape[1]
    tiles_per_mod = rows_per_mod // tm
    mod_spec = pl.BlockSpec((None, r, d), lambda i: (i // tiles_per_mod, 0, 0))
    row_spec = lambda w: pl.BlockSpec((tm, w), lambda i: (i, 0))
    in_specs = [row_spec(d), pl.BlockSpec((1, d), lambda i: (0, 0)), mod_spec, mod_spec,
                pl.BlockSpec((d, LANES), lambda i: (0, 0)), pl.BlockSpec((1, LANES), lambda i: (0, 0))]
    args = [x1, n2.reshape(1, d), sc, sh, w_route, b_route]
    if grouped:
        in_specs.append(pl.BlockSpec((tm, tm), lambda i: (0, 0)))
        args.append(jnp.asarray(np.tril(np.ones((tm, tm), np.float32), -1), BF16))
        out_specs = [row_spec(d), row_spec(LANES), row_spec(LANES), pl.BlockSpec((1, LANES), lambda i: (0, 0))]
        out_shape = [jax.ShapeDtypeStruct((n, d), F32), jax.ShapeDtypeStruct((n, LANES), jnp.int32),
                     jax.ShapeDtypeStruct((n, LANES), F32), jax.ShapeDtypeStruct((1, LANES), jnp.int32)]
        scratch = [pltpu.VMEM((1, LANES), F32)]
    else:
        out_specs = [row_spec(d), row_spec(LANES)]
        out_shape = [jax.ShapeDtypeStruct((n, d), BF16), jax.ShapeDtypeStruct((n, LANES), F32)]
        scratch = []
    return pl.pallas_call(
        functools.partial(_router_kernel, grouped=grouped),
        grid=(n // tm,),
        in_specs=in_specs,
        out_specs=out_specs,
        out_shape=out_shape,
        scratch_shapes=scratch,
        compiler_params=_cparams(("arbitrary",)),
        name="router",
    )(*args)


def _dispatch_kernel(pos_ref, pad_ref, h_ref, xs_hbm, zbuf, sem, psem, *, tm, ne):
    i = pl.program_id(0)
    base = i * tm

    def body(r, c):
        t = base + r
        src = h_ref.at[pl.ds(r, 1)]
        pltpu.make_async_copy(src, xs_hbm.at[pl.ds(pos_ref[2 * t], 1)], sem).start()
        pltpu.make_async_copy(src, xs_hbm.at[pl.ds(pos_ref[2 * t + 1], 1)], sem).start()
        return c

    lax.fori_loop(0, tm, body, 0, unroll=8)

    @pl.when(i == pl.num_programs(0) - 1)
    def _():
        zbuf[...] = jnp.zeros(zbuf.shape, F32)
        tile_rows = zbuf.shape[0]

        def pad_copy(e, r):
            return pltpu.make_async_copy(zbuf.at[pl.ds(0, 1)], xs_hbm.at[pl.ds(pad_ref[e] + r, 1)], psem)

        def tail_copy(t):
            return pltpu.make_async_copy(zbuf, xs_hbm.at[pl.ds(pl.multiple_of(t * tile_rows, tile_rows), tile_rows)], psem)

        def for_each_fill(fn):
            def per_expert(e, c):
                def per_row(r, cc):
                    fn(pad_copy(e, r))
                    return cc

                lax.fori_loop(0, pad_ref[ne + e], per_row, 0)
                return c

            def per_tile(t, c):
                fn(tail_copy(t))
                return c

            lax.fori_loop(0, ne, per_expert, 0)
            lax.fori_loop(pad_ref[2 * ne], xs_hbm.shape[0] // tile_rows, per_tile, 0)

        for_each_fill(lambda cp: cp.start())
        for_each_fill(lambda cp: cp.wait())

    for _ in range(2):
        pltpu.make_async_copy(h_ref, xs_hbm.at[pl.ds(0, tm)], sem).wait()


def _dispatch(pos, pad, h, n_rows, tile_rows, ne, tm=512):
    n, w = h.shape
    tm = min(tm, n)
    assert n % tm == 0 and n_rows % tile_rows == 0
    any_spec = pl.BlockSpec(memory_space=pl.ANY)
    return pl.pallas_call(
        functools.partial(_dispatch_kernel, tm=tm, ne=ne),
        grid_spec=pltpu.PrefetchScalarGridSpec(
            num_scalar_prefetch=2, grid=(n // tm,),
            in_specs=[pl.BlockSpec((tm, w), lambda i, p, q: (i, 0))], out_specs=any_spec,
            scratch_shapes=[pltpu.VMEM((tile_rows, w), F32), pltpu.SemaphoreType.DMA(()),
                            pltpu.SemaphoreType.DMA(())]),
        out_shape=jax.ShapeDtypeStruct((n_rows, w), F32),
        compiler_params=_cparams(("arbitrary",)),
        name="moe_dispatch",
    )(pos, pad, h)


def _expert_kernel(te_ref, nv_ref, x_ref, wg_ref, wu_ref, wd_ref, y_ref):
    del te_ref
    i = pl.program_id(0)

    @pl.when(i < nv_ref[0])
    def _():
        x = x_ref[...].astype(BF16)
        a = _dot(x, wg_ref[...].astype(BF16))
        u = _dot(x, wu_ref[...].astype(BF16))
        hid = a * _sigmoid(a) * u
        y_ref[...] = _dot(hid.astype(BF16), wd_ref[...].astype(BF16))

    @pl.when(i >= nv_ref[0])
    def _():
        y_ref[...] = jnp.zeros(y_ref.shape, F32)


def _experts(tile_expert, n_valid, xs, w_gate, w_up, w_down, tm):
    rows, half = xs.shape
    ne, d, f = w_gate.shape
    w_spec = lambda a, b: pl.BlockSpec((None, a, b), lambda i, te, nv: (te[i], 0, 0))
    return pl.pallas_call(
        _expert_kernel,
        grid_spec=pltpu.PrefetchScalarGridSpec(
            num_scalar_prefetch=2, grid=(rows // tm,),
            in_specs=[pl.BlockSpec((tm, half), lambda i, te, nv: (jnp.minimum(i, nv[0] - 1), 0)),
                      w_spec(d, f), w_spec(d, f), w_spec(f, d)],
            out_specs=pl.BlockSpec((tm, d), lambda i, te, nv: (i, 0))),
        out_shape=jax.ShapeDtypeStruct((rows, d), F32),
        compiler_params=_cparams(("arbitrary",)),
        name="moe_experts",
    )(tile_expert, n_valid, xs, w_gate, w_up, w_down)


def _combine_kernel(pos_ref, ys_hbm, x_ref, ga_ref, rw_ref, o_ref, buf, sem, *, tm):
    i = pl.program_id(0)
    n = pl.num_programs(0)

    def issue(tile, slot):
        def body(r, c):
            t = tile * tm + r
            pltpu.make_async_copy(ys_hbm.at[pl.ds(pos_ref[2 * t], 1)], buf.at[slot, pl.ds(r, 1)],
                                  sem.at[slot]).start()
            pltpu.make_async_copy(ys_hbm.at[pl.ds(pos_ref[2 * t + 1], 1)], buf.at[slot, pl.ds(tm + r, 1)],
                                  sem.at[slot]).start()
            return c

        lax.fori_loop(0, tm, body, 0, unroll=8)

    @pl.when(i == 0)
    def _():
        issue(0, 0)

    @pl.when(i + 1 < n)
    def _():
        issue(i + 1, (i + 1) % 2)

    slot = i % 2
    pltpu.make_async_copy(ys_hbm.at[pl.ds(0, 2 * tm)], buf.at[slot], sem.at[slot]).wait()
    w = rw_ref[...]
    y = w[:, 0:1] * buf[slot, 0:tm] + w[:, 1:2] * buf[slot, tm:2 * tm]
    o_ref[...] = x_ref[...] + ga_ref[...] * y


def _combine(pos, ys, x1, ga, rw, tm, rows_per_mod):
    n, d = x1.shape
    r = ga.shape[1]
    tiles_per_mod = rows_per_mod // tm
    return pl.pallas_call(
        functools.partial(_combine_kernel, tm=tm),
        grid_spec=pltpu.PrefetchScalarGridSpec(
            num_scalar_prefetch=1, grid=(n // tm,),
            in_specs=[pl.BlockSpec(memory_space=pl.ANY),
                      pl.BlockSpec((tm, d), lambda i, p: (i, 0)),
                      pl.BlockSpec((None, r, d), lambda i, p: (i // tiles_per_mod, 0, 0)),
                      pl.BlockSpec((tm, LANES), lambda i, p: (i, 0))],
            out_specs=pl.BlockSpec((tm, d), lambda i, p: (i, 0)),
            scratch_shapes=[pltpu.VMEM((2, 2 * tm, d), F32), pltpu.SemaphoreType.DMA((2,))]),
        out_shape=jax.ShapeDtypeStruct((n, d), F32),
        compiler_params=_cparams(("arbitrary",)),
        name="moe_combine",
    )(pos, ys, x1, ga, rw)


def _grouped_moe(hp, idx, rw, cnt, w_gate, w_up, w_down, x1, ga, rows_per_mod, tm_e=256, tm_c=256):
    n = hp.shape[0]
    ne = w_gate.shape[0]
    tm_e = min(tm_e, n)
    counts = cnt[0, ROUTE_LANE0:ROUTE_LANE0 + ne]
    tiles = (counts + tm_e - 1) // tm_e
    tile_end = jnp.cumsum(tiles)
    start_row = (tile_end - tiles) * tm_e
    n_tiles = (2 * n) // tm_e + ne
    tile_expert = jnp.minimum(jnp.sum(tile_end[None, :] <= jnp.arange(n_tiles)[:, None], axis=1), ne - 1)
    pos = (jnp.take(start_row, idx[:, 0:2], axis=0) + idx[:, 2:4]).reshape(-1).astype(jnp.int32)
    pad = jnp.concatenate([start_row + counts, tiles * tm_e - counts, tile_end[ne - 1:]]).astype(jnp.int32)
    xs = _dispatch(pos, pad, hp, n_tiles * tm_e, tm_e, ne)
    ys = _experts(tile_expert.astype(jnp.int32), tile_end[ne - 1:].astype(jnp.int32), xs, w_gate, w_up, w_down, tm_e)
    return _combine(pos, ys, x1, ga, rw, min(tm_c, n), rows_per_mod)


def _moe_kernel(h_ref, comb_ref, wg_ref, wu_ref, wd_ref, x_ref, ga_ref, o_ref, acc):
    e = pl.program_id(1)

    @pl.when(e == 0)
    def _():
        acc[...] = jnp.zeros(acc.shape, F32)

    h = h_ref[...]
    a = _dot(h, wg_ref[...].astype(BF16))
    u = _dot(h, wu_ref[...].astype(BF16))
    lane = lax.broadcasted_iota(jnp.int32, comb_ref.shape, 1)
    cw = jnp.sum(jnp.where(lane == e + ROUTE_LANE0, comb_ref[...], 0.0), axis=-1, keepdims=True)
    hid = a * _sigmoid(a) * u * cw
    acc[...] += _dot(hid.astype(BF16), wd_ref[...].astype(BF16))

    @pl.when(e == pl.num_programs(1) - 1)
    def _():
        o_ref[...] = x_ref[...] + ga_ref[...] * acc[...]


def _moe(h2, comb, w_gate, w_up, w_down, x1, ga, tm, rows_per_mod):
    n, d = x1.shape
    ne, _, f = w_gate.shape
    r = ga.shape[1]
    tiles_per_mod = rows_per_mod // tm
    return pl.pallas_call(
        _moe_kernel,
        grid=(n // tm, ne),
        in_specs=[
            pl.BlockSpec((tm, d), lambda i, e: (i, 0)),
            pl.BlockSpec((tm, LANES), lambda i, e: (i, 0)),
            pl.BlockSpec((None, d, f), lambda i, e: (e, 0, 0)),
            pl.BlockSpec((None, d, f), lambda i, e: (e, 0, 0)),
            pl.BlockSpec((None, f, d), lambda i, e: (e, 0, 0)),
            pl.BlockSpec((tm, d), lambda i, e: (i, 0)),
            pl.BlockSpec((None, r, d), lambda i, e: (i // tiles_per_mod, 0, 0)),
        ],
        out_specs=pl.BlockSpec((tm, d), lambda i, e: (i, 0)),
        out_shape=jax.ShapeDtypeStruct((n, d), F32),
        scratch_shapes=[pltpu.VMEM((tm, d), F32)],
        compiler_params=_cparams(("arbitrary", "arbitrary")),
        name="moe",
    )(h2, comb, w_gate, w_up, w_down, x1, ga)


def _pick_tile(n, pref):
    t = min(pref, n)
    while n % t:
        t //= 2
    return t


def _layer(x, mods, s0, attend, lw, lam_init, hg_chunk, hg_heads_per_step, per_row_mod):
    (n1, n2, w_in, gq, gk, subln, lam4, hlb, hg_g, w_out, w_route, b_route, w_gate, w_up, w_down) = lw
    b, t, d = x.shape
    n = b * t
    x2d = x.reshape(n, d)
    if per_row_mod:
        tm = n
        rows_per_mod = n
        mods = [jnp.repeat(m, t, axis=0).reshape(1, n, d) for m in mods]
    else:
        tm = _pick_tile(t, 512)
        rows_per_mod = t
        mods = [m.reshape(b, 1, d) for m in mods]
    sh1, sc1, ga1, sh2, sc2, ga2 = mods

    tm_in = tm if per_row_mod else _pick_tile(t, 1024)
    q, k, v, hmain, hk = _in_proj(x2d, n1, sc1, sh1, w_in, gq, gk, hlb, tm_in, rows_per_mod)
    a = attend(q.reshape(b, t, -1), k.reshape(b, t, -1), v.reshape(b, t, -1))
    tpad = -(-t // hg_chunk) * hg_chunk
    hm3, hk3 = hmain.reshape(b, t, -1), hk.reshape(b, t, -1)
    if tpad != t:
        hm3 = jnp.pad(hm3, ((0, 0), (0, tpad - t), (0, 0)))
        hk3 = jnp.pad(hk3, ((0, 0), (0, tpad - t), (0, 0)))
    m, s_new = _hgrn2(hm3, hk3, hg_g, s0, hg_chunk, _pick_tile(tpad, 2 * hg_chunk), hg_heads_per_step)
    m = m[:, :t]
    x1 = _out_proj(a.reshape(n, -1), m.reshape(n, -1), w_out, x2d, ga1, tm, rows_per_mod)
    if per_row_mod:
        h2, comb = _router(x1, n2, sc2, sh2, w_route, b_route, tm, rows_per_mod, False)
        y = _moe(h2, comb, w_gate, w_up, w_down, x1, ga2, tm, rows_per_mod)
    else:
        hp, idx, rw, cnt = _router(x1, n2, sc2, sh2, w_route, b_route, tm, rows_per_mod, True)
        y = _grouped_moe(hp, idx, rw, cnt, w_gate, w_up, w_down, x1, ga2, rows_per_mod)
    return y.reshape(b, t, d), k, v, s_new


def kernel(x_prompt, x_sample, cache_k, cache_v, state_hgrn, page_table, c_prompt, c_sample, norm1_g, norm2_g, w_ada, b_ada, w_in, q_norm_g, k_norm_g, lambda_q1, lambda_k1, lambda_q2, lambda_k2, subln_g, hg_lower_bound, hg_norm_g, w_out, w_router_group, b_router_group, w_router_expert, b_router_expert, w_exp_gate, w_exp_up, w_exp_down):
    depth = norm1_g.shape[0]
    assert depth == 1, "single-layer trunk"
    l = 0
    lam_init = 0.8 - 0.6 * math.exp(-0.3 * l)
    bp, tp, d = x_prompt.shape
    bs, ts, _ = x_sample.shape

    c_all = jnp.concatenate([c_prompt, c_sample], axis=0)
    rpad = -(-c_all.shape[0] // 8) * 8
    c_all = jnp.pad(c_all, ((0, rpad - c_all.shape[0]), (0, 0)))
    mod = _adaln(c_all, w_ada[l], b_ada[l])
    mods_p = [mod[:bp, i * d:(i + 1) * d] for i in range(6)]
    mods_s = [mod[bp:bp + bs, i * d:(i + 1) * d] for i in range(6)]

    w_re = jnp.transpose(w_router_expert[l], (1, 0, 2)).reshape(d, N_EXPERTS)
    w_route = jnp.pad(jnp.concatenate([w_router_group[l], w_re], axis=1), ((0, 0), (0, LANES - N_GROUPS - N_EXPERTS)))
    b_route = jnp.pad(jnp.concatenate([b_router_group[l], b_router_expert[l].reshape(-1)]),
                      (0, LANES - N_GROUPS - N_EXPERTS)).reshape(1, LANES)
    lam4 = jnp.stack([lambda_q1[l], lambda_k1[l], lambda_q2[l], lambda_k2[l]], axis=0)

    lw = (norm1_g[l], norm2_g[l], w_in[l], q_norm_g[l], k_norm_g[l], subln_g[l], lam4, hg_lower_bound,
          hg_norm_g[l], w_out[l], w_route, b_route, w_exp_gate[l], w_exp_up[l], w_exp_down[l])

    def attend_prompt(q, k, v):
        return _prompt_attention(q, k, v, lam4, subln_g[l], lam_init)

    def attend_sample(q, k, v):
        pad = lambda a: jnp.pad(a, ((0, 0), (0, -ts % 8), (0, 0)))
        o = _sample_attention(pad(q), pad(k), pad(v), ts, cache_k[l], cache_v[l], page_table, lam4,
                              subln_g[l], lam_init)
        return o[:, :ts]

    yp, kp, vp, sp = _layer(x_prompt, mods_p, None, attend_prompt, lw, lam_init, 128, 4, False)
    ys, kn, vn, sn = _layer(x_sample, mods_s, state_hgrn[l], attend_sample, lw, lam_init, 8, HG_HEADS, True)

    kv_shape = lambda b, t: (1, b, t, ATTN_KV_HEADS, 2 * QK_DIM)
    return (yp, ys,
            kp.reshape(kv_shape(bp, tp)), vp.reshape(kv_shape(bp, tp)),
            kn.reshape(kv_shape(bs, ts)), vn.reshape(kv_shape(bs, ts)),
            sp[None], sn[None])
```

```python
import functools
import math

import jax
import jax.numpy as jnp
import numpy as np
from jax import lax
from jax.experimental import pallas as pl
from jax.experimental.pallas import tpu as pltpu

F32 = jnp.float32
BF16 = jnp.bfloat16

QK_DIM = 64
V_DIM = 128
ATTN_KV_HEADS = 4
ATTN_GROUP = 2
ATTN_HEADS = ATTN_KV_HEADS * ATTN_GROUP
HG_HEADS = 8
HG_DIM = 128
N_GROUPS = 4
EXPERTS_PER_GROUP = 4
N_EXPERTS = N_GROUPS * EXPERTS_PER_GROUP
NORM_EPS = 1e-6
ALIBI_MAX_BIAS = 8.0
LANES = 128
MXU_DIM = 256
VMEM_LIMIT = 60 * 1024 * 1024
ROUTE_LANE0 = N_GROUPS
HG_TABLE_LEVELS = 2


def _cparams(sem):
    return pltpu.CompilerParams(dimension_semantics=sem, vmem_limit_bytes=VMEM_LIMIT)


def _dot(a, b):
    return jnp.dot(a, b, preferred_element_type=F32)


def _dot_nt(a, b):
    return lax.dot_general(a, b, (((1,), (1,)), ((), ())), preferred_element_type=F32)


def _dot_tn(a, b):
    return lax.dot_general(a, b, (((0,), (0,)), ((), ())), preferred_element_type=F32)


def _split3(x):
    hi = x.astype(BF16)
    r = x - hi.astype(F32)
    mid = r.astype(BF16)
    lo = (r - mid.astype(F32)).astype(BF16)
    return hi, mid, lo


def _sigmoid(x):
    return 1.0 / (1.0 + jnp.exp(-x))


def _rms(x, g):
    return x * lax.rsqrt(jnp.mean(x * x, axis=-1, keepdims=True) + NORM_EPS) * g


def _ada_kernel(c_ref, w_ref, b_ref, o_ref):
    c = c_ref[...]
    s = c * _sigmoid(c)
    o_ref[...] = _dot(s.astype(BF16), w_ref[...].astype(BF16)) + b_ref[...]


def _adaln(c_all, w_ada, b_ada, tn=1024):
    r, d = c_all.shape
    n = w_ada.shape[1]
    return pl.pallas_call(
        _ada_kernel,
        grid=(n // tn,),
        in_specs=[
            pl.BlockSpec((r, d), lambda j: (0, 0)),
            pl.BlockSpec((d, tn), lambda j: (0, j)),
            pl.BlockSpec((1, tn), lambda j: (0, j)),
        ],
        out_specs=pl.BlockSpec((r, tn), lambda j: (0, j)),
        out_shape=jax.ShapeDtypeStruct((r, n), F32),
        compiler_params=_cparams(("arbitrary",)),
        name="adaln",
    )(c_all, w_ada, b_ada.reshape(1, n))


IN_TN = 512


def _group_sumsq(p, bd):
    x2 = p * p
    hi = x2.astype(BF16)
    lo = (x2 - hi.astype(F32)).astype(BF16)
    outs = []
    for c in range(p.shape[1] // MXU_DIM):
        sl = slice(c * MXU_DIM, (c + 1) * MXU_DIM)
        outs.append(_dot(hi[:, sl], bd) + _dot(lo[:, sl], bd))
    return jnp.concatenate(outs, axis=1)


def _in_epilogue(c, p, gq_ref, gk_ref, lb_ref, bd_ref, q_ref, k_ref, v_ref, hm_ref, hk_ref):
    def qk_norm(g):
        ms = _group_sumsq(p, bd_ref[...]) * (1.0 / QK_DIM)
        return p * lax.rsqrt(ms + NORM_EPS) * g

    if c < 2:
        q_ref[...] = (qk_norm(gq_ref[...]) * (QK_DIM ** -0.5)).astype(BF16)
    elif c == 2:
        k_ref[...] = qk_norm(gk_ref[...])
    elif c == 3:
        v_ref[...] = p
    elif c in (4, 5, 10, 11):
        hm_ref[...] = p * _sigmoid(p)
    elif c in (6, 7):
        a = lb_ref[...]
        e = jnp.exp(a - jnp.max(a, axis=0, keepdims=True))
        lb = e[0:1] / jnp.sum(e, axis=0, keepdims=True)
        hm_ref[...] = jnp.log(lb + (1.0 - lb) * _sigmoid(p))
        hk_ref[...] = (1.0 - lb) * _sigmoid(-p)
    else:
        hm_ref[...] = p


def _in_kernel(x_ref, n1_ref, sc_ref, sh_ref, w_ref, gq_ref, gk_ref, lb_ref, bd_ref,
               q_ref, k_ref, v_ref, hm_ref, hk_ref, h_scr, p_a, p_b, *, nj):
    j = pl.program_id(1)
    p_bufs = (p_a, p_b)
    outs = (gq_ref, gk_ref, lb_ref, bd_ref, q_ref, k_ref, v_ref, hm_ref, hk_ref)
    for jj in range(nj + 1):
        @pl.when(j == jj)
        def _(jj=jj):
            if jj == 0:
                y = _rms(x_ref[...], n1_ref[...])
                h_scr[...] = (y * (1.0 + sc_ref[...]) + sh_ref[...]).astype(BF16)
            if jj < nj:
                p_bufs[jj % 2][...] = _dot(h_scr[...], w_ref[...].astype(BF16))
            if jj > 0:
                _in_epilogue(jj - 1, p_bufs[(jj - 1) % 2][...], *outs)


def _in_proj(x2d, n1, sc, sh, w_in, gq, gk, hlb, tm, rows_per_mod):
    n, d = x2d.shape
    cols = w_in.shape[1]
    nj = cols // IN_TN
    assert nj == 12 and n % tm == 0
    r = sc.shape[1]
    tiles_per_mod = rows_per_mod // tm
    bd = jnp.asarray(np.kron(np.eye(MXU_DIM // QK_DIM), np.ones((QK_DIM, QK_DIM))), BF16)
    clamp = lambda j, lo: jnp.clip(j - 1 - lo, 0, 1)
    mod_spec = pl.BlockSpec((None, r, d), lambda i, j: (i // tiles_per_mod, 0, 0))
    return pl.pallas_call(
        functools.partial(_in_kernel, nj=nj),
        grid=(n // tm, nj + 1),
        in_specs=[
            pl.BlockSpec((tm, d), lambda i, j: (i, 0)),
            pl.BlockSpec((1, d), lambda i, j: (0, 0)),
            mod_spec, mod_spec,
            pl.BlockSpec((d, IN_TN), lambda i, j: (0, jnp.minimum(j, nj - 1))),
            pl.BlockSpec((1, IN_TN), lambda i, j: (0, 0)),
            pl.BlockSpec((1, IN_TN), lambda i, j: (0, 0)),
            pl.BlockSpec((hlb.shape[0], IN_TN), lambda i, j: (0, clamp(j, 6))),
            pl.BlockSpec((MXU_DIM, MXU_DIM), lambda i, j: (0, 0)),
        ],
        out_specs=[
            pl.BlockSpec((tm, IN_TN), lambda i, j: (i, clamp(j, 0))),
            pl.BlockSpec((tm, IN_TN), lambda i, j: (i, 0)),
            pl.BlockSpec((tm, IN_TN), lambda i, j: (i, 0)),
            pl.BlockSpec((tm, IN_TN), lambda i, j: (i, jnp.clip(j - 5, 0, 7))),
            pl.BlockSpec((tm, IN_TN), lambda i, j: (i, clamp(j, 6))),
        ],
        out_shape=[
            jax.ShapeDtypeStruct((n, 2 * IN_TN), BF16),
            jax.ShapeDtypeStruct((n, IN_TN), F32),
            jax.ShapeDtypeStruct((n, IN_TN), F32),
            jax.ShapeDtypeStruct((n, 8 * IN_TN), F32),
            jax.ShapeDtypeStruct((n, 2 * IN_TN), F32),
        ],
        scratch_shapes=[pltpu.VMEM((tm, d), BF16), pltpu.VMEM((tm, IN_TN), F32), pltpu.VMEM((tm, IN_TN), F32)],
        compiler_params=_cparams(("arbitrary", "arbitrary")),
        name="in_proj",
    )(x2d, n1.reshape(1, d), sc, sh, w_in,
      jnp.tile(gq.reshape(1, QK_DIM), (1, IN_TN // QK_DIM)),
      jnp.tile(gk.reshape(1, QK_DIM), (1, IN_TN // QK_DIM)), hlb, bd)


def _alibi_slopes_np():
    h = np.arange(1, ATTN_HEADS + 1, dtype=np.float64)
    return np.exp2(-ALIBI_MAX_BIAS * h / ATTN_HEADS).reshape(ATTN_KV_HEADS, ATTN_GROUP).astype(np.float32)


def _alibi_slopes():
    out = np.zeros((ATTN_KV_HEADS, 1, LANES), np.float32)
    out[:, 0, :ATTN_GROUP] = _alibi_slopes_np()
    return jnp.asarray(out)


def _lambda(lam_ref, lam_init):
    l = lam_ref[...]
    s1 = jnp.sum(l[0:1] * l[1:2], axis=-1, keepdims=True)
    s2 = jnp.sum(l[2:3] * l[3:4], axis=-1, keepdims=True)
    return jnp.exp(s1) - jnp.exp(s2) + lam_init


def _diff_finish(o_all, t, lam, subln, lam_init, o_ref):
    for g in range(ATTN_GROUP):
        o = o_all[(2 * g) * t:(2 * g + 1) * t] - lam * o_all[(2 * g + 1) * t:(2 * g + 2) * t]
        y = _rms(o, subln) * (1.0 - lam_init)
        o_ref[:, g * LANES:(g + 1) * LANES] = y.astype(o_ref.dtype)


def _softmax_step_t(s, m_old, l_old):
    m_new = jnp.maximum(m_old, jnp.max(s, axis=0, keepdims=True))
    alpha = jnp.exp(m_old - m_new)
    p = jnp.exp(s - m_new)
    return m_new, alpha, alpha * l_old + jnp.sum(p, axis=0, keepdims=True), p.astype(BF16)


def _pattn_kernel(sl_ref, lam_ref, subln_ref, q_ref, k_ref, v_ref, o_ref, kb, vt, qs, s_a, s_b, m_scr, l_scr, acc,
                  *, tq, nblk, lam_init):
    qi = pl.program_id(2)

    @pl.when(qi == 0)
    def _():
        t = k_ref.shape[0]
        pos = lax.broadcasted_iota(jnp.int32, (t, LANES), 0)
        ln = lax.broadcasted_iota(jnp.int32, (t, LANES), 1)
        aug = jnp.where(ln == 0, pos & ~(LANES - 1), jnp.where(ln == 1, pos & (LANES - 1), 0))
        kb[:, :LANES] = k_ref[...].astype(BF16)
        kb[:, LANES:] = aug.astype(F32).astype(BF16)
        for jb in range(t // tq):
            vt[jb] = v_ref[jb * tq:(jb + 1) * tq, :].T.astype(BF16)

    rows = 4 * tq
    half = 2 * tq
    q = q_ref[...]
    lane = lax.broadcasted_iota(jnp.int32, (tq, LANES), 1)
    zero = jnp.zeros((tq, LANES), BF16)
    for g in range(ATTN_GROUP):
        qg = q[:, g * LANES:(g + 1) * LANES]
        aug = jnp.where(lane < 2, sl_ref[0:1, g:g + 1], 0.0).astype(BF16)
        qs[(2 * g) * tq:(2 * g + 1) * tq, :] = jnp.concatenate([jnp.where(lane < QK_DIM, qg, zero), aug], axis=1)
        qs[(2 * g + 1) * tq:(2 * g + 2) * tq, :] = jnp.concatenate([jnp.where(lane >= QK_DIM, qg, zero), aug], axis=1)
    t_q = lax.broadcasted_iota(jnp.int32, (1, half), 1) % tq
    m_scr[...] = jnp.full(m_scr.shape, -jnp.inf, F32)
    l_scr[...] = jnp.zeros(l_scr.shape, F32)
    acc[...] = jnp.zeros(acc.shape, F32)

    s_bufs = (s_a, s_b)

    def logits(jb):
        s_bufs[jb % 2][...] = _dot_nt(kb[jb * tq:(jb + 1) * tq, :], qs[...])

    def attend(jb, masked):
        for g in range(ATTN_GROUP):
            cs = slice(g * half, (g + 1) * half)
            s = s_bufs[jb % 2][:, cs]
            if masked:
                key = lax.broadcasted_iota(jnp.int32, (tq, 1), 0)
                s = jnp.where(key <= t_q, s, -jnp.inf)
            m_new, alpha, l_new, p = _softmax_step_t(s, m_scr[:, cs], l_scr[:, cs])
            acc[:, cs] = alpha * acc[:, cs] + _dot(vt[jb], p)
            m_scr[:, cs] = m_new
            l_scr[:, cs] = l_new

    logits(0)
    for jb in range(nblk):
        if jb + 1 < nblk:
            @pl.when(jb < qi)
            def _(jb=jb):
                logits(jb + 1)
                attend(jb, False)

        @pl.when(jb == qi)
        def _(jb=jb):
            attend(jb, True)

    o_t = acc[...] * (1.0 / l_scr[...])
    lam = _lambda(lam_ref, lam_init)
    for g in range(ATTN_GROUP):
        d = o_t[:, (2 * g) * tq:(2 * g + 1) * tq] - lam * o_t[:, (2 * g + 1) * tq:(2 * g + 2) * tq]
        y = _rms(d.T, subln_ref[...]) * (1.0 - lam_init)
        o_ref[:, g * LANES:(g + 1) * LANES] = y.astype(o_ref.dtype)


def _prompt_attention(q, k, v, lam4, subln, lam_init, tq=512):
    b, t, _ = q.shape
    tq = min(tq, t)
    assert t % tq == 0 and t < LANES * 256
    rows = 4 * tq
    kern = functools.partial(_pattn_kernel, tq=tq, nblk=t // tq, lam_init=lam_init)
    return pl.pallas_call(
        kern,
        grid=(b, ATTN_KV_HEADS, t // tq),
        in_specs=[
            pl.BlockSpec((None, 1, LANES), lambda bi, h, qi: (h, 0, 0)),
            pl.BlockSpec((4, QK_DIM), lambda bi, h, qi: (0, 0)),
            pl.BlockSpec((1, V_DIM), lambda bi, h, qi: (0, 0)),
            pl.BlockSpec((None, tq, 2 * LANES), lambda bi, h, qi: (bi, qi, h)),
            pl.BlockSpec((None, t, LANES), lambda bi, h, qi: (bi, 0, h)),
            pl.BlockSpec((None, t, LANES), lambda bi, h, qi: (bi, 0, h)),
        ],
        out_specs=pl.BlockSpec((None, tq, 2 * LANES), lambda bi, h, qi: (bi, qi, h)),
        out_shape=jax.ShapeDtypeStruct((b, t, ATTN_HEADS * V_DIM), BF16),
        scratch_shapes=[
            pltpu.VMEM((t, 2 * LANES), BF16), pltpu.VMEM((t // tq, V_DIM, tq), BF16),
            pltpu.VMEM((rows, 2 * LANES), BF16), pltpu.VMEM((tq, rows), F32), pltpu.VMEM((tq, rows), F32),
            pltpu.VMEM((1, rows), F32), pltpu.VMEM((1, rows), F32), pltpu.VMEM((V_DIM, rows), F32),
        ],
        compiler_params=_cparams(("arbitrary", "arbitrary", "arbitrary")),
        name="prompt_attention",
    )(_alibi_slopes(), lam4, subln.reshape(1, V_DIM), q, k, v)


def _to_col(row):
    n = row.shape[1]
    eye = lax.broadcasted_iota(jnp.int32, (n, n), 0) == lax.broadcasted_iota(jnp.int32, (n, n), 1)
    return jnp.sum(jnp.where(eye, row, 0.0), axis=1, keepdims=True)


def _sattn_kernel(pt_ref, srow_ref, lam_ref, subln_ref, q_ref, kn_ref, vn_ref, *rest,
                  npg, nj, page, t, t_real, lam_init):
    del pt_ref
    kp_refs, vp_refs = rest[:npg], rest[npg:2 * npg]
    o_ref = rest[2 * npg]
    kc, vc, qf_scr, qbd_scr, m_scr, l_scr, acc = rest[2 * npg + 1:]
    j = pl.program_id(1)
    ntok = npg * page
    past = nj * ntok
    nq = qbd_scr.shape[0]
    width = ATTN_KV_HEADS * LANES
    srow = srow_ref[...]

    @pl.when(j == 0)
    def _():
        m_scr[...] = jnp.full(m_scr.shape, -jnp.inf, F32)
        l_scr[...] = jnp.zeros(l_scr.shape, F32)
        acc[...] = jnp.zeros(acc.shape, F32)
        qf = q_ref[...].astype(F32)
        lane = lax.broadcasted_iota(jnp.int32, (t, LANES), 1)
        qf_scr[...] = jnp.zeros(qf_scr.shape, F32)
        for hg in range(ATTN_HEADS):
            h = hg // ATTN_GROUP
            qh = qf[:, hg * LANES:(hg + 1) * LANES]
            qf_scr[(2 * hg) * t:(2 * hg + 1) * t, h * LANES:(h + 1) * LANES] = jnp.where(lane < QK_DIM, qh, 0.0)
            qf_scr[(2 * hg + 1) * t:(2 * hg + 2) * t, h * LANES:(h + 1) * LANES] = jnp.where(lane >= QK_DIM, qh, 0.0)
        qbd_scr[...] = qf_scr[...].astype(BF16)

    qbd = qbd_scr[...]

    for p in range(npg):
        kc[p * page:(p + 1) * page, :] = kp_refs[p][...].reshape(page, width).astype(BF16)
    kpos = lax.broadcasted_iota(jnp.int32, (ntok, nq), 0) + j * ntok
    hw = ntok // 2
    s = jnp.concatenate([_dot_nt(kc[:hw, :], qbd), _dot_nt(kc[hw:, :], qbd)], axis=0)
    s = s + kpos.astype(F32) * srow
    for p in range(npg):
        vc[p * page:(p + 1) * page, :] = vp_refs[p][...].reshape(page, width).astype(BF16)
    m_new, alpha, l_new, p = _softmax_step_t(s, m_scr[...], l_scr[...])
    pv = _dot_tn(p[:hw], vc[:hw, :]) + _dot_tn(p[hw:], vc[hw:, :])
    acc[...] = _to_col(alpha) * acc[...] + pv
    m_scr[...] = m_new
    l_scr[...] = l_new

    @pl.when(j == nj - 1)
    def _():
        t_q = lax.broadcasted_iota(jnp.int32, (1, nq), 1) % t
        c = lax.broadcasted_iota(jnp.int32, (t, 1), 0)
        sn = _dot_nt(kn_ref[...].astype(BF16), qbd) + (past + c).astype(F32) * srow
        sn = jnp.where((c <= t_q) & (c < t_real), sn, -jnp.inf)
        m_fin, alpha_n, l_fin, _ = _softmax_step_t(sn, m_scr[...], l_scr[...])
        pn = jnp.exp(sn - m_fin)
        o = _to_col(alpha_n) * acc[...]
        vn = vn_ref[...]
        for u in range(t_real):
            o = o + _to_col(pn[u:u + 1]) * vn[u:u + 1]
        o = o * _to_col(1.0 / l_fin)
        lam = _lambda(lam_ref, lam_init)
        rq = nq // ATTN_KV_HEADS
        for h in range(ATTN_KV_HEADS):
            o_h = o[h * rq:(h + 1) * rq, h * LANES:(h + 1) * LANES]
            _diff_finish(o_h, t, lam, subln_ref[...], lam_init, o_ref.at[:, 2 * h * LANES:(2 * h + 2) * LANES])


def _sample_attention(q, k_new, v_new, t_real, cache_k, cache_v, page_table, lam4, subln, lam_init, npg=16):
    b, t, _ = q.shape
    tpad = t
    page = cache_k.shape[1]
    n_pages = page_table.shape[1]
    npg = min(npg, n_pages)
    assert n_pages % npg == 0
    ntok = npg * page
    width = ATTN_KV_HEADS * LANES
    nq = ATTN_KV_HEADS * ATTN_GROUP * 2 * t
    nj = n_pages // npg
    kern = functools.partial(_sattn_kernel, npg=npg, nj=nj, page=page, t=t, t_real=t_real, lam_init=lam_init)
    slope_row = jnp.asarray(np.repeat(_alibi_slopes_np().reshape(-1), 2 * t)[None, :])

    def page_spec(p):
        return pl.BlockSpec((None, page, ATTN_KV_HEADS, LANES),
                            lambda bi, j, pt: (pt[bi * n_pages + j * npg + p], 0, 0, 0))

    grid_spec = pltpu.PrefetchScalarGridSpec(
        num_scalar_prefetch=1,
        grid=(b, nj),
        in_specs=[
            pl.BlockSpec((1, nq), lambda bi, j, pt: (0, 0)),
            pl.BlockSpec((4, QK_DIM), lambda bi, j, pt: (0, 0)),
            pl.BlockSpec((1, V_DIM), lambda bi, j, pt: (0, 0)),
            pl.BlockSpec((None, t, ATTN_HEADS * V_DIM), lambda bi, j, pt: (bi, 0, 0)),
            pl.BlockSpec((None, tpad, width), lambda bi, j, pt: (bi, 0, 0)),
            pl.BlockSpec((None, tpad, width), lambda bi, j, pt: (bi, 0, 0)),
        ] + [page_spec(p) for p in range(npg)] * 2,
        out_specs=pl.BlockSpec((None, t, ATTN_HEADS * V_DIM), lambda bi, j, pt: (bi, 0, 0)),
        scratch_shapes=[
            pltpu.VMEM((ntok, width), BF16), pltpu.VMEM((ntok, width), BF16),
            pltpu.VMEM((nq, width), F32), pltpu.VMEM((nq, width), BF16),
            pltpu.VMEM((1, nq), F32), pltpu.VMEM((1, nq), F32), pltpu.VMEM((nq, width), F32),
        ],
    )
    return pl.pallas_call(
        kern,
        grid_spec=grid_spec,
        out_shape=jax.ShapeDtypeStruct((b, t, ATTN_HEADS * V_DIM), BF16),
        compiler_params=_cparams(("arbitrary", "arbitrary")),
        name="sample_attention",
    )(page_table.reshape(-1), slope_row, lam4, subln.reshape(1, V_DIM), q, k_new, v_new,
      *([cache_k] * npg), *([cache_v] * npg))


def _hgrn_tables(c):
    levels = int(math.log2(c))
    tril = np.tril(np.ones((c, c), np.float32))
    w = [tril]
    mask = [np.eye(c, dtype=np.float32)]
    idx = np.arange(c)
    for l in range(levels):
        bs = 2 << l
        mid = (idx // bs) * bs + bs // 2
        if l < HG_TABLE_LEVELS:
            w.append(tril - tril[mid])
        upper = (idx % bs) >= bs // 2
        same = (idx[:, None] // bs) == (idx[None, :] // bs)
        mask.append((same & upper[:, None] & ~upper[None, :]).astype(np.float32))
    return jnp.asarray(np.concatenate(w, 0), BF16), jnp.asarray(np.stack(mask, 0)), levels


def _minus_mid_rows(g, bs):
    c, n = g.shape
    g3 = g.reshape(c // bs, bs, n)
    return (g3 - g3[:, bs // 2:bs // 2 + 1, :]).reshape(c, n)


def _hgrn_kernel(*refs, c, levels, nchunk, hps, has_s0):
    if has_s0:
        s0_ref, refs = refs[0], refs[1:]
    (w_ref, mask_ref, g_ref, q_ref, lf_ref, v_ref, gate_ref, k_ref, o_ref, sn_ref, st) = refs
    ci = pl.program_id(2)

    @pl.when(ci == 0)
    def _():
        for hh in range(hps):
            st[hh] = s0_ref[hh].T if has_s0 else jnp.zeros((HG_DIM, HG_DIM), F32)

    w = w_ref[...]
    pair = 2 * HG_DIM
    for n in range(nchunk):
        rs = slice(n * c, (n + 1) * c)
        for hh in range(hps):
            hs = slice(hh * HG_DIM, (hh + 1) * HG_DIM)
            if hh % 2 == 0:
                ps = slice(hh * HG_DIM, hh * HG_DIM + pair)
                l_hi, l_mid, l_lo = _split3(lf_ref[rs, ps])
                gx2 = _dot(w, l_hi) + _dot(w, l_mid) + _dot(w, l_lo)
            gx = gx2[:, (hh % 2) * HG_DIM:(hh % 2 + 1) * HG_DIM]
            qs, kk, v = q_ref[rs, hs], k_ref[rs, hs], v_ref[rs, hs]
            g = gx[0:c]
            a = mask_ref[0] * _dot_nt(qs.astype(BF16), kk.astype(BF16))
            for l in range(levels):
                x = gx[(l + 1) * c:(l + 2) * c] if l < HG_TABLE_LEVELS else _minus_mid_rows(g, 2 << l)
                e = jnp.exp(-jnp.abs(x))
                qa = qs * e
                kb = kk * e
                a = a + mask_ref[l + 1] * _dot_nt(qa.astype(BF16), kb.astype(BF16))
            vb = v.astype(BF16)
            st_old = st[hh]
            o = _dot_nt((qs * jnp.exp(g)).astype(BF16), st_old.astype(BF16)) + _dot(a.astype(BF16), vb)
            g_end = g[c - 1:c]
            kd = kk * jnp.exp(g_end - g)
            st[hh] = st_old * jnp.exp(g_end) + _dot_tn(vb, kd.astype(BF16))
            o_ref[rs, hs] = (_rms(o, g_ref[...]) * gate_ref[rs, hs]).astype(o_ref.dtype)

    @pl.when(ci == pl.num_programs(2) - 1)
    def _():
        for hh in range(hps):
            sn_ref[hh] = st[hh].T


def _hgrn2(hmain, hk, hg_norm_g, s0, c, tc, hps):
    b, t, _ = hmain.shape
    w, mask, levels = _hgrn_tables(c)
    nl = levels + 1
    has_s0 = s0 is not None
    kern = functools.partial(_hgrn_kernel, c=c, levels=levels, nchunk=tc // c, hps=hps, has_s0=has_s0)
    nhg = HG_HEADS // hps
    col = lambda off: pl.BlockSpec((None, tc, hps * HG_DIM), lambda bi, h, ci: (bi, ci, off * nhg + h))
    state_spec = pl.BlockSpec((None, hps, HG_DIM, HG_DIM), lambda bi, h, ci: (bi, h, 0, 0))
    in_specs = [
        pl.BlockSpec(w.shape, lambda bi, h, ci: (0, 0)),
        pl.BlockSpec((nl, c, c), lambda bi, h, ci: (0, 0, 0)),
        pl.BlockSpec((1, HG_DIM), lambda bi, h, ci: (0, 0)),
        col(0), col(1), col(2), col(3), col(0),
    ]
    args = [w, mask, hg_norm_g.reshape(1, HG_DIM), hmain, hmain, hmain, hmain, hk]
    if has_s0:
        in_specs, args = [state_spec] + in_specs, [s0] + args
    return pl.pallas_call(
        kern,
        grid=(b, nhg, t // tc),
        in_specs=in_specs,
        out_specs=[col(0), state_spec],
        out_shape=[jax.ShapeDtypeStruct((b, t, HG_HEADS * HG_DIM), BF16),
                   jax.ShapeDtypeStruct((b, HG_HEADS, HG_DIM, HG_DIM), F32)],
        scratch_shapes=[pltpu.VMEM((hps, HG_DIM, HG_DIM), F32)],
        compiler_params=_cparams(("arbitrary", "arbitrary", "arbitrary")),
        name="hgrn2",
    )(*args)


def _out_kernel(a_ref, m_ref, w_ref, x_ref, ga_ref, o_ref):
    half = a_ref.shape[1]
    mix = _dot(a_ref[...], w_ref[:half, :]) + _dot(m_ref[...], w_ref[half:, :])
    o_ref[...] = x_ref[...] + ga_ref[...] * mix


def _out_proj(a2d, m2d, w_out, x2d, ga, tm, rows_per_mod):
    n, d = x2d.shape
    half = a2d.shape[1]
    r = ga.shape[1]
    tiles_per_mod = rows_per_mod // tm
    return pl.pallas_call(
        _out_kernel,
        grid=(n // tm,),
        in_specs=[
            pl.BlockSpec((tm, half), lambda i: (i, 0)),
            pl.BlockSpec((tm, half), lambda i: (i, 0)),
            pl.BlockSpec((2 * half, d), lambda i: (0, 0)),
            pl.BlockSpec((tm, d), lambda i: (i, 0)),
            pl.BlockSpec((None, r, d), lambda i: (i // tiles_per_mod, 0, 0)),
        ],
        out_specs=pl.BlockSpec((tm, d), lambda i: (i, 0)),
        out_shape=jax.ShapeDtypeStruct((n, d), F32),
        compiler_params=_cparams(("arbitrary",)),
        name="out_proj",
    )(a2d, m2d, w_out.astype(BF16), x2d, ga)


def _router_kernel(x_ref, n2_ref, sc_ref, sh_ref, wr_ref, br_ref, *rest, grouped):
    h = _rms(x_ref[...], n2_ref[...]) * (1.0 + sc_ref[...]) + sh_ref[...]
    if grouped:
        tril_ref, h_ref, idx_ref, rw_ref, cnt_ref, cnt_scr = rest
        h_ref[...] = h
    else:
        h_ref, comb_ref = rest
        h_ref[...] = h.astype(BF16)
    h_hi = h.astype(BF16)
    h_lo = (h - h_hi.astype(F32)).astype(BF16)
    w = wr_ref[...]
    w_hi = w.astype(BF16)
    w_lo = (w - w_hi.astype(F32)).astype(BF16)
    parts = []
    for rs in (slice(0, h.shape[0] // 2), slice(h.shape[0] // 2, h.shape[0])):
        parts.append(_dot(h_hi[rs], w_hi) + _dot(h_hi[rs], w_lo) + _dot(h_lo[rs], w_hi))
    lg = jnp.concatenate(parts, axis=0) + br_ref[...]

    lane = lax.broadcasted_iota(jnp.int32, lg.shape, 1)
    neg = -jnp.inf
    big = jnp.int32(LANES)

    def top(valid):
        m = jnp.max(jnp.where(valid, lg, neg), axis=-1, keepdims=True)
        idx = jnp.min(jnp.where(valid & (lg == m), lane, big), axis=-1, keepdims=True)
        return m, idx

    is_group = lane < N_GROUPS
    gmax, gidx = top(is_group)
    pg_top = 1.0 / jnp.sum(jnp.where(is_group, jnp.exp(lg - gmax), 0.0), axis=-1, keepdims=True)
    lo = ROUTE_LANE0 + EXPERTS_PER_GROUP * gidx
    in_group = (lane >= lo) & (lane < lo + EXPERTS_PER_GROUP)
    l0, e0 = top(in_group)
    l1, e1 = top(in_group & (lane != e0))
    r = jnp.exp(l1 - l0)
    w0 = pg_top / (1.0 + r)
    w1 = w0 * r
    if not grouped:
        comb_ref[...] = jnp.where(lane == e0, w0, 0.0) + jnp.where(lane == e1, w1, 0.0)
        return

    @pl.when(pl.program_id(0) == 0)
    def _():
        cnt_scr[...] = jnp.zeros(cnt_scr.shape, F32)

    is0, is1 = lane == e0, lane == e1
    onehot = jnp.where(is0 | is1, 1.0, 0.0)
    before = _dot(tril_ref[...], onehot.astype(BF16)) + cnt_scr[...]
    rank0 = jnp.sum(jnp.where(is0, before, 0.0), axis=-1, keepdims=True).astype(jnp.int32)
    rank1 = jnp.sum(jnp.where(is1, before, 0.0), axis=-1, keepdims=True).astype(jnp.int32)
    cnt_scr[...] = cnt_scr[...] + jnp.sum(onehot, axis=0, keepdims=True)
    idx_ref[...] = jnp.where(lane == 0, e0 - ROUTE_LANE0, jnp.where(lane == 1, e1 - ROUTE_LANE0,
                             jnp.where(lane == 2, rank0, jnp.where(lane == 3, rank1, 0))))
    rw_ref[...] = jnp.where(lane == 0, w0, jnp.where(lane == 1, w1, 0.0))
    cnt_ref[...] = cnt_scr[...].astype(jnp.int32)


def _router(x1, n2, sc, sh, w_route, b_route, tm, rows_per_mod, grouped):
    n, d = x1.shape
    r = sc.shape[1]
    tiles_per_mod = rows_per_mod // tm
    mod_spec = pl.BlockSpec((None, r, d), lambda i: (i // tiles_per_mod, 0, 0))
    row_spec = lambda w: pl.BlockSpec((tm, w), lambda i: (i, 0))
    in_specs = [row_spec(d), pl.BlockSpec((1, d), lambda i: (0, 0)), mod_spec, mod_spec,
                pl.BlockSpec((d, LANES), lambda i: (0, 0)), pl.BlockSpec((1, LANES), lambda i: (0, 0))]
    args = [x1, n2.reshape(1, d), sc, sh, w_route, b_route]
    if grouped:
        in_specs.append(pl.BlockSpec((tm, tm), lambda i: (0, 0)))
        args.append(jnp.asarray(np.tril(np.ones((tm, tm), np.float32), -1), BF16))
        out_specs = [row_spec(d), row_spec(LANES), row_spec(LANES), pl.BlockSpec((1, LANES), lambda i: (0, 0))]
        out_shape = [jax.ShapeDtypeStruct((n, d), F32), jax.ShapeDtypeStruct((n, LANES), jnp.int32),
                     jax.ShapeDtypeStruct((n, LANES), F32), jax.ShapeDtypeStruct((1, LANES), jnp.int32)]
        scratch = [pltpu.VMEM((1, LANES), F32)]
    else:
        out_specs = [row_spec(d), row_spec(LANES)]
        out_shape = [jax.ShapeDtypeStruct((n, d), BF16), jax.ShapeDtypeStruct((n, LANES), F32)]
        scratch = []
    return pl.pallas_call(
        functools.partial(_router_kernel, grouped=grouped),
        grid=(n // tm,),
        in_specs=in_specs,
        out_specs=out_specs,
        out_shape=out_shape,
        scratch_shapes=scratch,
        compiler_params=_cparams(("arbitrary",)),
        name="router",
    )(*args)


def _dispatch_kernel(pos_ref, pad_ref, h_ref, xs_hbm, zbuf, sem, psem, *, tm, ne):
    i = pl.program_id(0)
    base = i * tm

    def body(r, c):
        t = base + r
        src = h_ref.at[pl.ds(r, 1)]
        pltpu.make_async_copy(src, xs_hbm.at[pl.ds(pos_ref[2 * t], 1)], sem).start()
        pltpu.make_async_copy(src, xs_hbm.at[pl.ds(pos_ref[2 * t + 1], 1)], sem).start()
        return c

    lax.fori_loop(0, tm, body, 0, unroll=8)

    @pl.when(i == pl.num_programs(0) - 1)
    def _():
        zbuf[...] = jnp.zeros(zbuf.shape, F32)
        tile_rows = zbuf.shape[0]

        def pad_copy(e, r):
            return pltpu.make_async_copy(zbuf.at[pl.ds(0, 1)], xs_hbm.at[pl.ds(pad_ref[e] + r, 1)], psem)

        def tail_copy(t):
            return pltpu.make_async_copy(zbuf, xs_hbm.at[pl.ds(pl.multiple_of(t * tile_rows, tile_rows), tile_rows)], psem)

        def for_each_fill(fn):
            def per_expert(e, c):
                def per_row(r, cc):
                    fn(pad_copy(e, r))
                    return cc

                lax.fori_loop(0, pad_ref[ne + e], per_row, 0)
                return c

            def per_tile(t, c):
                fn(tail_copy(t))
                return c

            lax.fori_loop(0, ne, per_expert, 0)
            lax.fori_loop(pad_ref[2 * ne], xs_hbm.shape[0] // tile_rows, per_tile, 0)

        for_each_fill(lambda cp: cp.start())
        for_each_fill(lambda cp: cp.wait())

    for _ in range(2):
        pltpu.make_async_copy(h_ref, xs_hbm.at[pl.ds(0, tm)], sem).wait()


def _dispatch(pos, pad, h, n_rows, tile_rows, ne, tm=512):
    n, w = h.shape
    tm = min(tm, n)
    assert n % tm == 0 and n_rows % tile_rows == 0
    any_spec = pl.BlockSpec(memory_space=pl.ANY)
    return pl.pallas_call(
        functools.partial(_dispatch_kernel, tm=tm, ne=ne),
        grid_spec=pltpu.PrefetchScalarGridSpec(
            num_scalar_prefetch=2, grid=(n // tm,),
            in_specs=[pl.BlockSpec((tm, w), lambda i, p, q: (i, 0))], out_specs=any_spec,
            scratch_shapes=[pltpu.VMEM((tile_rows, w), F32), pltpu.SemaphoreType.DMA(()),
                            pltpu.SemaphoreType.DMA(())]),
        out_shape=jax.ShapeDtypeStruct((n_rows, w), F32),
        compiler_params=_cparams(("arbitrary",)),
        name="moe_dispatch",
    )(pos, pad, h)


def _expert_kernel(te_ref, nv_ref, x_ref, wg_ref, wu_ref, wd_ref, y_ref):
    del te_ref
    i = pl.program_id(0)

    @pl.when(i < nv_ref[0])
    def _():
        x = x_ref[...].astype(BF16)
        a = _dot(x, wg_ref[...].astype(BF16))
        u = _dot(x, wu_ref[...].astype(BF16))
        hid = a * _sigmoid(a) * u
        y_ref[...] = _dot(hid.astype(BF16), wd_ref[...].astype(BF16))

    @pl.when(i >= nv_ref[0])
    def _():
        y_ref[...] = jnp.zeros(y_ref.shape, F32)


def _experts(tile_expert, n_valid, xs, w_gate, w_up, w_down, tm):
    rows, half = xs.shape
    ne, d, f = w_gate.shape
    w_spec = lambda a, b: pl.BlockSpec((None, a, b), lambda i, te, nv: (te[i], 0, 0))
    return pl.pallas_call(
        _expert_kernel,
        grid_spec=pltpu.PrefetchScalarGridSpec(
            num_scalar_prefetch=2, grid=(rows // tm,),
            in_specs=[pl.BlockSpec((tm, half), lambda i, te, nv: (jnp.minimum(i, nv[0] - 1), 0)),
                      w_spec(d, f), w_spec(d, f), w_spec(f, d)],
            out_specs=pl.BlockSpec((tm, d), lambda i, te, nv: (i, 0))),
        out_shape=jax.ShapeDtypeStruct((rows, d), F32),
        compiler_params=_cparams(("arbitrary",)),
        name="moe_experts",
    )(tile_expert, n_valid, xs, w_gate, w_up, w_down)


def _combine_kernel(pos_ref, ys_hbm, x_ref, ga_ref, rw_ref, o_ref, buf, sem, *, tm):
    i = pl.program_id(0)
    n = pl.num_programs(0)

    def issue(tile, slot):
        def body(r, c):
            t = tile * tm + r
            pltpu.make_async_copy(ys_hbm.at[pl.ds(pos_ref[2 * t], 1)], buf.at[slot, pl.ds(r, 1)],
                                  sem.at[slot]).start()
            pltpu.make_async_copy(ys_hbm.at[pl.ds(pos_ref[2 * t + 1], 1)], buf.at[slot, pl.ds(tm + r, 1)],
                                  sem.at[slot]).start()
            return c

        lax.fori_loop(0, tm, body, 0, unroll=8)

    @pl.when(i == 0)
    def _():
        issue(0, 0)

    @pl.when(i + 1 < n)
    def _():
        issue(i + 1, (i + 1) % 2)

    slot = i % 2
    pltpu.make_async_copy(ys_hbm.at[pl.ds(0, 2 * tm)], buf.at[slot], sem.at[slot]).wait()
    w = rw_ref[...]
    y = w[:, 0:1] * buf[slot, 0:tm] + w[:, 1:2] * buf[slot, tm:2 * tm]
    o_ref[...] = x_ref[...] + ga_ref[...] * y


def _combine(pos, ys, x1, ga, rw, tm, rows_per_mod):
    n, d = x1.shape
    r = ga.shape[1]
    tiles_per_mod = rows_per_mod // tm
    return pl.pallas_call(
        functools.partial(_combine_kernel, tm=tm),
        grid_spec=pltpu.PrefetchScalarGridSpec(
            num_scalar_prefetch=1, grid=(n // tm,),
            in_specs=[pl.BlockSpec(memory_space=pl.ANY),
                      pl.BlockSpec((tm, d), lambda i, p: (i, 0)),
                      pl.BlockSpec((None, r, d), lambda i, p: (i // tiles_per_mod, 0, 0)),
                      pl.BlockSpec((tm, LANES), lambda i, p: (i, 0))],
            out_specs=pl.BlockSpec((tm, d), lambda i, p: (i, 0)),
            scratch_shapes=[pltpu.VMEM((2, 2 * tm, d), F32), pltpu.SemaphoreType.DMA((2,))]),
        out_shape=jax.ShapeDtypeStruct((n, d), F32),
        compiler_params=_cparams(("arbitrary",)),
        name="moe_combine",
    )(pos, ys, x1, ga, rw)


def _grouped_moe(hp, idx, rw, cnt, w_gate, w_up, w_down, x1, ga, rows_per_mod, tm_e=256, tm_c=256):
    n = hp.shape[0]
    ne = w_gate.shape[0]
    tm_e = min(tm_e, n)
    counts = cnt[0, ROUTE_LANE0:ROUTE_LANE0 + ne]
    tiles = (counts + tm_e - 1) // tm_e
    tile_end = jnp.cumsum(tiles)
    start_row = (tile_end - tiles) * tm_e
    n_tiles = (2 * n) // tm_e + ne
    tile_expert = jnp.minimum(jnp.sum(tile_end[None, :] <= jnp.arange(n_tiles)[:, None], axis=1), ne - 1)
    pos = (jnp.take(start_row, idx[:, 0:2], axis=0) + idx[:, 2:4]).reshape(-1).astype(jnp.int32)
    pad = jnp.concatenate([start_row + counts, tiles * tm_e - counts, tile_end[ne - 1:]]).astype(jnp.int32)
    xs = _dispatch(pos, pad, hp, n_tiles * tm_e, tm_e, ne)
    ys = _experts(tile_expert.astype(jnp.int32), tile_end[ne - 1:].astype(jnp.int32), xs, w_gate, w_up, w_down, tm_e)
    return _combine(pos, ys, x1, ga, rw, min(tm_c, n), rows_per_mod)


def _moe_kernel(h_ref, comb_ref, wg_ref, wu_ref, wd_ref, x_ref, ga_ref, o_ref, acc):
    e = pl.program_id(1)

    @pl.when(e == 0)
    def _():
        acc[...] = jnp.zeros(acc.shape, F32)

    h = h_ref[...]
    a = _dot(h, wg_ref[...].astype(BF16))
    u = _dot(h, wu_ref[...].astype(BF16))
    lane = lax.broadcasted_iota(jnp.int32, comb_ref.shape, 1)
    cw = jnp.sum(jnp.where(lane == e + ROUTE_LANE0, comb_ref[...], 0.0), axis=-1, keepdims=True)
    hid = a * _sigmoid(a) * u * cw
    acc[...] += _dot(hid.astype(BF16), wd_ref[...].astype(BF16))

    @pl.when(e == pl.num_programs(1) - 1)
    def _():
        o_ref[...] = x_ref[...] + ga_ref[...] * acc[...]


def _moe(h2, comb, w_gate, w_up, w_down, x1, ga, tm, rows_per_mod):
    n, d = x1.shape
    ne, _, f = w_gate.shape
    r = ga.shape[1]
    tiles_per_mod = rows_per_mod // tm
    return pl.pallas_call(
        _moe_kernel,
        grid=(n // tm, ne),
        in_specs=[
            pl.BlockSpec((tm, d), lambda i, e: (i, 0)),
            pl.BlockSpec((tm, LANES), lambda i, e: (i, 0)),
            pl.BlockSpec((None, d, f), lambda i, e: (e, 0, 0)),
            pl.BlockSpec((None, d, f), lambda i, e: (e, 0, 0)),
            pl.BlockSpec((None, f, d), lambda i, e: (e, 0, 0)),
            pl.BlockSpec((tm, d), lambda i, e: (i, 0)),
            pl.BlockSpec((None, r, d), lambda i, e: (i // tiles_per_mod, 0, 0)),
        ],
        out_specs=pl.BlockSpec((tm, d), lambda i, e: (i, 0)),
        out_shape=jax.ShapeDtypeStruct((n, d), F32),
        scratch_shapes=[pltpu.VMEM((tm, d), F32)],
        compiler_params=_cparams(("arbitrary", "arbitrary")),
        name="moe",
    )(h2, comb, w_gate, w_up, w_down, x1, ga)


def _pick_tile(n, pref):
    t = min(pref, n)
    while n % t:
        t //= 2
    return t


def _layer(x, mods, s0, attend, lw, lam_init, hg_chunk, hg_heads_per_step, per_row_mod):
    (n1, n2, w_in, gq, gk, subln, lam4, hlb, hg_g, w_out, w_route, b_route, w_gate, w_up, w_down) = lw
    b, t, d = x.shape
    n = b * t
    x2d = x.reshape(n, d)
    if per_row_mod:
        tm = n
        rows_per_mod = n
        mods = [jnp.repeat(m, t, axis=0).reshape(1, n, d) for m in mods]
    else:
        tm = _pick_tile(t, 512)
        rows_per_mod = t
        mods = [m.reshape(b, 1, d) for m in mods]
    sh1, sc1, ga1, sh2, sc2, ga2 = mods

    tm_in = tm if per_row_mod else _pick_tile(t, 1024)
    q, k, v, hmain, hk = _in_proj(x2d, n1, sc1, sh1, w_in, gq, gk, hlb, tm_in, rows_per_mod)
    a = attend(q.reshape(b, t, -1), k.reshape(b, t, -1), v.reshape(b, t, -1))
    tpad = -(-t // hg_chunk) * hg_chunk
    hm3, hk3 = hmain.reshape(b, t, -1), hk.reshape(b, t, -1)
    if tpad != t:
        hm3 = jnp.pad(hm3, ((0, 0), (0, tpad - t), (0, 0)))
        hk3 = jnp.pad(hk3, ((0, 0), (0, tpad - t), (0, 0)))
    m, s_new = _hgrn2(hm3, hk3, hg_g, s0, hg_chunk, _pick_tile(tpad, 2 * hg_chunk), hg_heads_per_step)
    m = m[:, :t]
    x1 = _out_proj(a.reshape(n, -1), m.reshape(n, -1), w_out, x2d, ga1, tm, rows_per_mod)
    if per_row_mod:
        h2, comb = _router(x1, n2, sc2, sh2, w_route, b_route, tm, rows_per_mod, False)
        y = _moe(h2, comb, w_gate, w_up, w_down, x1, ga2, tm, rows_per_mod)
    else:
        hp, idx, rw, cnt = _router(x1, n2, sc2, sh2, w_route, b_route, tm, rows_per_mod, True)
        y = _grouped_moe(hp, idx, rw, cnt, w_gate, w_up, w_down, x1, ga2, rows_per_mod)
    return y.reshape(b, t, d), k, v, s_new


def kernel(x_prompt, x_sample, cache_k, cache_v, state_hgrn, page_table, c_prompt, c_sample, norm1_g, norm2_g, w_ada, b_ada, w_in, q_norm_g, k_norm_g, lambda_q1, lambda_k1, lambda_q2, lambda_k2, subln_g, hg_lower_bound, hg_norm_g, w_out, w_router_group, b_router_group, w_router_expert, b_router_expert, w_exp_gate, w_exp_up, w_exp_down):
    depth = norm1_g.shape[0]
    assert depth == 1, "single-layer trunk"
    l = 0
    lam_init = 0.8 - 0.6 * math.exp(-0.3 * l)
    bp, tp, d = x_prompt.shape
    bs, ts, _ = x_sample.shape

    c_all = jnp.concatenate([c_prompt, c_sample], axis=0)
    rpad = -(-c_all.shape[0] // 8) * 8
    c_all = jnp.pad(c_all, ((0, rpad - c_all.shape[0]), (0, 0)))
    mod = _adaln(c_all, w_ada[l], b_ada[l])
    mods_p = [mod[:bp, i * d:(i + 1) * d] for i in range(6)]
    mods_s = [mod[bp:bp + bs, i * d:(i + 1) * d] for i in range(6)]

    w_re = jnp.transpose(w_router_expert[l], (1, 0, 2)).reshape(d, N_EXPERTS)
    w_route = jnp.pad(jnp.concatenate([w_router_group[l], w_re], axis=1), ((0, 0), (0, LANES - N_GROUPS - N_EXPERTS)))
    b_route = jnp.pad(jnp.concatenate([b_router_group[l], b_router_expert[l].reshape(-1)]),
                      (0, LANES - N_GROUPS - N_EXPERTS)).reshape(1, LANES)
    lam4 = jnp.stack([lambda_q1[l], lambda_k1[l], lambda_q2[l], lambda_k2[l]], axis=0)

    lw = (norm1_g[l], norm2_g[l], w_in[l], q_norm_g[l], k_norm_g[l], subln_g[l], lam4, hg_lower_bound,
          hg_norm_g[l], w_out[l], w_route, b_route, w_exp_gate[l], w_exp_up[l], w_exp_down[l])

    def attend_prompt(q, k, v):
        return _prompt_attention(q, k, v, lam4, subln_g[l], lam_init)

    def attend_sample(q, k, v):
        pad = lambda a: jnp.pad(a, ((0, 0), (0, -ts % 8), (0, 0)))
        o = _sample_attention(pad(q), pad(k), pad(v), ts, cache_k[l], cache_v[l], page_table, lam4,
                              subln_g[l], lam_init)
        return o[:, :ts]

    yp, kp, vp, sp = _layer(x_prompt, mods_p, None, attend_prompt, lw, lam_init, 128, 4, False)
    ys, kn, vn, sn = _layer(x_sample, mods_s, state_hgrn[l], attend_sample, lw, lam_init, 8, HG_HEADS, True)

    kv_shape = lambda b, t: (1, b, t, ATTN_KV_HEADS, 2 * QK_DIM)
    return (yp, ys,
            kp.reshape(kv_shape(bp, tp)), vp.reshape(kv_shape(bp, tp)),
            kn.reshape(kv_shape(bs, ts)), vn.reshape(kv_shape(bs, ts)),
            sp[None], sn[None])
```

```python
import functools
import math

import jax
import jax.numpy as jnp
import numpy as np
from jax import lax
from jax.experimental import pallas as pl
from jax.experimental.pallas import tpu as pltpu

F32 = jnp.float32
BF16 = jnp.bfloat16

QK_DIM = 64
V_DIM = 128
ATTN_KV_HEADS = 4
ATTN_GROUP = 2
ATTN_HEADS = ATTN_KV_HEADS * ATTN_GROUP
HG_HEADS = 8
HG_DIM = 128
N_GROUPS = 4
EXPERTS_PER_GROUP = 4
N_EXPERTS = N_GROUPS * EXPERTS_PER_GROUP
NORM_EPS = 1e-6
ALIBI_MAX_BIAS = 8.0
LANES = 128
MXU_DIM = 256
VMEM_LIMIT = 60 * 1024 * 1024
ROUTE_LANE0 = N_GROUPS
HG_TABLE_LEVELS = 2


def _cparams(sem):
    return pltpu.CompilerParams(dimension_semantics=sem, vmem_limit_bytes=VMEM_LIMIT)


def _dot(a, b):
    return jnp.dot(a, b, preferred_element_type=F32)


def _dot_nt(a, b):
    return lax.dot_general(a, b, (((1,), (1,)), ((), ())), preferred_element_type=F32)


def _dot_tn(a, b):
    return lax.dot_general(a, b, (((0,), (0,)), ((), ())), preferred_element_type=F32)


def _split3(x):
    hi = x.astype(BF16)
    r = x - hi.astype(F32)
    mid = r.astype(BF16)
    lo = (r - mid.astype(F32)).astype(BF16)
    return hi, mid, lo


def _sigmoid(x):
    return 1.0 / (1.0 + jnp.exp(-x))


def _rms(x, g):
    return x * lax.rsqrt(jnp.mean(x * x, axis=-1, keepdims=True) + NORM_EPS) * g


def _ada_kernel(c_ref, w_ref, b_ref, o_ref):
    c = c_ref[...]
    s = c * _sigmoid(c)
    o_ref[...] = _dot(s.astype(BF16), w_ref[...].astype(BF16)) + b_ref[...]


def _adaln(c_all, w_ada, b_ada, tn=1024):
    r, d = c_all.shape
    n = w_ada.shape[1]
    return pl.pallas_call(
        _ada_kernel,
        grid=(n // tn,),
        in_specs=[
            pl.BlockSpec((r, d), lambda j: (0, 0)),
            pl.BlockSpec((d, tn), lambda j: (0, j)),
            pl.BlockSpec((1, tn), lambda j: (0, j)),
        ],
        out_specs=pl.BlockSpec((r, tn), lambda j: (0, j)),
        out_shape=jax.ShapeDtypeStruct((r, n), F32),
        compiler_params=_cparams(("arbitrary",)),
        name="adaln",
    )(c_all, w_ada, b_ada.reshape(1, n))


IN_TN = 512


def _group_sumsq(p, bd):
    x2 = p * p
    hi = x2.astype(BF16)
    lo = (x2 - hi.astype(F32)).astype(BF16)
    outs = []
    for c in range(p.shape[1] // MXU_DIM):
        sl = slice(c * MXU_DIM, (c + 1) * MXU_DIM)
        outs.append(_dot(hi[:, sl], bd) + _dot(lo[:, sl], bd))
    return jnp.concatenate(outs, axis=1)


def _in_epilogue(c, p, gq_ref, gk_ref, lb_ref, bd_ref, q_ref, k_ref, v_ref, hm_ref, hk_ref):
    def qk_norm(g):
        ms = _group_sumsq(p, bd_ref[...]) * (1.0 / QK_DIM)
        return p * lax.rsqrt(ms + NORM_EPS) * g

    if c < 2:
        q_ref[...] = (qk_norm(gq_ref[...]) * (QK_DIM ** -0.5)).astype(BF16)
    elif c == 2:
        k_ref[...] = qk_norm(gk_ref[...])
    elif c == 3:
        v_ref[...] = p
    elif c in (4, 5, 10, 11):
        hm_ref[...] = p * _sigmoid(p)
    elif c in (6, 7):
        a = lb_ref[...]
        e = jnp.exp(a - jnp.max(a, axis=0, keepdims=True))
        lb = e[0:1] / jnp.sum(e, axis=0, keepdims=True)
        sg = _sigmoid(p)
        hm_ref[...] = jnp.log(lb + (1.0 - lb) * sg)
        hk_ref[...] = (1.0 - lb) * (1.0 - sg)
    else:
        hm_ref[...] = p


def _in_kernel(x_ref, n1_ref, sc_ref, sh_ref, w_ref, gq_ref, gk_ref, lb_ref, bd_ref,
               q_ref, k_ref, v_ref, hm_ref, hk_ref, h_scr, *, nj):
    j = pl.program_id(1)

    @pl.when(j == 0)
    def _():
        y = _rms(x_ref[...], n1_ref[...])
        h_scr[...] = (y * (1.0 + sc_ref[...]) + sh_ref[...]).astype(BF16)

    p = _dot(h_scr[...], w_ref[...].astype(BF16))
    outs = (gq_ref, gk_ref, lb_ref, bd_ref, q_ref, k_ref, v_ref, hm_ref, hk_ref)
    for cols in ((0, 1), (2,), (3,), (4, 5, 10, 11), (6, 7), (8, 9)):
        cond = functools.reduce(lambda a, b: a | b, [j == c for c in cols])

        @pl.when(cond)
        def _(c=cols[0]):
            _in_epilogue(c, p, *outs)


def _in_proj(x2d, n1, sc, sh, w_in, gq, gk, hlb, tm, rows_per_mod):
    n, d = x2d.shape
    cols = w_in.shape[1]
    nj = cols // IN_TN
    assert nj == 12 and n % tm == 0
    r = sc.shape[1]
    tiles_per_mod = rows_per_mod // tm
    bd = jnp.asarray(np.kron(np.eye(MXU_DIM // QK_DIM), np.ones((QK_DIM, QK_DIM))), BF16)
    clamp = lambda j, lo: jnp.clip(j - lo, 0, 1)
    mod_spec = pl.BlockSpec((None, r, d), lambda i, j: (i // tiles_per_mod, 0, 0))
    return pl.pallas_call(
        functools.partial(_in_kernel, nj=nj),
        grid=(n // tm, nj),
        in_specs=[
            pl.BlockSpec((tm, d), lambda i, j: (i, 0)),
            pl.BlockSpec((1, d), lambda i, j: (0, 0)),
            mod_spec, mod_spec,
            pl.BlockSpec((d, IN_TN), lambda i, j: (0, j)),
            pl.BlockSpec((1, IN_TN), lambda i, j: (0, 0)),
            pl.BlockSpec((1, IN_TN), lambda i, j: (0, 0)),
            pl.BlockSpec((hlb.shape[0], IN_TN), lambda i, j: (0, clamp(j, 6))),
            pl.BlockSpec((MXU_DIM, MXU_DIM), lambda i, j: (0, 0)),
        ],
        out_specs=[
            pl.BlockSpec((tm, IN_TN), lambda i, j: (i, clamp(j, 0))),
            pl.BlockSpec((tm, IN_TN), lambda i, j: (i, 0)),
            pl.BlockSpec((tm, IN_TN), lambda i, j: (i, 0)),
            pl.BlockSpec((tm, IN_TN), lambda i, j: (i, jnp.clip(j - 4, 0, 7))),
            pl.BlockSpec((tm, IN_TN), lambda i, j: (i, clamp(j, 6))),
        ],
        out_shape=[
            jax.ShapeDtypeStruct((n, 2 * IN_TN), BF16),
            jax.ShapeDtypeStruct((n, IN_TN), F32),
            jax.ShapeDtypeStruct((n, IN_TN), F32),
            jax.ShapeDtypeStruct((n, 8 * IN_TN), F32),
            jax.ShapeDtypeStruct((n, 2 * IN_TN), F32),
        ],
        scratch_shapes=[pltpu.VMEM((tm, d), BF16)],
        compiler_params=_cparams(("arbitrary", "arbitrary")),
        name="in_proj",
    )(x2d, n1.reshape(1, d), sc, sh, w_in,
      jnp.tile(gq.reshape(1, QK_DIM), (1, IN_TN // QK_DIM)),
      jnp.tile(gk.reshape(1, QK_DIM), (1, IN_TN // QK_DIM)), hlb, bd)


def _alibi_slopes_np():
    h = np.arange(1, ATTN_HEADS + 1, dtype=np.float64)
    return np.exp2(-ALIBI_MAX_BIAS * h / ATTN_HEADS).reshape(ATTN_KV_HEADS, ATTN_GROUP).astype(np.float32)


def _alibi_slopes():
    out = np.zeros((ATTN_KV_HEADS, 1, LANES), np.float32)
    out[:, 0, :ATTN_GROUP] = _alibi_slopes_np()
    return jnp.asarray(out)


def _lambda(lam_ref, lam_init):
    l = lam_ref[...]
    s1 = jnp.sum(l[0:1] * l[1:2], axis=-1, keepdims=True)
    s2 = jnp.sum(l[2:3] * l[3:4], axis=-1, keepdims=True)
    return jnp.exp(s1) - jnp.exp(s2) + lam_init


def _diff_finish(o_all, t, lam, subln, lam_init, o_ref):
    for g in range(ATTN_GROUP):
        o = o_all[(2 * g) * t:(2 * g + 1) * t] - lam * o_all[(2 * g + 1) * t:(2 * g + 2) * t]
        y = _rms(o, subln) * (1.0 - lam_init)
        o_ref[:, g * LANES:(g + 1) * LANES] = y.astype(o_ref.dtype)


def _softmax_step_t(s, m_old, l_old):
    m_new = jnp.maximum(m_old, jnp.max(s, axis=0, keepdims=True))
    alpha = jnp.exp(m_old - m_new)
    p = jnp.exp(s - m_new)
    return m_new, alpha, alpha * l_old + jnp.sum(p, axis=0, keepdims=True), p.astype(BF16)


def _pattn_kernel(sl_ref, lam_ref, subln_ref, q_ref, k_ref, v_ref, o_ref, kb, vt, qs, s_a, s_b, m_scr, l_scr, acc,
                  *, tq, nblk, lam_init):
    qi = pl.program_id(2)

    @pl.when(qi == 0)
    def _():
        t = k_ref.shape[0]
        pos = lax.broadcasted_iota(jnp.int32, (t, LANES), 0)
        ln = lax.broadcasted_iota(jnp.int32, (t, LANES), 1)
        aug = jnp.where(ln == 0, pos & ~(LANES - 1), jnp.where(ln == 1, pos & (LANES - 1), 0))
        kb[:, :LANES] = k_ref[...].astype(BF16)
        kb[:, LANES:] = aug.astype(F32).astype(BF16)
        for jb in range(t // tq):
            vt[jb] = v_ref[jb * tq:(jb + 1) * tq, :].T.astype(BF16)

    rows = 4 * tq
    half = 2 * tq
    q = q_ref[...]
    lane = lax.broadcasted_iota(jnp.int32, (tq, LANES), 1)
    zero = jnp.zeros((tq, LANES), BF16)
    for g in range(ATTN_GROUP):
        qg = q[:, g * LANES:(g + 1) * LANES]
        aug = jnp.where(lane < 2, sl_ref[0:1, g:g + 1], 0.0).astype(BF16)
        qs[(2 * g) * tq:(2 * g + 1) * tq, :] = jnp.concatenate([jnp.where(lane < QK_DIM, qg, zero), aug], axis=1)
        qs[(2 * g + 1) * tq:(2 * g + 2) * tq, :] = jnp.concatenate([jnp.where(lane >= QK_DIM, qg, zero), aug], axis=1)
    t_q = lax.broadcasted_iota(jnp.int32, (1, half), 1) % tq
    m_scr[...] = jnp.full(m_scr.shape, -jnp.inf, F32)
    l_scr[...] = jnp.zeros(l_scr.shape, F32)
    acc[...] = jnp.zeros(acc.shape, F32)

    s_bufs = (s_a, s_b)

    def logits(jb):
        s_bufs[jb % 2][...] = _dot_nt(kb[jb * tq:(jb + 1) * tq, :], qs[...])

    def attend(jb, masked):
        for g in range(ATTN_GROUP):
            cs = slice(g * half, (g + 1) * half)
            s = s_bufs[jb % 2][:, cs]
            if masked:
                key = lax.broadcasted_iota(jnp.int32, (tq, 1), 0)
                s = jnp.where(key <= t_q, s, -jnp.inf)
            m_new, alpha, l_new, p = _softmax_step_t(s, m_scr[:, cs], l_scr[:, cs])
            acc[:, cs] = alpha * acc[:, cs] + _dot(vt[jb], p)
            m_scr[:, cs] = m_new
            l_scr[:, cs] = l_new

    logits(0)
    for jb in range(nblk):
        if jb + 1 < nblk:
            @pl.when(jb < qi)
            def _(jb=jb):
                logits(jb + 1)
                attend(jb, False)

        @pl.when(jb == qi)
        def _(jb=jb):
            attend(jb, True)

    o_t = acc[...] * (1.0 / l_scr[...])
    lam = _lambda(lam_ref, lam_init)
    for g in range(ATTN_GROUP):
        d = o_t[:, (2 * g) * tq:(2 * g + 1) * tq] - lam * o_t[:, (2 * g + 1) * tq:(2 * g + 2) * tq]
        y = _rms(d.T, subln_ref[...]) * (1.0 - lam_init)
        o_ref[:, g * LANES:(g + 1) * LANES] = y.astype(o_ref.dtype)


def _prompt_attention(q, k, v, lam4, subln, lam_init, tq=512):
    b, t, _ = q.shape
    tq = min(tq, t)
    assert t % tq == 0 and t < LANES * 256
    rows = 4 * tq
    kern = functools.partial(_pattn_kernel, tq=tq, nblk=t // tq, lam_init=lam_init)
    return pl.pallas_call(
        kern,
        grid=(b, ATTN_KV_HEADS, t // tq),
        in_specs=[
            pl.BlockSpec((None, 1, LANES), lambda bi, h, qi: (h, 0, 0)),
            pl.BlockSpec((4, QK_DIM), lambda bi, h, qi: (0, 0)),
            pl.BlockSpec((1, V_DIM), lambda bi, h, qi: (0, 0)),
            pl.BlockSpec((None, tq, 2 * LANES), lambda bi, h, qi: (bi, qi, h)),
            pl.BlockSpec((None, t, LANES), lambda bi, h, qi: (bi, 0, h)),
            pl.BlockSpec((None, t, LANES), lambda bi, h, qi: (bi, 0, h)),
        ],
        out_specs=pl.BlockSpec((None, tq, 2 * LANES), lambda bi, h, qi: (bi, qi, h)),
        out_shape=jax.ShapeDtypeStruct((b, t, ATTN_HEADS * V_DIM), BF16),
        scratch_shapes=[
            pltpu.VMEM((t, 2 * LANES), BF16), pltpu.VMEM((t // tq, V_DIM, tq), BF16),
            pltpu.VMEM((rows, 2 * LANES), BF16), pltpu.VMEM((tq, rows), F32), pltpu.VMEM((tq, rows), F32),
            pltpu.VMEM((1, rows), F32), pltpu.VMEM((1, rows), F32), pltpu.VMEM((V_DIM, rows), F32),
        ],
        compiler_params=_cparams(("arbitrary", "arbitrary", "arbitrary")),
        name="prompt_attention",
    )(_alibi_slopes(), lam4, subln.reshape(1, V_DIM), q, k, v)


def _to_col(row):
    n = row.shape[1]
    eye = lax.broadcasted_iota(jnp.int32, (n, n), 0) == lax.broadcasted_iota(jnp.int32, (n, n), 1)
    return jnp.sum(jnp.where(eye, row, 0.0), axis=1, keepdims=True)


def _sattn_kernel(pt_ref, srow_ref, lam_ref, subln_ref, q_ref, kn_ref, vn_ref, *rest,
                  npg, nj, page, t, t_real, lam_init):
    del pt_ref
    kp_refs, vp_refs = rest[:npg], rest[npg:2 * npg]
    o_ref = rest[2 * npg]
    kc, vc, qf_scr, qbd_scr, m_scr, l_scr, acc = rest[2 * npg + 1:]
    j = pl.program_id(1)
    ntok = npg * page
    past = nj * ntok
    nq = qbd_scr.shape[0]
    width = ATTN_KV_HEADS * LANES
    srow = srow_ref[...]

    @pl.when(j == 0)
    def _():
        m_scr[...] = jnp.full(m_scr.shape, -jnp.inf, F32)
        l_scr[...] = jnp.zeros(l_scr.shape, F32)
        acc[...] = jnp.zeros(acc.shape, F32)
        qf = q_ref[...].astype(F32)
        lane = lax.broadcasted_iota(jnp.int32, (t, LANES), 1)
        qf_scr[...] = jnp.zeros(qf_scr.shape, F32)
        for hg in range(ATTN_HEADS):
            h = hg // ATTN_GROUP
            qh = qf[:, hg * LANES:(hg + 1) * LANES]
            qf_scr[(2 * hg) * t:(2 * hg + 1) * t, h * LANES:(h + 1) * LANES] = jnp.where(lane < QK_DIM, qh, 0.0)
            qf_scr[(2 * hg + 1) * t:(2 * hg + 2) * t, h * LANES:(h + 1) * LANES] = jnp.where(lane >= QK_DIM, qh, 0.0)
        qbd_scr[...] = qf_scr[...].astype(BF16)

    qbd = qbd_scr[...]

    for p in range(npg):
        kc[p * page:(p + 1) * page, :] = kp_refs[p][...].reshape(page, width).astype(BF16)
    kpos = lax.broadcasted_iota(jnp.int32, (ntok, nq), 0) + j * ntok
    hw = ntok // 2
    s = jnp.concatenate([_dot_nt(kc[:hw, :], qbd), _dot_nt(kc[hw:, :], qbd)], axis=0)
    s = s + kpos.astype(F32) * srow
    for p in range(npg):
        vc[p * page:(p + 1) * page, :] = vp_refs[p][...].reshape(page, width).astype(BF16)
    m_new, alpha, l_new, p = _softmax_step_t(s, m_scr[...], l_scr[...])
    pv = _dot_tn(p[:hw], vc[:hw, :]) + _dot_tn(p[hw:], vc[hw:, :])
    acc[...] = _to_col(alpha) * acc[...] + pv
    m_scr[...] = m_new
    l_scr[...] = l_new

    @pl.when(j == nj - 1)
    def _():
        t_q = lax.broadcasted_iota(jnp.int32, (1, nq), 1) % t
        c = lax.broadcasted_iota(jnp.int32, (t, 1), 0)
        sn = _dot_nt(kn_ref[...].astype(BF16), qbd) + (past + c).astype(F32) * srow
        sn = jnp.where((c <= t_q) & (c < t_real), sn, -jnp.inf)
        m_fin, alpha_n, l_fin, _ = _softmax_step_t(sn, m_scr[...], l_scr[...])
        pn = jnp.exp(sn - m_fin)
        o = _to_col(alpha_n) * acc[...]
        vn = vn_ref[...]
        for u in range(t_real):
            o = o + _to_col(pn[u:u + 1]) * vn[u:u + 1]
        o = o * _to_col(1.0 / l_fin)
        lam = _lambda(lam_ref, lam_init)
        rq = nq // ATTN_KV_HEADS
        for h in range(ATTN_KV_HEADS):
            o_h = o[h * rq:(h + 1) * rq, h * LANES:(h + 1) * LANES]
            _diff_finish(o_h, t, lam, subln_ref[...], lam_init, o_ref.at[:, 2 * h * LANES:(2 * h + 2) * LANES])


def _sample_attention(q, k_new, v_new, t_real, cache_k, cache_v, page_table, lam4, subln, lam_init, npg=32):
    b, t, _ = q.shape
    tpad = t
    page = cache_k.shape[1]
    n_pages = page_table.shape[1]
    npg = min(npg, n_pages)
    assert n_pages % npg == 0
    ntok = npg * page
    width = ATTN_KV_HEADS * LANES
    nq = ATTN_KV_HEADS * ATTN_GROUP * 2 * t
    nj = n_pages // npg
    kern = functools.partial(_sattn_kernel, npg=npg, nj=nj, page=page, t=t, t_real=t_real, lam_init=lam_init)
    slope_row = jnp.asarray(np.repeat(_alibi_slopes_np().reshape(-1), 2 * t)[None, :])

    def page_spec(p):
        return pl.BlockSpec((None, page, ATTN_KV_HEADS, LANES),
                            lambda bi, j, pt: (pt[bi * n_pages + j * npg + p], 0, 0, 0))

    grid_spec = pltpu.PrefetchScalarGridSpec(
        num_scalar_prefetch=1,
        grid=(b, nj),
        in_specs=[
            pl.BlockSpec((1, nq), lambda bi, j, pt: (0, 0)),
            pl.BlockSpec((4, QK_DIM), lambda bi, j, pt: (0, 0)),
            pl.BlockSpec((1, V_DIM), lambda bi, j, pt: (0, 0)),
            pl.BlockSpec((None, t, ATTN_HEADS * V_DIM), lambda bi, j, pt: (bi, 0, 0)),
            pl.BlockSpec((None, tpad, width), lambda bi, j, pt: (bi, 0, 0)),
            pl.BlockSpec((None, tpad, width), lambda bi, j, pt: (bi, 0, 0)),
        ] + [page_spec(p) for p in range(npg)] * 2,
        out_specs=pl.BlockSpec((None, t, ATTN_HEADS * V_DIM), lambda bi, j, pt: (bi, 0, 0)),
        scratch_shapes=[
            pltpu.VMEM((ntok, width), BF16), pltpu.VMEM((ntok, width), BF16),
            pltpu.VMEM((nq, width), F32), pltpu.VMEM((nq, width), BF16),
            pltpu.VMEM((1, nq), F32), pltpu.VMEM((1, nq), F32), pltpu.VMEM((nq, width), F32),
        ],
    )
    return pl.pallas_call(
        kern,
        grid_spec=grid_spec,
        out_shape=jax.ShapeDtypeStruct((b, t, ATTN_HEADS * V_DIM), BF16),
        compiler_params=_cparams(("arbitrary", "arbitrary")),
        name="sample_attention",
    )(page_table.reshape(-1), slope_row, lam4, subln.reshape(1, V_DIM), q, k_new, v_new,
      *([cache_k] * npg), *([cache_v] * npg))


def _hgrn_tables(c):
    levels = int(math.log2(c))
    tril = np.tril(np.ones((c, c), np.float32))
    w = [tril]
    mask = [np.eye(c, dtype=np.float32)]
    idx = np.arange(c)
    for l in range(levels):
        bs = 2 << l
        mid = (idx // bs) * bs + bs // 2
        if l < HG_TABLE_LEVELS:
            w.append(tril - tril[mid])
        upper = (idx % bs) >= bs // 2
        same = (idx[:, None] // bs) == (idx[None, :] // bs)
        mask.append((same & upper[:, None] & ~upper[None, :]).astype(np.float32))
    return jnp.asarray(np.concatenate(w, 0), BF16), jnp.asarray(np.stack(mask, 0)), levels


def _minus_mid_rows(g, bs):
    c, n = g.shape
    g3 = g.reshape(c // bs, bs, n)
    return (g3 - g3[:, bs // 2:bs // 2 + 1, :]).reshape(c, n)


def _hgrn_kernel(*refs, c, levels, nchunk, hps, has_s0):
    if has_s0:
        s0_ref, refs = refs[0], refs[1:]
    (w_ref, mask_ref, g_ref, q_ref, lf_ref, v_ref, gate_ref, k_ref, o_ref, sn_ref, st) = refs
    ci = pl.program_id(2)

    @pl.when(ci == 0)
    def _():
        for hh in range(hps):
            st[hh] = s0_ref[hh].T if has_s0 else jnp.zeros((HG_DIM, HG_DIM), F32)

    w = w_ref[...]
    pair = 2 * HG_DIM
    for n in range(nchunk):
        rs = slice(n * c, (n + 1) * c)
        for hh in range(hps):
            hs = slice(hh * HG_DIM, (hh + 1) * HG_DIM)
            if hh % 2 == 0:
                ps = slice(hh * HG_DIM, hh * HG_DIM + pair)
                l_hi, l_mid, l_lo = _split3(lf_ref[rs, ps])
                gx2 = _dot(w, l_hi) + _dot(w, l_mid) + _dot(w, l_lo)
            gx = gx2[:, (hh % 2) * HG_DIM:(hh % 2 + 1) * HG_DIM]
            qs, kk, v = q_ref[rs, hs], k_ref[rs, hs], v_ref[rs, hs]
            g = gx[0:c]
            a = mask_ref[0] * _dot_nt(qs.astype(BF16), kk.astype(BF16))
            for l in range(levels):
                x = gx[(l + 1) * c:(l + 2) * c] if l < HG_TABLE_LEVELS else _minus_mid_rows(g, 2 << l)
                e = jnp.exp(-jnp.abs(x))
                qa = qs * e
                kb = kk * e
                a = a + mask_ref[l + 1] * _dot_nt(qa.astype(BF16), kb.astype(BF16))
            vb = v.astype(BF16)
            st_old = st[hh]
            o = _dot_nt((qs * jnp.exp(g)).astype(BF16), st_old.astype(BF16)) + _dot(a.astype(BF16), vb)
            g_end = g[c - 1:c]
            kd = kk * jnp.exp(g_end - g)
            st[hh] = st_old * jnp.exp(g_end) + _dot_tn(vb, kd.astype(BF16))
            o_ref[rs, hs] = (_rms(o, g_ref[...]) * gate_ref[rs, hs]).astype(o_ref.dtype)

    @pl.when(ci == pl.num_programs(2) - 1)
    def _():
        for hh in range(hps):
            sn_ref[hh] = st[hh].T


def _hgrn2(hmain, hk, hg_norm_g, s0, c, tc, hps):
    b, t, _ = hmain.shape
    w, mask, levels = _hgrn_tables(c)
    nl = levels + 1
    has_s0 = s0 is not None
    kern = functools.partial(_hgrn_kernel, c=c, levels=levels, nchunk=tc // c, hps=hps, has_s0=has_s0)
    nhg = HG_HEADS // hps
    col = lambda off: pl.BlockSpec((None, tc, hps * HG_DIM), lambda bi, h, ci: (bi, ci, off * nhg + h))
    state_spec = pl.BlockSpec((None, hps, HG_DIM, HG_DIM), lambda bi, h, ci: (bi, h, 0, 0))
    in_specs = [
        pl.BlockSpec(w.shape, lambda bi, h, ci: (0, 0)),
        pl.BlockSpec((nl, c, c), lambda bi, h, ci: (0, 0, 0)),
        pl.BlockSpec((1, HG_DIM), lambda bi, h, ci: (0, 0)),
        col(0), col(1), col(2), col(3), col(0),
    ]
    args = [w, mask, hg_norm_g.reshape(1, HG_DIM), hmain, hmain, hmain, hmain, hk]
    if has_s0:
        in_specs, args = [state_spec] + in_specs, [s0] + args
    return pl.pallas_call(
        kern,
        grid=(b, nhg, t // tc),
        in_specs=in_specs,
        out_specs=[col(0), state_spec],
        out_shape=[jax.ShapeDtypeStruct((b, t, HG_HEADS * HG_DIM), BF16),
                   jax.ShapeDtypeStruct((b, HG_HEADS, HG_DIM, HG_DIM), F32)],
        scratch_shapes=[pltpu.VMEM((hps, HG_DIM, HG_DIM), F32)],
        compiler_params=_cparams(("arbitrary", "arbitrary", "arbitrary")),
        name="hgrn2",
    )(*args)


def _out_kernel(a_ref, m_ref, w_ref, x_ref, ga_ref, o_ref):
    half = a_ref.shape[1]
    mix = _dot(a_ref[...], w_ref[:half, :]) + _dot(m_ref[...], w_ref[half:, :])
    o_ref[...] = x_ref[...] + ga_ref[...] * mix


def _out_proj(a2d, m2d, w_out, x2d, ga, tm, rows_per_mod):
    n, d = x2d.shape
    half = a2d.shape[1]
    r = ga.shape[1]
    tiles_per_mod = rows_per_mod // tm
    return pl.pallas_call(
        _out_kernel,
        grid=(n // tm,),
        in_specs=[
            pl.BlockSpec((tm, half), lambda i: (i, 0)),
            pl.BlockSpec((tm, half), lambda i: (i, 0)),
            pl.BlockSpec((2 * half, d), lambda i: (0, 0)),
            pl.BlockSpec((tm, d), lambda i: (i, 0)),
            pl.BlockSpec((None, r, d), lambda i: (i // tiles_per_mod, 0, 0)),
        ],
        out_specs=pl.BlockSpec((tm, d), lambda i: (i, 0)),
        out_shape=jax.ShapeDtypeStruct((n, d), F32),
        compiler_params=_cparams(("arbitrary",)),
        name="out_proj",
    )(a2d, m2d, w_out.astype(BF16), x2d, ga)


def _router_kernel(x_ref, n2_ref, sc_ref, sh_ref, wr_ref, br_ref, tril_ref, h_ref, idx_ref, rw_ref, cnt_ref, cnt_scr):
    h = _rms(x_ref[...], n2_ref[...]) * (1.0 + sc_ref[...]) + sh_ref[...]
    h_ref[...] = h
    h_hi = h.astype(BF16)
    h_lo = (h - h_hi.astype(F32)).astype(BF16)
    w = wr_ref[...]
    w_hi = w.astype(BF16)
    w_lo = (w - w_hi.astype(F32)).astype(BF16)
    parts = []
    for rs in (slice(0, h.shape[0] // 2), slice(h.shape[0] // 2, h.shape[0])):
        parts.append(_dot(h_hi[rs], w_hi) + _dot(h_hi[rs], w_lo) + _dot(h_lo[rs], w_hi))
    lg = jnp.concatenate(parts, axis=0) + br_ref[...]

    lane = lax.broadcasted_iota(jnp.int32, lg.shape, 1)
    neg = -jnp.inf
    big = jnp.int32(LANES)

    def top(valid):
        m = jnp.max(jnp.where(valid, lg, neg), axis=-1, keepdims=True)
        idx = jnp.min(jnp.where(valid & (lg == m), lane, big), axis=-1, keepdims=True)
        return m, idx

    is_group = lane < N_GROUPS
    gmax, gidx = top(is_group)
    pg_top = 1.0 / jnp.sum(jnp.where(is_group, jnp.exp(lg - gmax), 0.0), axis=-1, keepdims=True)
    lo = ROUTE_LANE0 + EXPERTS_PER_GROUP * gidx
    in_group = (lane >= lo) & (lane < lo + EXPERTS_PER_GROUP)
    l0, e0 = top(in_group)
    l1, e1 = top(in_group & (lane != e0))
    r = jnp.exp(l1 - l0)
    w0 = pg_top / (1.0 + r)
    w1 = w0 * r

    @pl.when(pl.program_id(0) == 0)
    def _():
        cnt_scr[...] = jnp.zeros(cnt_scr.shape, F32)

    is0, is1 = lane == e0, lane == e1
    onehot = jnp.where(is0 | is1, 1.0, 0.0)
    before = _dot(tril_ref[...], onehot.astype(BF16)) + cnt_scr[...]
    rank0 = jnp.sum(jnp.where(is0, before, 0.0), axis=-1, keepdims=True).astype(jnp.int32)
    rank1 = jnp.sum(jnp.where(is1, before, 0.0), axis=-1, keepdims=True).astype(jnp.int32)
    cnt_scr[...] = cnt_scr[...] + jnp.sum(onehot, axis=0, keepdims=True)
    idx_ref[...] = jnp.where(lane == 0, e0 - ROUTE_LANE0, jnp.where(lane == 1, e1 - ROUTE_LANE0,
                             jnp.where(lane == 2, rank0, jnp.where(lane == 3, rank1, 0))))
    rw_ref[...] = jnp.where(lane == 0, w0, jnp.where(lane == 1, w1, 0.0))
    cnt_ref[...] = cnt_scr[...].astype(jnp.int32)


def _router(x1, n2, sc, sh, w_route, b_route, tm, rows_per_mod):
    n, d = x1.shape
    r = sc.shape[1]
    tiles_per_mod = rows_per_mod // tm
    mod_spec = pl.BlockSpec((None, r, d), lambda i: (i // tiles_per_mod, 0, 0))
    row_spec = lambda w: pl.BlockSpec((tm, w), lambda i: (i, 0))
    fixed = lambda a, b: pl.BlockSpec((a, b), lambda i: (0, 0))
    return pl.pallas_call(
        _router_kernel,
        grid=(n // tm,),
        in_specs=[row_spec(d), fixed(1, d), mod_spec, mod_spec, fixed(d, LANES), fixed(1, LANES), fixed(tm, tm)],
        out_specs=[row_spec(d), row_spec(LANES), row_spec(LANES), fixed(1, LANES)],
        out_shape=[jax.ShapeDtypeStruct((n, d), F32), jax.ShapeDtypeStruct((n, LANES), jnp.int32),
                   jax.ShapeDtypeStruct((n, LANES), F32), jax.ShapeDtypeStruct((1, LANES), jnp.int32)],
        scratch_shapes=[pltpu.VMEM((1, LANES), F32)],
        compiler_params=_cparams(("arbitrary",)),
        name="router",
    )(x1, n2.reshape(1, d), sc, sh, w_route, b_route,
      jnp.asarray(np.tril(np.ones((tm, tm), np.float32), -1), BF16))


def _dispatch_kernel(pos_ref, pad_ref, h_ref, hs_ref, xs_hbm, zbuf, sem, psem, *, tm, ne):
    i = pl.program_id(0)
    last = i == pl.num_programs(0) - 1

    def scatter(ref, first_token, count):
        def body(r, c):
            t = first_token + r
            src = ref.at[pl.ds(r, 1)]
            pltpu.make_async_copy(src, xs_hbm.at[pl.ds(pos_ref[2 * t], 1)], sem).start()
            pltpu.make_async_copy(src, xs_hbm.at[pl.ds(pos_ref[2 * t + 1], 1)], sem).start()
            return c

        lax.fori_loop(0, count, body, 0, unroll=8)

    def drain(ref):
        for _ in range(2):
            pltpu.make_async_copy(ref, xs_hbm.at[pl.ds(0, ref.shape[0])], sem).wait()

    scatter(h_ref, i * tm, tm)

    @pl.when(last)
    def _():
        scatter(hs_ref, pl.num_programs(0) * tm, hs_ref.shape[0])
        drain(hs_ref)
        zbuf[...] = jnp.zeros(zbuf.shape, F32)
        tile_rows = zbuf.shape[0]

        def pad_copy(e, r):
            return pltpu.make_async_copy(zbuf.at[pl.ds(0, 1)], xs_hbm.at[pl.ds(pad_ref[e] + r, 1)], psem)

        def tail_copy(t):
            return pltpu.make_async_copy(zbuf, xs_hbm.at[pl.ds(pl.multiple_of(t * tile_rows, tile_rows), tile_rows)], psem)

        def for_each_fill(fn):
            def per_expert(e, c):
                def per_row(r, cc):
                    fn(pad_copy(e, r))
                    return cc

                lax.fori_loop(0, pad_ref[ne + e], per_row, 0)
                return c

            def per_tile(t, c):
                fn(tail_copy(t))
                return c

            lax.fori_loop(0, ne, per_expert, 0)
            lax.fori_loop(pad_ref[2 * ne], xs_hbm.shape[0] // tile_rows, per_tile, 0)

        for_each_fill(lambda cp: cp.start())
        for_each_fill(lambda cp: cp.wait())

    drain(h_ref)


def _dispatch(pos, pad, h, h_small, n_rows, tile_rows, ne, tm=512):
    n, w = h.shape
    tm = min(tm, n)
    assert n % tm == 0 and n_rows % tile_rows == 0
    any_spec = pl.BlockSpec(memory_space=pl.ANY)
    return pl.pallas_call(
        functools.partial(_dispatch_kernel, tm=tm, ne=ne),
        grid_spec=pltpu.PrefetchScalarGridSpec(
            num_scalar_prefetch=2, grid=(n // tm,),
            in_specs=[pl.BlockSpec((tm, w), lambda i, p, q: (i, 0)),
                      pl.BlockSpec(h_small.shape, lambda i, p, q: (0, 0))],
            out_specs=any_spec,
            scratch_shapes=[pltpu.VMEM((tile_rows, w), F32), pltpu.SemaphoreType.DMA(()),
                            pltpu.SemaphoreType.DMA(())]),
        out_shape=jax.ShapeDtypeStruct((n_rows, w), F32),
        compiler_params=_cparams(("arbitrary",)),
        name="moe_dispatch",
    )(pos, pad, h, h_small)


def _expert_kernel(te_ref, nv_ref, x_ref, wg_ref, wu_ref, wd_ref, y_ref):
    del te_ref
    i = pl.program_id(0)

    @pl.when(i < nv_ref[0])
    def _():
        x = x_ref[...].astype(BF16)
        a = _dot(x, wg_ref[...].astype(BF16))
        u = _dot(x, wu_ref[...].astype(BF16))
        hid = a * _sigmoid(a) * u
        y_ref[...] = _dot(hid.astype(BF16), wd_ref[...].astype(BF16))

    @pl.when(i >= nv_ref[0])
    def _():
        y_ref[...] = jnp.zeros(y_ref.shape, F32)


def _experts(tile_expert, n_valid, xs, w_gate, w_up, w_down, tm):
    rows, half = xs.shape
    ne, d, f = w_gate.shape
    w_spec = lambda a, b: pl.BlockSpec((None, a, b), lambda i, te, nv: (te[i], 0, 0))
    return pl.pallas_call(
        _expert_kernel,
        grid_spec=pltpu.PrefetchScalarGridSpec(
            num_scalar_prefetch=2, grid=(rows // tm,),
            in_specs=[pl.BlockSpec((tm, half), lambda i, te, nv: (jnp.minimum(i, nv[0] - 1), 0)),
                      w_spec(d, f), w_spec(d, f), w_spec(f, d)],
            out_specs=pl.BlockSpec((tm, d), lambda i, te, nv: (i, 0))),
        out_shape=jax.ShapeDtypeStruct((rows, d), F32),
        compiler_params=_cparams(("arbitrary",)),
        name="moe_experts",
    )(tile_expert, n_valid, xs, w_gate, w_up, w_down)


def _combine_kernel(pos_ref, ys_hbm, x_ref, ga_ref, rw_ref, o_ref, buf, sem, *, tm):
    i = pl.program_id(0)
    n = pl.num_programs(0)

    def issue(tile, slot):
        def body(r, c):
            t = tile * tm + r
            pltpu.make_async_copy(ys_hbm.at[pl.ds(pos_ref[2 * t], 1)], buf.at[slot, pl.ds(r, 1)],
                                  sem.at[slot]).start()
            pltpu.make_async_copy(ys_hbm.at[pl.ds(pos_ref[2 * t + 1], 1)], buf.at[slot, pl.ds(tm + r, 1)],
                                  sem.at[slot]).start()
            return c

        lax.fori_loop(0, tm, body, 0, unroll=8)

    @pl.when(i == 0)
    def _():
        issue(0, 0)

    @pl.when(i + 1 < n)
    def _():
        issue(i + 1, (i + 1) % 2)

    slot = i % 2
    pltpu.make_async_copy(ys_hbm.at[pl.ds(0, 2 * tm)], buf.at[slot], sem.at[slot]).wait()
    w = rw_ref[...]
    y = w[:, 0:1] * buf[slot, 0:tm] + w[:, 1:2] * buf[slot, tm:2 * tm]
    o_ref[...] = x_ref[...] + ga_ref[...] * y


def _combine(pos, ys, x1, ga, rw, tm, rows_per_mod):
    n, d = x1.shape
    r = ga.shape[1]
    tiles_per_mod = rows_per_mod // tm
    return pl.pallas_call(
        functools.partial(_combine_kernel, tm=tm),
        grid_spec=pltpu.PrefetchScalarGridSpec(
            num_scalar_prefetch=1, grid=(n // tm,),
            in_specs=[pl.BlockSpec(memory_space=pl.ANY),
                      pl.BlockSpec((tm, d), lambda i, p: (i, 0)),
                      pl.BlockSpec((None, r, d), lambda i, p: (i // tiles_per_mod, 0, 0)),
                      pl.BlockSpec((tm, LANES), lambda i, p: (i, 0))],
            out_specs=pl.BlockSpec((tm, d), lambda i, p: (i, 0)),
            scratch_shapes=[pltpu.VMEM((2, 2 * tm, d), F32), pltpu.SemaphoreType.DMA((2,))]),
        out_shape=jax.ShapeDtypeStruct((n, d), F32),
        compiler_params=_cparams(("arbitrary",)),
        name="moe_combine",
    )(pos, ys, x1, ga, rw)


def _grouped_moe(big, small, w_gate, w_up, w_down, tm_e=256, tm_c=256):
    ne = w_gate.shape[0]
    n_all = big["h"].shape[0] + small["h"].shape[0]
    tm_e = min(tm_e, big["h"].shape[0])
    cnt_big = big["cnt"][0, ROUTE_LANE0:ROUTE_LANE0 + ne]
    counts = cnt_big + small["cnt"][0, ROUTE_LANE0:ROUTE_LANE0 + ne]
    tiles = (counts + tm_e - 1) // tm_e
    tile_end = jnp.cumsum(tiles)
    start_row = (tile_end - tiles) * tm_e
    n_tiles = (2 * n_all + tm_e - 1) // tm_e + ne
    tile_expert = jnp.minimum(jnp.sum(tile_end[None, :] <= jnp.arange(n_tiles)[:, None], axis=1), ne - 1)

    def positions(stream, first_rank):
        e, rank = stream["idx"][:, 0:2], stream["idx"][:, 2:4]
        return (jnp.take(start_row + first_rank, e, axis=0) + rank).reshape(-1).astype(jnp.int32)

    pos_big, pos_small = positions(big, 0), positions(small, cnt_big)
    pad = jnp.concatenate([start_row + counts, tiles * tm_e - counts, tile_end[ne - 1:]]).astype(jnp.int32)
    xs = _dispatch(jnp.concatenate([pos_big, pos_small]), pad, big["h"], small["h"], n_tiles * tm_e, tm_e, ne)
    ys = _experts(tile_expert.astype(jnp.int32), tile_end[ne - 1:].astype(jnp.int32), xs, w_gate, w_up, w_down, tm_e)
    return [_combine(pos, ys, s["x1"], s["ga"], s["rw"], min(tm_c, s["h"].shape[0]), s["rows_per_mod"])
            for s, pos in ((big, pos_big), (small, pos_small))]


def _pick_tile(n, pref):
    t = min(pref, n)
    while n % t:
        t //= 2
    return t


def _layer(x, mods, s0, attend, lw, lam_init, hg_chunk, hg_heads_per_step, per_row_mod):
    (n1, n2, w_in, gq, gk, subln, lam4, hlb, hg_g, w_out, w_route, b_route, w_gate, w_up, w_down) = lw
    b, t, d = x.shape
    n = b * t
    x2d = x.reshape(n, d)
    if per_row_mod:
        tm = n
        rows_per_mod = n
        mods = [jnp.repeat(m, t, axis=0).reshape(1, n, d) for m in mods]
    else:
        tm = _pick_tile(t, 512)
        rows_per_mod = t
        mods = [m.reshape(b, 1, d) for m in mods]
    sh1, sc1, ga1, sh2, sc2, ga2 = mods

    tm_in = tm if per_row_mod else _pick_tile(t, 1024)
    q, k, v, hmain, hk = _in_proj(x2d, n1, sc1, sh1, w_in, gq, gk, hlb, tm_in, rows_per_mod)
    a = attend(q.reshape(b, t, -1), k.reshape(b, t, -1), v.reshape(b, t, -1))
    tpad = -(-t // hg_chunk) * hg_chunk
    hm3, hk3 = hmain.reshape(b, t, -1), hk.reshape(b, t, -1)
    if tpad != t:
        hm3 = jnp.pad(hm3, ((0, 0), (0, tpad - t), (0, 0)))
        hk3 = jnp.pad(hk3, ((0, 0), (0, tpad - t), (0, 0)))
    m, s_new = _hgrn2(hm3, hk3, hg_g, s0, hg_chunk, _pick_tile(tpad, 2 * hg_chunk), hg_heads_per_step)
    m = m[:, :t]
    x1 = _out_proj(a.reshape(n, -1), m.reshape(n, -1), w_out, x2d, ga1, tm, rows_per_mod)
    hp, idx, rw, cnt = _router(x1, n2, sc2, sh2, w_route, b_route, tm, rows_per_mod)
    routed = dict(h=hp, idx=idx, rw=rw, cnt=cnt, x1=x1, ga=ga2, rows_per_mod=rows_per_mod)
    return routed, k, v, s_new


def kernel(x_prompt, x_sample, cache_k, cache_v, state_hgrn, page_table, c_prompt, c_sample, norm1_g, norm2_g, w_ada, b_ada, w_in, q_norm_g, k_norm_g, lambda_q1, lambda_k1, lambda_q2, lambda_k2, subln_g, hg_lower_bound, hg_norm_g, w_out, w_router_group, b_router_group, w_router_expert, b_router_expert, w_exp_gate, w_exp_up, w_exp_down):
    depth = norm1_g.shape[0]
    assert depth == 1, "single-layer trunk"
    l = 0
    lam_init = 0.8 - 0.6 * math.exp(-0.3 * l)
    bp, tp, d = x_prompt.shape
    bs, ts, _ = x_sample.shape

    c_all = jnp.concatenate([c_prompt, c_sample], axis=0)
    rpad = -(-c_all.shape[0] // 8) * 8
    c_all = jnp.pad(c_all, ((0, rpad - c_all.shape[0]), (0, 0)))
    mod = _adaln(c_all, w_ada[l], b_ada[l])
    mods_p = [mod[:bp, i * d:(i + 1) * d] for i in range(6)]
    mods_s = [mod[bp:bp + bs, i * d:(i + 1) * d] for i in range(6)]

    w_re = jnp.transpose(w_router_expert[l], (1, 0, 2)).reshape(d, N_EXPERTS)
    w_route = jnp.pad(jnp.concatenate([w_router_group[l], w_re], axis=1), ((0, 0), (0, LANES - N_GROUPS - N_EXPERTS)))
    b_route = jnp.pad(jnp.concatenate([b_router_group[l], b_router_expert[l].reshape(-1)]),
                      (0, LANES - N_GROUPS - N_EXPERTS)).reshape(1, LANES)
    lam4 = jnp.stack([lambda_q1[l], lambda_k1[l], lambda_q2[l], lambda_k2[l]], axis=0)

    lw = (norm1_g[l], norm2_g[l], w_in[l], q_norm_g[l], k_norm_g[l], subln_g[l], lam4, hg_lower_bound,
          hg_norm_g[l], w_out[l], w_route, b_route, w_exp_gate[l], w_exp_up[l], w_exp_down[l])

    def attend_prompt(q, k, v):
        return _prompt_attention(q, k, v, lam4, subln_g[l], lam_init)

    def attend_sample(q, k, v):
        pad = lambda a: jnp.pad(a, ((0, 0), (0, -ts % 8), (0, 0)))
        o = _sample_attention(pad(q), pad(k), pad(v), ts, cache_k[l], cache_v[l], page_table, lam4,
                              subln_g[l], lam_init)
        return o[:, :ts]

    routed_p, kp, vp, sp = _layer(x_prompt, mods_p, None, attend_prompt, lw, lam_init, 128, 4, False)
    routed_s, kn, vn, sn = _layer(x_sample, mods_s, state_hgrn[l], attend_sample, lw, lam_init, 8, HG_HEADS, True)
    yp, ys = _grouped_moe(routed_p, routed_s, w_exp_gate[l], w_exp_up[l], w_exp_down[l])

    kv_shape = lambda b, t: (1, b, t, ATTN_KV_HEADS, 2 * QK_DIM)
    return (yp.reshape(x_prompt.shape), ys.reshape(x_sample.shape),
            kp.reshape(kv_shape(bp, tp)), vp.reshape(kv_shape(bp, tp)),
            kn.reshape(kv_shape(bs, ts)), vn.reshape(kv_shape(bs, ts)),
            sp[None], sn[None])
```

```python
import functools
import math

import jax
import jax.numpy as jnp
import numpy as np
from jax import lax
from jax.experimental import pallas as pl
from jax.experimental.pallas import tpu as pltpu

F32 = jnp.float32
BF16 = jnp.bfloat16

QK_DIM = 64
V_DIM = 128
ATTN_KV_HEADS = 4
ATTN_GROUP = 2
ATTN_HEADS = ATTN_KV_HEADS * ATTN_GROUP
HG_HEADS = 8
HG_DIM = 128
N_GROUPS = 4
EXPERTS_PER_GROUP = 4
N_EXPERTS = N_GROUPS * EXPERTS_PER_GROUP
NORM_EPS = 1e-6
ALIBI_MAX_BIAS = 8.0
LANES = 128
MXU_DIM = 256
VMEM_LIMIT = 60 * 1024 * 1024
ROUTE_LANE0 = N_GROUPS
HG_TABLE_LEVELS = 2


def _cparams(sem):
    return pltpu.CompilerParams(dimension_semantics=sem, vmem_limit_bytes=VMEM_LIMIT)


def _dot(a, b):
    return jnp.dot(a, b, preferred_element_type=F32)


def _dot_nt(a, b):
    return lax.dot_general(a, b, (((1,), (1,)), ((), ())), preferred_element_type=F32)


def _dot_tn(a, b):
    return lax.dot_general(a, b, (((0,), (0,)), ((), ())), preferred_element_type=F32)


def _split3(x):
    hi = x.astype(BF16)
    r = x - hi.astype(F32)
    mid = r.astype(BF16)
    lo = (r - mid.astype(F32)).astype(BF16)
    return hi, mid, lo


def _sigmoid(x):
    return 1.0 / (1.0 + jnp.exp(-x))


def _rms(x, g):
    return x * lax.rsqrt(jnp.mean(x * x, axis=-1, keepdims=True) + NORM_EPS) * g


def _ada_kernel(c_ref, w_ref, b_ref, o_ref):
    c = c_ref[...]
    s = c * _sigmoid(c)
    o_ref[...] = _dot(s.astype(BF16), w_ref[...].astype(BF16)) + b_ref[...]


def _adaln(c_all, w_ada, b_ada, tn=1024):
    r, d = c_all.shape
    n = w_ada.shape[1]
    return pl.pallas_call(
        _ada_kernel,
        grid=(n // tn,),
        in_specs=[
            pl.BlockSpec((r, d), lambda j: (0, 0)),
            pl.BlockSpec((d, tn), lambda j: (0, j)),
            pl.BlockSpec((1, tn), lambda j: (0, j)),
        ],
        out_specs=pl.BlockSpec((r, tn), lambda j: (0, j)),
        out_shape=jax.ShapeDtypeStruct((r, n), F32),
        compiler_params=_cparams(("arbitrary",)),
        name="adaln",
    )(c_all, w_ada, b_ada.reshape(1, n))


IN_TN = 512


def _group_sumsq(p, bd):
    x2 = p * p
    hi = x2.astype(BF16)
    lo = (x2 - hi.astype(F32)).astype(BF16)
    outs = []
    for c in range(p.shape[1] // MXU_DIM):
        sl = slice(c * MXU_DIM, (c + 1) * MXU_DIM)
        outs.append(_dot(hi[:, sl], bd) + _dot(lo[:, sl], bd))
    return jnp.concatenate(outs, axis=1)


def _in_epilogue(c, p, cs, gq_ref, gk_ref, lb_ref, bd_ref, q_ref, k_ref, v_ref, hm_ref, hk_ref):
    def qk_norm(g):
        ms = _group_sumsq(p, bd_ref[...]) * (1.0 / QK_DIM)
        return p * lax.rsqrt(ms + NORM_EPS) * g

    if c < 2:
        q_ref[:, cs] = (qk_norm(gq_ref[:, cs]) * (QK_DIM ** -0.5)).astype(BF16)
    elif c == 2:
        k_ref[:, cs] = qk_norm(gk_ref[:, cs])
    elif c == 3:
        v_ref[:, cs] = p
    elif c in (4, 5, 10, 11):
        hm_ref[:, cs] = p * _sigmoid(p)
    elif c in (6, 7):
        a = lb_ref[:, cs]
        e = jnp.exp(a - jnp.max(a, axis=0, keepdims=True))
        lb = e[0:1] / jnp.sum(e, axis=0, keepdims=True)
        sg = _sigmoid(p)
        hm_ref[:, cs] = jnp.log(lb + (1.0 - lb) * sg)
        hk_ref[:, cs] = (1.0 - lb) * (1.0 - sg)
    else:
        hm_ref[:, cs] = p


def _in_kernel(x_ref, n1_ref, sc_ref, sh_ref, w_ref, gq_ref, gk_ref, lb_ref, bd_ref,
               q_ref, k_ref, v_ref, hm_ref, hk_ref, h_scr, *, nj):
    j = pl.program_id(1)

    @pl.when(j == 0)
    def _():
        y = _rms(x_ref[...], n1_ref[...])
        h_scr[...] = (y * (1.0 + sc_ref[...]) + sh_ref[...]).astype(BF16)

    outs = (gq_ref, gk_ref, lb_ref, bd_ref, q_ref, k_ref, v_ref, hm_ref, hk_ref)
    for cols in ((0, 1), (2,), (3,), (4, 5, 10, 11), (6, 7), (8, 9)):
        cond = functools.reduce(lambda a, b: a | b, [j == c for c in cols])
        width = IN_TN if cols[0] <= 2 else MXU_DIM

        @pl.when(cond)
        def _(c=cols[0], width=width):
            for u in range(IN_TN // width):
                cs = slice(u * width, (u + 1) * width)
                p = _dot(h_scr[...], w_ref[:, cs].astype(BF16))
                _in_epilogue(c, p, cs, *outs)


def _in_proj(x2d, n1, sc, sh, w_in, gq, gk, hlb, tm, rows_per_mod):
    n, d = x2d.shape
    cols = w_in.shape[1]
    nj = cols // IN_TN
    assert nj == 12 and n % tm == 0
    r = sc.shape[1]
    tiles_per_mod = rows_per_mod // tm
    bd = jnp.asarray(np.kron(np.eye(MXU_DIM // QK_DIM), np.ones((QK_DIM, QK_DIM))), BF16)
    clamp = lambda j, lo: jnp.clip(j - lo, 0, 1)
    mod_spec = pl.BlockSpec((None, r, d), lambda i, j: (i // tiles_per_mod, 0, 0))
    return pl.pallas_call(
        functools.partial(_in_kernel, nj=nj),
        grid=(n // tm, nj),
        in_specs=[
            pl.BlockSpec((tm, d), lambda i, j: (i, 0)),
            pl.BlockSpec((1, d), lambda i, j: (0, 0)),
            mod_spec, mod_spec,
            pl.BlockSpec((d, IN_TN), lambda i, j: (0, j)),
            pl.BlockSpec((1, IN_TN), lambda i, j: (0, 0)),
            pl.BlockSpec((1, IN_TN), lambda i, j: (0, 0)),
            pl.BlockSpec((hlb.shape[0], IN_TN), lambda i, j: (0, clamp(j, 6))),
            pl.BlockSpec((MXU_DIM, MXU_DIM), lambda i, j: (0, 0)),
        ],
        out_specs=[
            pl.BlockSpec((tm, IN_TN), lambda i, j: (i, clamp(j, 0))),
            pl.BlockSpec((tm, IN_TN), lambda i, j: (i, 0)),
            pl.BlockSpec((tm, IN_TN), lambda i, j: (i, 0)),
            pl.BlockSpec((tm, IN_TN), lambda i, j: (i, jnp.clip(j - 4, 0, 7))),
            pl.BlockSpec((tm, IN_TN), lambda i, j: (i, clamp(j, 6))),
        ],
        out_shape=[
            jax.ShapeDtypeStruct((n, 2 * IN_TN), BF16),
            jax.ShapeDtypeStruct((n, IN_TN), F32),
            jax.ShapeDtypeStruct((n, IN_TN), F32),
            jax.ShapeDtypeStruct((n, 8 * IN_TN), F32),
            jax.ShapeDtypeStruct((n, 2 * IN_TN), F32),
        ],
        scratch_shapes=[pltpu.VMEM((tm, d), BF16)],
        compiler_params=_cparams(("arbitrary", "arbitrary")),
        name="in_proj",
    )(x2d, n1.reshape(1, d), sc, sh, w_in,
      jnp.tile(gq.reshape(1, QK_DIM), (1, IN_TN // QK_DIM)),
      jnp.tile(gk.reshape(1, QK_DIM), (1, IN_TN // QK_DIM)), hlb, bd)


def _alibi_slopes_np():
    h = np.arange(1, ATTN_HEADS + 1, dtype=np.float64)
    return np.exp2(-ALIBI_MAX_BIAS * h / ATTN_HEADS).reshape(ATTN_KV_HEADS, ATTN_GROUP).astype(np.float32)


def _alibi_slopes():
    out = np.zeros((ATTN_KV_HEADS, 1, LANES), np.float32)
    out[:, 0, :ATTN_GROUP] = _alibi_slopes_np()
    return jnp.asarray(out)


def _lambda(lam_ref, lam_init):
    l = lam_ref[...]
    s1 = jnp.sum(l[0:1] * l[1:2], axis=-1, keepdims=True)
    s2 = jnp.sum(l[2:3] * l[3:4], axis=-1, keepdims=True)
    return jnp.exp(s1) - jnp.exp(s2) + lam_init


def _diff_finish(o_all, t, lam, subln, lam_init, o_ref):
    for g in range(ATTN_GROUP):
        o = o_all[(2 * g) * t:(2 * g + 1) * t] - lam * o_all[(2 * g + 1) * t:(2 * g + 2) * t]
        y = _rms(o, subln) * (1.0 - lam_init)
        o_ref[:, g * LANES:(g + 1) * LANES] = y.astype(o_ref.dtype)


def _softmax_step_t(s, m_old, l_old):
    m_new = jnp.maximum(m_old, jnp.max(s, axis=0, keepdims=True))
    alpha = jnp.exp(m_old - m_new)
    p = jnp.exp(s - m_new)
    return m_new, alpha, alpha * l_old + jnp.sum(p, axis=0, keepdims=True), p.astype(BF16)


def _pattn_kernel(sl_ref, lam_ref, subln_ref, q_ref, k_ref, v_ref, o_ref, kb, vt, qs, s_a, s_b, m_scr, l_scr, acc,
                  *, tq, nblk, lam_init):
    qi = pl.program_id(2)

    @pl.when(qi == 0)
    def _():
        t = k_ref.shape[0]
        pos = lax.broadcasted_iota(jnp.int32, (t, LANES), 0)
        ln = lax.broadcasted_iota(jnp.int32, (t, LANES), 1)
        aug = jnp.where(ln == 0, pos & ~(LANES - 1), jnp.where(ln == 1, pos & (LANES - 1), 0))
        kb[:, :LANES] = k_ref[...].astype(BF16)
        kb[:, LANES:] = aug.astype(F32).astype(BF16)
        for jb in range(t // tq):
            vt[jb] = v_ref[jb * tq:(jb + 1) * tq, :].T.astype(BF16)

    rows = 4 * tq
    half = 2 * tq
    q = q_ref[...]
    lane = lax.broadcasted_iota(jnp.int32, (tq, LANES), 1)
    zero = jnp.zeros((tq, LANES), BF16)
    for g in range(ATTN_GROUP):
        qg = q[:, g * LANES:(g + 1) * LANES]
        aug = jnp.where(lane < 2, sl_ref[0:1, g:g + 1], 0.0).astype(BF16)
        qs[(2 * g) * tq:(2 * g + 1) * tq, :] = jnp.concatenate([jnp.where(lane < QK_DIM, qg, zero), aug], axis=1)
        qs[(2 * g + 1) * tq:(2 * g + 2) * tq, :] = jnp.concatenate([jnp.where(lane >= QK_DIM, qg, zero), aug], axis=1)
    t_q = lax.broadcasted_iota(jnp.int32, (1, tq), 1)
    m_scr[...] = jnp.full(m_scr.shape, -jnp.inf, F32)
    l_scr[...] = jnp.zeros(l_scr.shape, F32)
    acc[...] = jnp.zeros(acc.shape, F32)

    s_bufs = (s_a, s_b)
    groups = [slice(u * tq, (u + 1) * tq) for u in range(rows // tq)]

    def logits(jb, cs):
        s_bufs[jb % 2][:, cs] = _dot_nt(kb[jb * tq:(jb + 1) * tq, :], qs[cs, :])

    def attend(jb, cs, masked):
        s = s_bufs[jb % 2][:, cs]
        if masked:
            key = lax.broadcasted_iota(jnp.int32, (tq, 1), 0)
            s = jnp.where(key <= t_q, s, -jnp.inf)
        m_new, alpha, l_new, p = _softmax_step_t(s, m_scr[:, cs], l_scr[:, cs])
        acc[:, cs] = alpha * acc[:, cs] + _dot(vt[jb], p)
        m_scr[:, cs] = m_new
        l_scr[:, cs] = l_new

    for cs in groups:
        logits(0, cs)
    for jb in range(nblk):
        if jb + 1 < nblk:
            @pl.when(jb < qi)
            def _(jb=jb):
                for cs in groups:
                    logits(jb + 1, cs)
                    attend(jb, cs, False)

        @pl.when(jb == qi)
        def _(jb=jb):
            for cs in groups:
                attend(jb, cs, True)

    o_t = acc[...] * (1.0 / l_scr[...])
    lam = _lambda(lam_ref, lam_init)
    for g in range(ATTN_GROUP):
        d = o_t[:, (2 * g) * tq:(2 * g + 1) * tq] - lam * o_t[:, (2 * g + 1) * tq:(2 * g + 2) * tq]
        y = _rms(d.T, subln_ref[...]) * (1.0 - lam_init)
        o_ref[:, g * LANES:(g + 1) * LANES] = y.astype(o_ref.dtype)


def _prompt_attention(q, k, v, lam4, subln, lam_init, tq=512):
    b, t, _ = q.shape
    tq = min(tq, t)
    assert t % tq == 0 and t < LANES * 256
    rows = 4 * tq
    kern = functools.partial(_pattn_kernel, tq=tq, nblk=t // tq, lam_init=lam_init)
    return pl.pallas_call(
        kern,
        grid=(b, ATTN_KV_HEADS, t // tq),
        in_specs=[
            pl.BlockSpec((None, 1, LANES), lambda bi, h, qi: (h, 0, 0)),
            pl.BlockSpec((4, QK_DIM), lambda bi, h, qi: (0, 0)),
            pl.BlockSpec((1, V_DIM), lambda bi, h, qi: (0, 0)),
            pl.BlockSpec((None, tq, 2 * LANES), lambda bi, h, qi: (bi, qi, h)),
            pl.BlockSpec((None, t, LANES), lambda bi, h, qi: (bi, 0, h)),
            pl.BlockSpec((None, t, LANES), lambda bi, h, qi: (bi, 0, h)),
        ],
        out_specs=pl.BlockSpec((None, tq, 2 * LANES), lambda bi, h, qi: (bi, qi, h)),
        out_shape=jax.ShapeDtypeStruct((b, t, ATTN_HEADS * V_DIM), BF16),
        scratch_shapes=[
            pltpu.VMEM((t, 2 * LANES), BF16), pltpu.VMEM((t // tq, V_DIM, tq), BF16),
            pltpu.VMEM((rows, 2 * LANES), BF16), pltpu.VMEM((tq, rows), F32), pltpu.VMEM((tq, rows), F32),
            pltpu.VMEM((1, rows), F32), pltpu.VMEM((1, rows), F32), pltpu.VMEM((V_DIM, rows), F32),
        ],
        compiler_params=_cparams(("arbitrary", "arbitrary", "arbitrary")),
        name="prompt_attention",
    )(_alibi_slopes(), lam4, subln.reshape(1, V_DIM), q, k, v)


def _to_col(row):
    n = row.shape[1]
    eye = lax.broadcasted_iota(jnp.int32, (n, n), 0) == lax.broadcasted_iota(jnp.int32, (n, n), 1)
    return jnp.sum(jnp.where(eye, row, 0.0), axis=1, keepdims=True)


def _sattn_kernel(pt_ref, srow_ref, lam_ref, subln_ref, q_ref, kn_ref, vn_ref, *rest,
                  npg, nj, page, t, t_real, lam_init):
    del pt_ref
    kp_refs, vp_refs = rest[:npg], rest[npg:2 * npg]
    o_ref = rest[2 * npg]
    kc, vc, qf_scr, qbd_scr, m_scr, l_scr, acc = rest[2 * npg + 1:]
    j = pl.program_id(1)
    ntok = npg * page
    past = nj * ntok
    nq = qbd_scr.shape[0]
    width = ATTN_KV_HEADS * LANES
    srow = srow_ref[...]

    @pl.when(j == 0)
    def _():
        m_scr[...] = jnp.full(m_scr.shape, -jnp.inf, F32)
        l_scr[...] = jnp.zeros(l_scr.shape, F32)
        acc[...] = jnp.zeros(acc.shape, F32)
        qf = q_ref[...].astype(F32)
        lane = lax.broadcasted_iota(jnp.int32, (t, LANES), 1)
        qf_scr[...] = jnp.zeros(qf_scr.shape, F32)
        for hg in range(ATTN_HEADS):
            h = hg // ATTN_GROUP
            qh = qf[:, hg * LANES:(hg + 1) * LANES]
            qf_scr[(2 * hg) * t:(2 * hg + 1) * t, h * LANES:(h + 1) * LANES] = jnp.where(lane < QK_DIM, qh, 0.0)
            qf_scr[(2 * hg + 1) * t:(2 * hg + 2) * t, h * LANES:(h + 1) * LANES] = jnp.where(lane >= QK_DIM, qh, 0.0)
        qbd_scr[...] = qf_scr[...].astype(BF16)

    qbd = qbd_scr[...]

    for p in range(npg):
        kc[p * page:(p + 1) * page, :] = kp_refs[p][...].reshape(page, width).astype(BF16)
    kpos = lax.broadcasted_iota(jnp.int32, (ntok, nq), 0) + j * ntok
    hw = ntok // 2
    s = jnp.concatenate([_dot_nt(kc[:hw, :], qbd), _dot_nt(kc[hw:, :], qbd)], axis=0)
    s = s + kpos.astype(F32) * srow
    for p in range(npg):
        vc[p * page:(p + 1) * page, :] = vp_refs[p][...].reshape(page, width).astype(BF16)
    m_new, alpha, l_new, p = _softmax_step_t(s, m_scr[...], l_scr[...])
    pv = _dot_tn(p[:hw], vc[:hw, :]) + _dot_tn(p[hw:], vc[hw:, :])
    acc[...] = _to_col(alpha) * acc[...] + pv
    m_scr[...] = m_new
    l_scr[...] = l_new

    @pl.when(j == nj - 1)
    def _():
        t_q = lax.broadcasted_iota(jnp.int32, (1, nq), 1) % t
        c = lax.broadcasted_iota(jnp.int32, (t, 1), 0)
        sn = _dot_nt(kn_ref[...].astype(BF16), qbd) + (past + c).astype(F32) * srow
        sn = jnp.where((c <= t_q) & (c < t_real), sn, -jnp.inf)
        m_fin, alpha_n, l_fin, _ = _softmax_step_t(sn, m_scr[...], l_scr[...])
        pn = jnp.exp(sn - m_fin)
        o = _to_col(alpha_n) * acc[...]
        vn = vn_ref[...]
        for u in range(t_real):
            o = o + _to_col(pn[u:u + 1]) * vn[u:u + 1]
        o = o * _to_col(1.0 / l_fin)
        lam = _lambda(lam_ref, lam_init)
        rq = nq // ATTN_KV_HEADS
        for h in range(ATTN_KV_HEADS):
            o_h = o[h * rq:(h + 1) * rq, h * LANES:(h + 1) * LANES]
            _diff_finish(o_h, t, lam, subln_ref[...], lam_init, o_ref.at[:, 2 * h * LANES:(2 * h + 2) * LANES])


def _sample_attention(q, k_new, v_new, t_real, cache_k, cache_v, page_table, lam4, subln, lam_init, npg=32):
    b, t, _ = q.shape
    tpad = t
    page = cache_k.shape[1]
    n_pages = page_table.shape[1]
    npg = min(npg, n_pages)
    assert n_pages % npg == 0
    ntok = npg * page
    width = ATTN_KV_HEADS * LANES
    nq = ATTN_KV_HEADS * ATTN_GROUP * 2 * t
    nj = n_pages // npg
    kern = functools.partial(_sattn_kernel, npg=npg, nj=nj, page=page, t=t, t_real=t_real, lam_init=lam_init)
    slope_row = jnp.asarray(np.repeat(_alibi_slopes_np().reshape(-1), 2 * t)[None, :])

    def page_spec(p):
        return pl.BlockSpec((None, page, ATTN_KV_HEADS, LANES),
                            lambda bi, j, pt: (pt[bi * n_pages + j * npg + p], 0, 0, 0))

    grid_spec = pltpu.PrefetchScalarGridSpec(
        num_scalar_prefetch=1,
        grid=(b, nj),
        in_specs=[
            pl.BlockSpec((1, nq), lambda bi, j, pt: (0, 0)),
            pl.BlockSpec((4, QK_DIM), lambda bi, j, pt: (0, 0)),
            pl.BlockSpec((1, V_DIM), lambda bi, j, pt: (0, 0)),
            pl.BlockSpec((None, t, ATTN_HEADS * V_DIM), lambda bi, j, pt: (bi, 0, 0)),
            pl.BlockSpec((None, tpad, width), lambda bi, j, pt: (bi, 0, 0)),
            pl.BlockSpec((None, tpad, width), lambda bi, j, pt: (bi, 0, 0)),
        ] + [page_spec(p) for p in range(npg)] * 2,
        out_specs=pl.BlockSpec((None, t, ATTN_HEADS * V_DIM), lambda bi, j, pt: (bi, 0, 0)),
        scratch_shapes=[
            pltpu.VMEM((ntok, width), BF16), pltpu.VMEM((ntok, width), BF16),
            pltpu.VMEM((nq, width), F32), pltpu.VMEM((nq, width), BF16),
            pltpu.VMEM((1, nq), F32), pltpu.VMEM((1, nq), F32), pltpu.VMEM((nq, width), F32),
        ],
    )
    return pl.pallas_call(
        kern,
        grid_spec=grid_spec,
        out_shape=jax.ShapeDtypeStruct((b, t, ATTN_HEADS * V_DIM), BF16),
        compiler_params=_cparams(("arbitrary", "arbitrary")),
        name="sample_attention",
    )(page_table.reshape(-1), slope_row, lam4, subln.reshape(1, V_DIM), q, k_new, v_new,
      *([cache_k] * npg), *([cache_v] * npg))


def _hgrn_tables(c):
    levels = int(math.log2(c))
    tril = np.tril(np.ones((c, c), np.float32))
    w = [tril]
    mask = [np.eye(c, dtype=np.float32)]
    idx = np.arange(c)
    for l in range(levels):
        bs = 2 << l
        mid = (idx // bs) * bs + bs // 2
        if l < HG_TABLE_LEVELS:
            w.append(tril - tril[mid])
        upper = (idx % bs) >= bs // 2
        same = (idx[:, None] // bs) == (idx[None, :] // bs)
        mask.append((same & upper[:, None] & ~upper[None, :]).astype(np.float32))
    return jnp.asarray(np.concatenate(w, 0), BF16), jnp.asarray(np.stack(mask, 0)), levels


def _minus_mid_rows(g, bs):
    c, n = g.shape
    g3 = g.reshape(c // bs, bs, n)
    return (g3 - g3[:, bs // 2:bs // 2 + 1, :]).reshape(c, n)


def _hgrn_kernel(*refs, c, levels, nchunk, hps, has_s0):
    if has_s0:
        s0_ref, refs = refs[0], refs[1:]
    (w_ref, mask_ref, g_ref, q_ref, lf_ref, v_ref, gate_ref, k_ref, o_ref, sn_ref, st) = refs
    ci = pl.program_id(2)

    @pl.when(ci == 0)
    def _():
        for hh in range(hps):
            st[hh] = s0_ref[hh].T if has_s0 else jnp.zeros((HG_DIM, HG_DIM), F32)

    w = w_ref[...]
    pair = 2 * HG_DIM
    heads = range(hps)
    hsl = [slice(hh * HG_DIM, (hh + 1) * HG_DIM) for hh in heads]
    for n in range(nchunk):
        rs = slice(n * c, (n + 1) * c)
        gx = []
        for hh in range(0, hps, 2):
            l_hi, l_mid, l_lo = _split3(lf_ref[rs, hh * HG_DIM:hh * HG_DIM + pair])
            gx2 = _dot(w, l_hi) + _dot(w, l_mid) + _dot(w, l_lo)
            gx += [gx2[:, :HG_DIM], gx2[:, HG_DIM:]]
        qs = [q_ref[rs, hsl[hh]] for hh in heads]
        kk = [k_ref[rs, hsl[hh]] for hh in heads]
        g = [gx[hh][0:c] for hh in heads]
        a = [mask_ref[0] * _dot_nt(qs[hh].astype(BF16), kk[hh].astype(BF16)) for hh in heads]
        for l in range(levels):
            for hh in heads:
                x = gx[hh][(l + 1) * c:(l + 2) * c] if l < HG_TABLE_LEVELS else _minus_mid_rows(g[hh], 2 << l)
                e = jnp.exp(-jnp.abs(x))
                qa = (qs[hh] * e).astype(BF16)
                kb = (kk[hh] * e).astype(BF16)
                a[hh] = a[hh] + mask_ref[l + 1] * _dot_nt(qa, kb)
        vb = [v_ref[rs, hsl[hh]].astype(BF16) for hh in heads]
        st_old = [st[hh] for hh in heads]
        o = [_dot_nt((qs[hh] * jnp.exp(g[hh])).astype(BF16), st_old[hh].astype(BF16))
             + _dot(a[hh].astype(BF16), vb[hh]) for hh in heads]
        for hh in heads:
            g_end = g[hh][c - 1:c]
            kd = kk[hh] * jnp.exp(g_end - g[hh])
            st[hh] = st_old[hh] * jnp.exp(g_end) + _dot_tn(vb[hh], kd.astype(BF16))
        for hh in heads:
            o_ref[rs, hsl[hh]] = (_rms(o[hh], g_ref[...]) * gate_ref[rs, hsl[hh]]).astype(o_ref.dtype)

    @pl.when(ci == pl.num_programs(2) - 1)
    def _():
        for hh in range(hps):
            sn_ref[hh] = st[hh].T


def _hgrn2(hmain, hk, hg_norm_g, s0, c, tc, hps):
    b, t, _ = hmain.shape
    w, mask, levels = _hgrn_tables(c)
    nl = levels + 1
    has_s0 = s0 is not None
    kern = functools.partial(_hgrn_kernel, c=c, levels=levels, nchunk=tc // c, hps=hps, has_s0=has_s0)
    nhg = HG_HEADS // hps
    col = lambda off: pl.BlockSpec((None, tc, hps * HG_DIM), lambda bi, h, ci: (bi, ci, off * nhg + h))
    state_spec = pl.BlockSpec((None, hps, HG_DIM, HG_DIM), lambda bi, h, ci: (bi, h, 0, 0))
    in_specs = [
        pl.BlockSpec(w.shape, lambda bi, h, ci: (0, 0)),
        pl.BlockSpec((nl, c, c), lambda bi, h, ci: (0, 0, 0)),
        pl.BlockSpec((1, HG_DIM), lambda bi, h, ci: (0, 0)),
        col(0), col(1), col(2), col(3), col(0),
    ]
    args = [w, mask, hg_norm_g.reshape(1, HG_DIM), hmain, hmain, hmain, hmain, hk]
    if has_s0:
        in_specs, args = [state_spec] + in_specs, [s0] + args
    return pl.pallas_call(
        kern,
        grid=(b, nhg, t // tc),
        in_specs=in_specs,
        out_specs=[col(0), state_spec],
        out_shape=[jax.ShapeDtypeStruct((b, t, HG_HEADS * HG_DIM), BF16),
                   jax.ShapeDtypeStruct((b, HG_HEADS, HG_DIM, HG_DIM), F32)],
        scratch_shapes=[pltpu.VMEM((hps, HG_DIM, HG_DIM), F32)],
        compiler_params=_cparams(("arbitrary", "arbitrary", "arbitrary")),
        name="hgrn2",
    )(*args)


def _out_kernel(a_ref, m_ref, w_ref, x_ref, ga_ref, o_ref):
    half = a_ref.shape[1]
    mix = _dot(a_ref[...], w_ref[:half, :]) + _dot(m_ref[...], w_ref[half:, :])
    o_ref[...] = x_ref[...] + ga_ref[...] * mix


def _out_proj(a2d, m2d, w_out, x2d, ga, tm, rows_per_mod):
    n, d = x2d.shape
    half = a2d.shape[1]
    r = ga.shape[1]
    tiles_per_mod = rows_per_mod // tm
    return pl.pallas_call(
        _out_kernel,
        grid=(n // tm,),
        in_specs=[
            pl.BlockSpec((tm, half), lambda i: (i, 0)),
            pl.BlockSpec((tm, half), lambda i: (i, 0)),
            pl.BlockSpec((2 * half, d), lambda i: (0, 0)),
            pl.BlockSpec((tm, d), lambda i: (i, 0)),
            pl.BlockSpec((None, r, d), lambda i: (i // tiles_per_mod, 0, 0)),
        ],
        out_specs=pl.BlockSpec((tm, d), lambda i: (i, 0)),
        out_shape=jax.ShapeDtypeStruct((n, d), F32),
        compiler_params=_cparams(("arbitrary",)),
        name="out_proj",
    )(a2d, m2d, w_out.astype(BF16), x2d, ga)


def _router_kernel(x_ref, n2_ref, sc_ref, sh_ref, wr_ref, br_ref, tril_ref, h_ref, idx_ref, rw_ref, cnt_ref, cnt_scr):
    h = _rms(x_ref[...], n2_ref[...]) * (1.0 + sc_ref[...]) + sh_ref[...]
    h_ref[...] = h
    h_hi = h.astype(BF16)
    h_lo = (h - h_hi.astype(F32)).astype(BF16)
    w = wr_ref[...]
    w_hi = w.astype(BF16)
    w_lo = (w - w_hi.astype(F32)).astype(BF16)
    parts = []
    for rs in (slice(0, h.shape[0] // 2), slice(h.shape[0] // 2, h.shape[0])):
        parts.append(_dot(h_hi[rs], w_hi) + _dot(h_hi[rs], w_lo) + _dot(h_lo[rs], w_hi))
    lg = jnp.concatenate(parts, axis=0) + br_ref[...]

    lane = lax.broadcasted_iota(jnp.int32, lg.shape, 1)
    neg = -jnp.inf
    big = jnp.int32(LANES)

    def top(valid):
        m = jnp.max(jnp.where(valid, lg, neg), axis=-1, keepdims=True)
        idx = jnp.min(jnp.where(valid & (lg == m), lane, big), axis=-1, keepdims=True)
        return m, idx

    is_group = lane < N_GROUPS
    gmax, gidx = top(is_group)
    pg_top = 1.0 / jnp.sum(jnp.where(is_group, jnp.exp(lg - gmax), 0.0), axis=-1, keepdims=True)
    lo = ROUTE_LANE0 + EXPERTS_PER_GROUP * gidx
    in_group = (lane >= lo) & (lane < lo + EXPERTS_PER_GROUP)
    l0, e0 = top(in_group)
    l1, e1 = top(in_group & (lane != e0))
    r = jnp.exp(l1 - l0)
    w0 = pg_top / (1.0 + r)
    w1 = w0 * r

    @pl.when(pl.program_id(0) == 0)
    def _():
        cnt_scr[...] = jnp.zeros(cnt_scr.shape, F32)

    is0, is1 = lane == e0, lane == e1
    onehot = jnp.where(is0 | is1, 1.0, 0.0)
    before = _dot(tril_ref[...], onehot.astype(BF16)) + cnt_scr[...]
    rank0 = jnp.sum(jnp.where(is0, before, 0.0), axis=-1, keepdims=True).astype(jnp.int32)
    rank1 = jnp.sum(jnp.where(is1, before, 0.0), axis=-1, keepdims=True).astype(jnp.int32)
    cnt_scr[...] = cnt_scr[...] + jnp.sum(onehot, axis=0, keepdims=True)
    idx_ref[...] = jnp.where(lane == 0, e0 - ROUTE_LANE0, jnp.where(lane == 1, e1 - ROUTE_LANE0,
                             jnp.where(lane == 2, rank0, jnp.where(lane == 3, rank1, 0))))
    rw_ref[...] = jnp.where(lane == 0, w0, jnp.where(lane == 1, w1, 0.0))
    cnt_ref[...] = cnt_scr[...].astype(jnp.int32)


def _router(x1, n2, sc, sh, w_route, b_route, tm, rows_per_mod):
    n, d = x1.shape
    r = sc.shape[1]
    tiles_per_mod = rows_per_mod // tm
    mod_spec = pl.BlockSpec((None, r, d), lambda i: (i // tiles_per_mod, 0, 0))
    row_spec = lambda w: pl.BlockSpec((tm, w), lambda i: (i, 0))
    fixed = lambda a, b: pl.BlockSpec((a, b), lambda i: (0, 0))
    return pl.pallas_call(
        _router_kernel,
        grid=(n // tm,),
        in_specs=[row_spec(d), fixed(1, d), mod_spec, mod_spec, fixed(d, LANES), fixed(1, LANES), fixed(tm, tm)],
        out_specs=[row_spec(d), row_spec(LANES), row_spec(LANES), fixed(1, LANES)],
        out_shape=[jax.ShapeDtypeStruct((n, d), F32), jax.ShapeDtypeStruct((n, LANES), jnp.int32),
                   jax.ShapeDtypeStruct((n, LANES), F32), jax.ShapeDtypeStruct((1, LANES), jnp.int32)],
        scratch_shapes=[pltpu.VMEM((1, LANES), F32)],
        compiler_params=_cparams(("arbitrary",)),
        name="router",
    )(x1, n2.reshape(1, d), sc, sh, w_route, b_route,
      jnp.asarray(np.tril(np.ones((tm, tm), np.float32), -1), BF16))


def _dispatch_kernel(pos_ref, pad_ref, h_ref, hs_ref, xs_hbm, zbuf, sem, psem, *, tm, ne):
    i = pl.program_id(0)
    last = i == pl.num_programs(0) - 1

    def scatter(ref, first_token, count):
        def body(r, c):
            t = first_token + r
            src = ref.at[pl.ds(r, 1)]
            pltpu.make_async_copy(src, xs_hbm.at[pl.ds(pos_ref[2 * t], 1)], sem).start()
            pltpu.make_async_copy(src, xs_hbm.at[pl.ds(pos_ref[2 * t + 1], 1)], sem).start()
            return c

        lax.fori_loop(0, count, body, 0, unroll=8)

    def drain(ref):
        for _ in range(2):
            pltpu.make_async_copy(ref, xs_hbm.at[pl.ds(0, ref.shape[0])], sem).wait()

    scatter(h_ref, i * tm, tm)

    @pl.when(last)
    def _():
        scatter(hs_ref, pl.num_programs(0) * tm, hs_ref.shape[0])
        drain(hs_ref)
        zbuf[...] = jnp.zeros(zbuf.shape, F32)
        tile_rows = zbuf.shape[0]

        def pad_copy(e, r):
            return pltpu.make_async_copy(zbuf.at[pl.ds(0, 1)], xs_hbm.at[pl.ds(pad_ref[e] + r, 1)], psem)

        def tail_copy(t):
            return pltpu.make_async_copy(zbuf, xs_hbm.at[pl.ds(pl.multiple_of(t * tile_rows, tile_rows), tile_rows)], psem)

        def for_each_fill(fn):
            def per_expert(e, c):
                def per_row(r, cc):
                    fn(pad_copy(e, r))
                    return cc

                lax.fori_loop(0, pad_ref[ne + e], per_row, 0)
                return c

            def per_tile(t, c):
                fn(tail_copy(t))
                return c

            lax.fori_loop(0, ne, per_expert, 0)
            lax.fori_loop(pad_ref[2 * ne], xs_hbm.shape[0] // tile_rows, per_tile, 0)

        for_each_fill(lambda cp: cp.start())
        for_each_fill(lambda cp: cp.wait())

    drain(h_ref)


def _dispatch(pos, pad, h, h_small, n_rows, tile_rows, ne, tm=512):
    n, w = h.shape
    tm = min(tm, n)
    assert n % tm == 0 and n_rows % tile_rows == 0
    any_spec = pl.BlockSpec(memory_space=pl.ANY)
    return pl.pallas_call(
        functools.partial(_dispatch_kernel, tm=tm, ne=ne),
        grid_spec=pltpu.PrefetchScalarGridSpec(
            num_scalar_prefetch=2, grid=(n // tm,),
            in_specs=[pl.BlockSpec((tm, w), lambda i, p, q: (i, 0)),
                      pl.BlockSpec(h_small.shape, lambda i, p, q: (0, 0))],
            out_specs=any_spec,
            scratch_shapes=[pltpu.VMEM((tile_rows, w), F32), pltpu.SemaphoreType.DMA(()),
                            pltpu.SemaphoreType.DMA(())]),
        out_shape=jax.ShapeDtypeStruct((n_rows, w), F32),
        compiler_params=_cparams(("arbitrary",)),
        name="moe_dispatch",
    )(pos, pad, h, h_small)


def _expert_kernel(te_ref, nv_ref, x_ref, wg_ref, wu_ref, wd_ref, y_ref):
    del te_ref
    i = pl.program_id(0)

    @pl.when(i < nv_ref[0])
    def _():
        x = x_ref[...].astype(BF16)
        a = _dot(x, wg_ref[...].astype(BF16))
        u = _dot(x, wu_ref[...].astype(BF16))
        hid = a * _sigmoid(a) * u
        y_ref[...] = _dot(hid.astype(BF16), wd_ref[...].astype(BF16))

    @pl.when(i >= nv_ref[0])
    def _():
        y_ref[...] = jnp.zeros(y_ref.shape, F32)


def _experts(tile_expert, n_valid, xs, w_gate, w_up, w_down, tm):
    rows, half = xs.shape
    ne, d, f = w_gate.shape
    w_spec = lambda a, b: pl.BlockSpec((None, a, b), lambda i, te, nv: (te[i], 0, 0))
    return pl.pallas_call(
        _expert_kernel,
        grid_spec=pltpu.PrefetchScalarGridSpec(
            num_scalar_prefetch=2, grid=(rows // tm,),
            in_specs=[pl.BlockSpec((tm, half), lambda i, te, nv: (jnp.minimum(i, nv[0] - 1), 0)),
                      w_spec(d, f), w_spec(d, f), w_spec(f, d)],
            out_specs=pl.BlockSpec((tm, d), lambda i, te, nv: (i, 0))),
        out_shape=jax.ShapeDtypeStruct((rows, d), F32),
        compiler_params=_cparams(("arbitrary",)),
        name="moe_experts",
    )(tile_expert, n_valid, xs, w_gate, w_up, w_down)


def _combine_kernel(pos_ref, ys_hbm, x_ref, ga_ref, rw_ref, o_ref, buf, sem, *, tm):
    i = pl.program_id(0)
    n = pl.num_programs(0)

    def issue(tile, slot):
        def body(r, c):
            t = tile * tm + r
            pltpu.make_async_copy(ys_hbm.at[pl.ds(pos_ref[2 * t], 1)], buf.at[slot, pl.ds(r, 1)],
                                  sem.at[slot]).start()
            pltpu.make_async_copy(ys_hbm.at[pl.ds(pos_ref[2 * t + 1], 1)], buf.at[slot, pl.ds(tm + r, 1)],
                                  sem.at[slot]).start()
            return c

        lax.fori_loop(0, tm, body, 0, unroll=8)

    @pl.when(i == 0)
    def _():
        issue(0, 0)

    @pl.when(i + 1 < n)
    def _():
        issue(i + 1, (i + 1) % 2)

    slot = i % 2
    pltpu.make_async_copy(ys_hbm.at[pl.ds(0, 2 * tm)], buf.at[slot], sem.at[slot]).wait()
    w = rw_ref[...]
    y = w[:, 0:1] * buf[slot, 0:tm] + w[:, 1:2] * buf[slot, tm:2 * tm]
    o_ref[...] = x_ref[...] + ga_ref[...] * y


def _combine(pos, ys, x1, ga, rw, tm, rows_per_mod):
    n, d = x1.shape
    r = ga.shape[1]
    tiles_per_mod = rows_per_mod // tm
    return pl.pallas_call(
        functools.partial(_combine_kernel, tm=tm),
        grid_spec=pltpu.PrefetchScalarGridSpec(
            num_scalar_prefetch=1, grid=(n // tm,),
            in_specs=[pl.BlockSpec(memory_space=pl.ANY),
                      pl.BlockSpec((tm, d), lambda i, p: (i, 0)),
                      pl.BlockSpec((None, r, d), lambda i, p: (i // tiles_per_mod, 0, 0)),
                      pl.BlockSpec((tm, LANES), lambda i, p: (i, 0))],
            out_specs=pl.BlockSpec((tm, d), lambda i, p: (i, 0)),
            scratch_shapes=[pltpu.VMEM((2, 2 * tm, d), F32), pltpu.SemaphoreType.DMA((2,))]),
        out_shape=jax.ShapeDtypeStruct((n, d), F32),
        compiler_params=_cparams(("arbitrary",)),
        name="moe_combine",
    )(pos, ys, x1, ga, rw)


def _grouped_moe(big, small, w_gate, w_up, w_down, tm_e=256, tm_c=256):
    ne = w_gate.shape[0]
    n_all = big["h"].shape[0] + small["h"].shape[0]
    tm_e = min(tm_e, big["h"].shape[0])
    cnt_big = big["cnt"][0, ROUTE_LANE0:ROUTE_LANE0 + ne]
    counts = cnt_big + small["cnt"][0, ROUTE_LANE0:ROUTE_LANE0 + ne]
    tiles = (counts + tm_e - 1) // tm_e
    tile_end = jnp.cumsum(tiles)
    start_row = (tile_end - tiles) * tm_e
    n_tiles = (2 * n_all + tm_e - 1) // tm_e + ne
    tile_expert = jnp.minimum(jnp.sum(tile_end[None, :] <= jnp.arange(n_tiles)[:, None], axis=1), ne - 1)

    def positions(stream, first_rank):
        e, rank = stream["idx"][:, 0:2], stream["idx"][:, 2:4]
        return (jnp.take(start_row + first_rank, e, axis=0) + rank).reshape(-1).astype(jnp.int32)

    pos_big, pos_small = positions(big, 0), positions(small, cnt_big)
    pad = jnp.concatenate([start_row + counts, tiles * tm_e - counts, tile_end[ne - 1:]]).astype(jnp.int32)
    xs = _dispatch(jnp.concatenate([pos_big, pos_small]), pad, big["h"], small["h"], n_tiles * tm_e, tm_e, ne)
    ys = _experts(tile_expert.astype(jnp.int32), tile_end[ne - 1:].astype(jnp.int32), xs, w_gate, w_up, w_down, tm_e)
    return [_combine(pos, ys, s["x1"], s["ga"], s["rw"], min(tm_c, s["h"].shape[0]), s["rows_per_mod"])
            for s, pos in ((big, pos_big), (small, pos_small))]


def _pick_tile(n, pref):
    t = min(pref, n)
    while n % t:
        t //= 2
    return t


def _layer(x, mods, s0, attend, lw, lam_init, hg_chunk, hg_heads_per_step, per_row_mod):
    (n1, n2, w_in, gq, gk, subln, lam4, hlb, hg_g, w_out, w_route, b_route, w_gate, w_up, w_down) = lw
    b, t, d = x.shape
    n = b * t
    x2d = x.reshape(n, d)
    if per_row_mod:
        tm = n
        rows_per_mod = n
        mods = [jnp.repeat(m, t, axis=0).reshape(1, n, d) for m in mods]
    else:
        tm = _pick_tile(t, 512)
        rows_per_mod = t
        mods = [m.reshape(b, 1, d) for m in mods]
    sh1, sc1, ga1, sh2, sc2, ga2 = mods

    tm_in = tm if per_row_mod else _pick_tile(t, 1024)
    q, k, v, hmain, hk = _in_proj(x2d, n1, sc1, sh1, w_in, gq, gk, hlb, tm_in, rows_per_mod)
    a = attend(q.reshape(b, t, -1), k.reshape(b, t, -1), v.reshape(b, t, -1))
    tpad = -(-t // hg_chunk) * hg_chunk
    hm3, hk3 = hmain.reshape(b, t, -1), hk.reshape(b, t, -1)
    if tpad != t:
        hm3 = jnp.pad(hm3, ((0, 0), (0, tpad - t), (0, 0)))
        hk3 = jnp.pad(hk3, ((0, 0), (0, tpad - t), (0, 0)))
    m, s_new = _hgrn2(hm3, hk3, hg_g, s0, hg_chunk, _pick_tile(tpad, 2 * hg_chunk), hg_heads_per_step)
    m = m[:, :t]
    x1 = _out_proj(a.reshape(n, -1), m.reshape(n, -1), w_out, x2d, ga1, tm, rows_per_mod)
    hp, idx, rw, cnt = _router(x1, n2, sc2, sh2, w_route, b_route, tm, rows_per_mod)
    routed = dict(h=hp, idx=idx, rw=rw, cnt=cnt, x1=x1, ga=ga2, rows_per_mod=rows_per_mod)
    return routed, k, v, s_new


def kernel(x_prompt, x_sample, cache_k, cache_v, state_hgrn, page_table, c_prompt, c_sample, norm1_g, norm2_g, w_ada, b_ada, w_in, q_norm_g, k_norm_g, lambda_q1, lambda_k1, lambda_q2, lambda_k2, subln_g, hg_lower_bound, hg_norm_g, w_out, w_router_group, b_router_group, w_router_expert, b_router_expert, w_exp_gate, w_exp_up, w_exp_down):
    depth = norm1_g.shape[0]
    assert depth == 1, "single-layer trunk"
    l = 0
    lam_init = 0.8 - 0.6 * math.exp(-0.3 * l)
    bp, tp, d = x_prompt.shape
    bs, ts, _ = x_sample.shape

    c_all = jnp.concatenate([c_prompt, c_sample], axis=0)
    rpad = -(-c_all.shape[0] // 8) * 8
    c_all = jnp.pad(c_all, ((0, rpad - c_all.shape[0]), (0, 0)))
    mod = _adaln(c_all, w_ada[l], b_ada[l])
    mods_p = [mod[:bp, i * d:(i + 1) * d] for i in range(6)]
    mods_s = [mod[bp:bp + bs, i * d:(i + 1) * d] for i in range(6)]

    w_re = jnp.transpose(w_router_expert[l], (1, 0, 2)).reshape(d, N_EXPERTS)
    w_route = jnp.pad(jnp.concatenate([w_router_group[l], w_re], axis=1), ((0, 0), (0, LANES - N_GROUPS - N_EXPERTS)))
    b_route = jnp.pad(jnp.concatenate([b_router_group[l], b_router_expert[l].reshape(-1)]),
                      (0, LANES - N_GROUPS - N_EXPERTS)).reshape(1, LANES)
    lam4 = jnp.stack([lambda_q1[l], lambda_k1[l], lambda_q2[l], lambda_k2[l]], axis=0)

    lw = (norm1_g[l], norm2_g[l], w_in[l], q_norm_g[l], k_norm_g[l], subln_g[l], lam4, hg_lower_bound,
          hg_norm_g[l], w_out[l], w_route, b_route, w_exp_gate[l], w_exp_up[l], w_exp_down[l])

    def attend_prompt(q, k, v):
        return _prompt_attention(q, k, v, lam4, subln_g[l], lam_init)

    def attend_sample(q, k, v):
        pad = lambda a: jnp.pad(a, ((0, 0), (0, -ts % 8), (0, 0)))
        o = _sample_attention(pad(q), pad(k), pad(v), ts, cache_k[l], cache_v[l], page_table, lam4,
                              subln_g[l], lam_init)
        return o[:, :ts]

    routed_p, kp, vp, sp = _layer(x_prompt, mods_p, None, attend_prompt, lw, lam_init, 128, 4, False)
    routed_s, kn, vn, sn = _layer(x_sample, mods_s, state_hgrn[l], attend_sample, lw, lam_init, 8, HG_HEADS, True)
    yp, ys = _grouped_moe(routed_p, routed_s, w_exp_gate[l], w_exp_up[l], w_exp_down[l])

    kv_shape = lambda b, t: (1, b, t, ATTN_KV_HEADS, 2 * QK_DIM)
    return (yp.reshape(x_prompt.shape), ys.reshape(x_sample.shape),
            kp.reshape(kv_shape(bp, tp)), vp.reshape(kv_shape(bp, tp)),
            kn.reshape(kv_shape(bs, ts)), vn.reshape(kv_shape(bs, ts)),
            sp[None], sn[None])
```

```python
import functools
import math

import jax
import jax.numpy as jnp
import numpy as np
from jax import lax
from jax.experimental import pallas as pl
from jax.experimental.pallas import tpu as pltpu

F32 = jnp.float32
BF16 = jnp.bfloat16

QK_DIM = 64
V_DIM = 128
ATTN_KV_HEADS = 4
ATTN_GROUP = 2
ATTN_HEADS = ATTN_KV_HEADS * ATTN_GROUP
HG_HEADS = 8
HG_DIM = 128
N_GROUPS = 4
EXPERTS_PER_GROUP = 4
N_EXPERTS = N_GROUPS * EXPERTS_PER_GROUP
NORM_EPS = 1e-6
ALIBI_MAX_BIAS = 8.0
LANES = 128
MXU_DIM = 256
VMEM_LIMIT = 60 * 1024 * 1024
ROUTE_LANE0 = N_GROUPS
HG_TABLE_LEVELS = 2


def _cparams(sem):
    return pltpu.CompilerParams(dimension_semantics=sem, vmem_limit_bytes=VMEM_LIMIT)


def _dot(a, b):
    return jnp.dot(a, b, preferred_element_type=F32)


def _dot_nt(a, b):
    return lax.dot_general(a, b, (((1,), (1,)), ((), ())), preferred_element_type=F32)


def _dot_tn(a, b):
    return lax.dot_general(a, b, (((0,), (0,)), ((), ())), preferred_element_type=F32)


def _split3(x):
    hi = x.astype(BF16)
    r = x - hi.astype(F32)
    mid = r.astype(BF16)
    lo = (r - mid.astype(F32)).astype(BF16)
    return hi, mid, lo


def _sigmoid(x):
    return 1.0 / (1.0 + jnp.exp(-x))


def _rms(x, g):
    return x * lax.rsqrt(jnp.mean(x * x, axis=-1, keepdims=True) + NORM_EPS) * g


def _ada_kernel(c_ref, w_ref, b_ref, o_ref):
    c = c_ref[...]
    s = c * _sigmoid(c)
    o_ref[...] = _dot(s.astype(BF16), w_ref[...].astype(BF16)) + b_ref[...]


def _adaln(c_all, w_ada, b_ada, tn=1024):
    r, d = c_all.shape
    n = w_ada.shape[1]
    return pl.pallas_call(
        _ada_kernel,
        grid=(n // tn,),
        in_specs=[
            pl.BlockSpec((r, d), lambda j: (0, 0)),
            pl.BlockSpec((d, tn), lambda j: (0, j)),
            pl.BlockSpec((1, tn), lambda j: (0, j)),
        ],
        out_specs=pl.BlockSpec((r, tn), lambda j: (0, j)),
        out_shape=jax.ShapeDtypeStruct((r, n), F32),
        compiler_params=_cparams(("arbitrary",)),
        name="adaln",
    )(c_all, w_ada, b_ada.reshape(1, n))


IN_TN = 512


def _group_sumsq(p, bd):
    x2 = p * p
    hi = x2.astype(BF16)
    lo = (x2 - hi.astype(F32)).astype(BF16)
    outs = []
    for c in range(p.shape[1] // MXU_DIM):
        sl = slice(c * MXU_DIM, (c + 1) * MXU_DIM)
        outs.append(_dot(hi[:, sl], bd) + _dot(lo[:, sl], bd))
    return jnp.concatenate(outs, axis=1)


def _in_epilogue(c, p, cs, gq_ref, gk_ref, lb_ref, bd_ref, q_ref, k_ref, v_ref, hm_ref, hk_ref):
    def qk_norm(g):
        ms = _group_sumsq(p, bd_ref[...]) * (1.0 / QK_DIM)
        return p * lax.rsqrt(ms + NORM_EPS) * g

    if c < 2:
        q_ref[:, cs] = (qk_norm(gq_ref[:, cs]) * (QK_DIM ** -0.5)).astype(BF16)
    elif c == 2:
        k_ref[:, cs] = qk_norm(gk_ref[:, cs])
    elif c == 3:
        v_ref[:, cs] = p
    elif c in (4, 5, 10, 11):
        hm_ref[:, cs] = p * _sigmoid(p)
    elif c in (6, 7):
        a = lb_ref[:, cs]
        e = jnp.exp(a - jnp.max(a, axis=0, keepdims=True))
        lb = e[0:1] / jnp.sum(e, axis=0, keepdims=True)
        sg = _sigmoid(p)
        hm_ref[:, cs] = jnp.log(lb + (1.0 - lb) * sg)
        hk_ref[:, cs] = (1.0 - lb) * (1.0 - sg)
    else:
        hm_ref[:, cs] = p


def _in_kernel(x_ref, n1_ref, sc_ref, sh_ref, w_ref, gq_ref, gk_ref, lb_ref, bd_ref,
               q_ref, k_ref, v_ref, hm_ref, hk_ref, h_scr, *, nj):
    j = pl.program_id(1)

    @pl.when(j == 0)
    def _():
        y = _rms(x_ref[...], n1_ref[...])
        h_scr[...] = (y * (1.0 + sc_ref[...]) + sh_ref[...]).astype(BF16)

    outs = (gq_ref, gk_ref, lb_ref, bd_ref, q_ref, k_ref, v_ref, hm_ref, hk_ref)
    for cols in ((0, 1), (2,), (3,), (4, 5, 10, 11), (6, 7), (8, 9)):
        cond = functools.reduce(lambda a, b: a | b, [j == c for c in cols])
        width = IN_TN if cols[0] <= 2 else MXU_DIM

        @pl.when(cond)
        def _(c=cols[0], width=width):
            for u in range(IN_TN // width):
                cs = slice(u * width, (u + 1) * width)
                p = _dot(h_scr[...], w_ref[:, cs].astype(BF16))
                _in_epilogue(c, p, cs, *outs)


def _in_proj(x2d, n1, sc, sh, w_in, gq, gk, hlb, tm, rows_per_mod):
    n, d = x2d.shape
    cols = w_in.shape[1]
    nj = cols // IN_TN
    assert nj == 12 and n % tm == 0
    r = sc.shape[1]
    tiles_per_mod = rows_per_mod // tm
    bd = jnp.asarray(np.kron(np.eye(MXU_DIM // QK_DIM), np.ones((QK_DIM, QK_DIM))), BF16)
    clamp = lambda j, lo: jnp.clip(j - lo, 0, 1)
    mod_spec = pl.BlockSpec((None, r, d), lambda i, j: (i // tiles_per_mod, 0, 0))
    return pl.pallas_call(
        functools.partial(_in_kernel, nj=nj),
        grid=(n // tm, nj),
        in_specs=[
            pl.BlockSpec((tm, d), lambda i, j: (i, 0)),
            pl.BlockSpec((1, d), lambda i, j: (0, 0)),
            mod_spec, mod_spec,
            pl.BlockSpec((d, IN_TN), lambda i, j: (0, j)),
            pl.BlockSpec((1, IN_TN), lambda i, j: (0, 0)),
            pl.BlockSpec((1, IN_TN), lambda i, j: (0, 0)),
            pl.BlockSpec((hlb.shape[0], IN_TN), lambda i, j: (0, clamp(j, 6))),
            pl.BlockSpec((MXU_DIM, MXU_DIM), lambda i, j: (0, 0)),
        ],
        out_specs=[
            pl.BlockSpec((tm, IN_TN), lambda i, j: (i, clamp(j, 0))),
            pl.BlockSpec((tm, IN_TN), lambda i, j: (i, 0)),
            pl.BlockSpec((tm, IN_TN), lambda i, j: (i, 0)),
            pl.BlockSpec((tm, IN_TN), lambda i, j: (i, jnp.clip(j - 4, 0, 7))),
            pl.BlockSpec((tm, IN_TN), lambda i, j: (i, clamp(j, 6))),
        ],
        out_shape=[
            jax.ShapeDtypeStruct((n, 2 * IN_TN), BF16),
            jax.ShapeDtypeStruct((n, IN_TN), F32),
            jax.ShapeDtypeStruct((n, IN_TN), F32),
            jax.ShapeDtypeStruct((n, 8 * IN_TN), F32),
            jax.ShapeDtypeStruct((n, 2 * IN_TN), F32),
        ],
        scratch_shapes=[pltpu.VMEM((tm, d), BF16)],
        compiler_params=_cparams(("arbitrary", "arbitrary")),
        name="in_proj",
    )(x2d, n1.reshape(1, d), sc, sh, w_in,
      jnp.tile(gq.reshape(1, QK_DIM), (1, IN_TN // QK_DIM)),
      jnp.tile(gk.reshape(1, QK_DIM), (1, IN_TN // QK_DIM)), hlb, bd)


def _alibi_slopes_np():
    h = np.arange(1, ATTN_HEADS + 1, dtype=np.float64)
    return np.exp2(-ALIBI_MAX_BIAS * h / ATTN_HEADS).reshape(ATTN_KV_HEADS, ATTN_GROUP).astype(np.float32)


def _alibi_slopes():
    out = np.zeros((ATTN_KV_HEADS, 1, LANES), np.float32)
    out[:, 0, :ATTN_GROUP] = _alibi_slopes_np()
    return jnp.asarray(out)


def _lambda(lam_ref, lam_init):
    l = lam_ref[...]
    s1 = jnp.sum(l[0:1] * l[1:2], axis=-1, keepdims=True)
    s2 = jnp.sum(l[2:3] * l[3:4], axis=-1, keepdims=True)
    return jnp.exp(s1) - jnp.exp(s2) + lam_init


def _diff_finish(o_all, t, lam, subln, lam_init, o_ref):
    for g in range(ATTN_GROUP):
        o = o_all[(2 * g) * t:(2 * g + 1) * t] - lam * o_all[(2 * g + 1) * t:(2 * g + 2) * t]
        y = _rms(o, subln) * (1.0 - lam_init)
        o_ref[:, g * LANES:(g + 1) * LANES] = y.astype(o_ref.dtype)


def _softmax_step_t(s, m_old, l_old):
    m_new = jnp.maximum(m_old, jnp.max(s, axis=0, keepdims=True))
    alpha = jnp.exp(m_old - m_new)
    p = jnp.exp(s - m_new)
    return m_new, alpha, alpha * l_old + jnp.sum(p, axis=0, keepdims=True), p.astype(BF16)


def _pattn_kernel(sl_ref, lam_ref, subln_ref, q_ref, k_ref, v_ref, o_ref, kb, vt, qs, s_a, s_b, m_scr, l_scr, acc,
                  *, tq, nblk, lam_init):
    qi = pl.program_id(2)

    @pl.when(qi == 0)
    def _():
        t = k_ref.shape[0]
        pos = lax.broadcasted_iota(jnp.int32, (t, LANES), 0)
        ln = lax.broadcasted_iota(jnp.int32, (t, LANES), 1)
        aug = jnp.where(ln == 0, pos & ~(LANES - 1), jnp.where(ln == 1, pos & (LANES - 1), 0))
        kb[:, :LANES] = k_ref[...].astype(BF16)
        kb[:, LANES:] = aug.astype(F32).astype(BF16)
        for jb in range(t // tq):
            vt[jb] = v_ref[jb * tq:(jb + 1) * tq, :].T.astype(BF16)

    rows = 4 * tq
    half = 2 * tq
    q = q_ref[...]
    lane = lax.broadcasted_iota(jnp.int32, (tq, LANES), 1)
    zero = jnp.zeros((tq, LANES), BF16)
    for g in range(ATTN_GROUP):
        qg = q[:, g * LANES:(g + 1) * LANES]
        aug = jnp.where(lane < 2, sl_ref[0:1, g:g + 1], 0.0).astype(BF16)
        qs[(2 * g) * tq:(2 * g + 1) * tq, :] = jnp.concatenate([jnp.where(lane < QK_DIM, qg, zero), aug], axis=1)
        qs[(2 * g + 1) * tq:(2 * g + 2) * tq, :] = jnp.concatenate([jnp.where(lane >= QK_DIM, qg, zero), aug], axis=1)
    t_q = lax.broadcasted_iota(jnp.int32, (1, tq), 1)
    m_scr[...] = jnp.full(m_scr.shape, -jnp.inf, F32)
    l_scr[...] = jnp.zeros(l_scr.shape, F32)
    acc[...] = jnp.zeros(acc.shape, F32)

    s_bufs = (s_a, s_b)
    groups = [slice(u * tq, (u + 1) * tq) for u in range(rows // tq)]

    def logits(jb, cs):
        s_bufs[jb % 2][:, cs] = _dot_nt(kb[jb * tq:(jb + 1) * tq, :], qs[cs, :])

    def attend(jb, cs, masked):
        s = s_bufs[jb % 2][:, cs]
        if masked:
            key = lax.broadcasted_iota(jnp.int32, (tq, 1), 0)
            s = jnp.where(key <= t_q, s, -jnp.inf)
        m_new, alpha, l_new, p = _softmax_step_t(s, m_scr[:, cs], l_scr[:, cs])
        acc[:, cs] = alpha * acc[:, cs] + _dot(vt[jb], p)
        m_scr[:, cs] = m_new
        l_scr[:, cs] = l_new

    for cs in groups:
        logits(0, cs)
    for jb in range(nblk):
        if jb + 1 < nblk:
            @pl.when(jb < qi)
            def _(jb=jb):
                for cs in groups:
                    logits(jb + 1, cs)
                    attend(jb, cs, False)

        @pl.when(jb == qi)
        def _(jb=jb):
            for cs in groups:
                attend(jb, cs, True)

    o_t = acc[...] * (1.0 / l_scr[...])
    lam = _lambda(lam_ref, lam_init)
    for g in range(ATTN_GROUP):
        d = o_t[:, (2 * g) * tq:(2 * g + 1) * tq] - lam * o_t[:, (2 * g + 1) * tq:(2 * g + 2) * tq]
        y = _rms(d.T, subln_ref[...]) * (1.0 - lam_init)
        o_ref[:, g * LANES:(g + 1) * LANES] = y.astype(o_ref.dtype)


def _prompt_attention(q, k, v, lam4, subln, lam_init, tq=512):
    b, t, _ = q.shape
    tq = min(tq, t)
    assert t % tq == 0 and t < LANES * 256
    rows = 4 * tq
    kern = functools.partial(_pattn_kernel, tq=tq, nblk=t // tq, lam_init=lam_init)
    return pl.pallas_call(
        kern,
        grid=(b, ATTN_KV_HEADS, t // tq),
        in_specs=[
            pl.BlockSpec((None, 1, LANES), lambda bi, h, qi: (h, 0, 0)),
            pl.BlockSpec((4, QK_DIM), lambda bi, h, qi: (0, 0)),
            pl.BlockSpec((1, V_DIM), lambda bi, h, qi: (0, 0)),
            pl.BlockSpec((None, tq, 2 * LANES), lambda bi, h, qi: (bi, qi, h)),
            pl.BlockSpec((None, t, LANES), lambda bi, h, qi: (bi, 0, h)),
            pl.BlockSpec((None, t, LANES), lambda bi, h, qi: (bi, 0, h)),
        ],
        out_specs=pl.BlockSpec((None, tq, 2 * LANES), lambda bi, h, qi: (bi, qi, h)),
        out_shape=jax.ShapeDtypeStruct((b, t, ATTN_HEADS * V_DIM), BF16),
        scratch_shapes=[
            pltpu.VMEM((t, 2 * LANES), BF16), pltpu.VMEM((t // tq, V_DIM, tq), BF16),
            pltpu.VMEM((rows, 2 * LANES), BF16), pltpu.VMEM((tq, rows), F32), pltpu.VMEM((tq, rows), F32),
            pltpu.VMEM((1, rows), F32), pltpu.VMEM((1, rows), F32), pltpu.VMEM((V_DIM, rows), F32),
        ],
        compiler_params=_cparams(("arbitrary", "arbitrary", "arbitrary")),
        name="prompt_attention",
    )(_alibi_slopes(), lam4, subln.reshape(1, V_DIM), q, k, v)


def _to_col(row):
    n = row.shape[1]
    eye = lax.broadcasted_iota(jnp.int32, (n, n), 0) == lax.broadcasted_iota(jnp.int32, (n, n), 1)
    return jnp.sum(jnp.where(eye, row, 0.0), axis=1, keepdims=True)


def _sattn_kernel(pt_ref, srow_ref, lam_ref, subln_ref, q_ref, kn_ref, vn_ref, *rest,
                  npg, nj, page, t, t_real, lam_init):
    del pt_ref
    kp_refs, vp_refs = rest[:npg], rest[npg:2 * npg]
    o_ref = rest[2 * npg]
    kc, vc, qf_scr, qbd_scr, m_scr, l_scr, acc = rest[2 * npg + 1:]
    j = pl.program_id(1)
    ntok = npg * page
    past = nj * ntok
    nq = qbd_scr.shape[0]
    width = ATTN_KV_HEADS * LANES
    srow = srow_ref[...]

    @pl.when(j == 0)
    def _():
        m_scr[...] = jnp.full(m_scr.shape, -jnp.inf, F32)
        l_scr[...] = jnp.zeros(l_scr.shape, F32)
        acc[...] = jnp.zeros(acc.shape, F32)
        qf = q_ref[...].astype(F32)
        lane = lax.broadcasted_iota(jnp.int32, (t, LANES), 1)
        qf_scr[...] = jnp.zeros(qf_scr.shape, F32)
        for hg in range(ATTN_HEADS):
            h = hg // ATTN_GROUP
            qh = qf[:, hg * LANES:(hg + 1) * LANES]
            qf_scr[(2 * hg) * t:(2 * hg + 1) * t, h * LANES:(h + 1) * LANES] = jnp.where(lane < QK_DIM, qh, 0.0)
            qf_scr[(2 * hg + 1) * t:(2 * hg + 2) * t, h * LANES:(h + 1) * LANES] = jnp.where(lane >= QK_DIM, qh, 0.0)
        qbd_scr[...] = qf_scr[...].astype(BF16)

    qbd = qbd_scr[...]

    for p in range(npg):
        kc[p * page:(p + 1) * page, :] = kp_refs[p][...].reshape(page, width).astype(BF16)
    kpos = lax.broadcasted_iota(jnp.int32, (ntok, nq), 0) + j * ntok
    hw = ntok // 2
    s = jnp.concatenate([_dot_nt(kc[:hw, :], qbd), _dot_nt(kc[hw:, :], qbd)], axis=0)
    s = s + kpos.astype(F32) * srow
    for p in range(npg):
        vc[p * page:(p + 1) * page, :] = vp_refs[p][...].reshape(page, width).astype(BF16)
    m_new, alpha, l_new, p = _softmax_step_t(s, m_scr[...], l_scr[...])
    pv = _dot_tn(p[:hw], vc[:hw, :]) + _dot_tn(p[hw:], vc[hw:, :])
    acc[...] = _to_col(alpha) * acc[...] + pv
    m_scr[...] = m_new
    l_scr[...] = l_new

    @pl.when(j == nj - 1)
    def _():
        t_q = lax.broadcasted_iota(jnp.int32, (1, nq), 1) % t
        c = lax.broadcasted_iota(jnp.int32, (t, 1), 0)
        sn = _dot_nt(kn_ref[...].astype(BF16), qbd) + (past + c).astype(F32) * srow
        sn = jnp.where((c <= t_q) & (c < t_real), sn, -jnp.inf)
        m_fin, alpha_n, l_fin, _ = _softmax_step_t(sn, m_scr[...], l_scr[...])
        pn = jnp.exp(sn - m_fin)
        o = _to_col(alpha_n) * acc[...]
        vn = vn_ref[...]
        for u in range(t_real):
            o = o + _to_col(pn[u:u + 1]) * vn[u:u + 1]
        o = o * _to_col(1.0 / l_fin)
        lam = _lambda(lam_ref, lam_init)
        rq = nq // ATTN_KV_HEADS
        for h in range(ATTN_KV_HEADS):
            o_h = o[h * rq:(h + 1) * rq, h * LANES:(h + 1) * LANES]
            _diff_finish(o_h, t, lam, subln_ref[...], lam_init, o_ref.at[:, 2 * h * LANES:(2 * h + 2) * LANES])


def _sample_attention(q, k_new, v_new, t_real, cache_k, cache_v, page_table, lam4, subln, lam_init, npg=32):
    b, t, _ = q.shape
    tpad = t
    page = cache_k.shape[1]
    n_pages = page_table.shape[1]
    npg = min(npg, n_pages)
    assert n_pages % npg == 0
    ntok = npg * page
    width = ATTN_KV_HEADS * LANES
    nq = ATTN_KV_HEADS * ATTN_GROUP * 2 * t
    nj = n_pages // npg
    kern = functools.partial(_sattn_kernel, npg=npg, nj=nj, page=page, t=t, t_real=t_real, lam_init=lam_init)
    slope_row = jnp.asarray(np.repeat(_alibi_slopes_np().reshape(-1), 2 * t)[None, :])

    def page_spec(p):
        return pl.BlockSpec((None, page, ATTN_KV_HEADS, LANES),
                            lambda bi, j, pt: (pt[bi * n_pages + j * npg + p], 0, 0, 0))

    grid_spec = pltpu.PrefetchScalarGridSpec(
        num_scalar_prefetch=1,
        grid=(b, nj),
        in_specs=[
            pl.BlockSpec((1, nq), lambda bi, j, pt: (0, 0)),
            pl.BlockSpec((4, QK_DIM), lambda bi, j, pt: (0, 0)),
            pl.BlockSpec((1, V_DIM), lambda bi, j, pt: (0, 0)),
            pl.BlockSpec((None, t, ATTN_HEADS * V_DIM), lambda bi, j, pt: (bi, 0, 0)),
            pl.BlockSpec((None, tpad, width), lambda bi, j, pt: (bi, 0, 0)),
            pl.BlockSpec((None, tpad, width), lambda bi, j, pt: (bi, 0, 0)),
        ] + [page_spec(p) for p in range(npg)] * 2,
        out_specs=pl.BlockSpec((None, t, ATTN_HEADS * V_DIM), lambda bi, j, pt: (bi, 0, 0)),
        scratch_shapes=[
            pltpu.VMEM((ntok, width), BF16), pltpu.VMEM((ntok, width), BF16),
            pltpu.VMEM((nq, width), F32), pltpu.VMEM((nq, width), BF16),
            pltpu.VMEM((1, nq), F32), pltpu.VMEM((1, nq), F32), pltpu.VMEM((nq, width), F32),
        ],
    )
    return pl.pallas_call(
        kern,
        grid_spec=grid_spec,
        out_shape=jax.ShapeDtypeStruct((b, t, ATTN_HEADS * V_DIM), BF16),
        compiler_params=_cparams(("arbitrary", "arbitrary")),
        name="sample_attention",
    )(page_table.reshape(-1), slope_row, lam4, subln.reshape(1, V_DIM), q, k_new, v_new,
      *([cache_k] * npg), *([cache_v] * npg))


def _hgrn_tables(c):
    levels = int(math.log2(c))
    tril = np.tril(np.ones((c, c), np.float32))
    w = [tril]
    mask = [np.eye(c, dtype=np.float32)]
    idx = np.arange(c)
    for l in range(levels):
        bs = 2 << l
        mid = (idx // bs) * bs + bs // 2
        if l < HG_TABLE_LEVELS:
            w.append(tril - tril[mid])
        upper = (idx % bs) >= bs // 2
        same = (idx[:, None] // bs) == (idx[None, :] // bs)
        mask.append((same & upper[:, None] & ~upper[None, :]).astype(np.float32))
    return jnp.asarray(np.concatenate(w, 0), BF16), jnp.asarray(np.stack(mask, 0)), levels


def _minus_mid_rows(g, bs):
    c, n = g.shape
    g3 = g.reshape(c // bs, bs, n)
    return (g3 - g3[:, bs // 2:bs // 2 + 1, :]).reshape(c, n)


def _hgrn_kernel(*refs, c, levels, nchunk, hps, has_s0):
    if has_s0:
        s0_ref, refs = refs[0], refs[1:]
    (w_ref, mask_ref, g_ref, q_ref, lf_ref, v_ref, gate_ref, k_ref, o_ref, sn_ref, st) = refs
    ci = pl.program_id(2)

    @pl.when(ci == 0)
    def _():
        for hh in range(hps):
            st[hh] = s0_ref[hh].T if has_s0 else jnp.zeros((HG_DIM, HG_DIM), F32)

    w = w_ref[...]
    pair = 2 * HG_DIM
    heads = range(hps)
    hsl = [slice(hh * HG_DIM, (hh + 1) * HG_DIM) for hh in heads]
    for n in range(nchunk):
        rs = slice(n * c, (n + 1) * c)
        gx = []
        for hh in range(0, hps, 2):
            l_hi, l_mid, l_lo = _split3(lf_ref[rs, hh * HG_DIM:hh * HG_DIM + pair])
            gx2 = _dot(w, l_hi) + _dot(w, l_mid) + _dot(w, l_lo)
            gx += [gx2[:, :HG_DIM], gx2[:, HG_DIM:]]
        qs = [q_ref[rs, hsl[hh]] for hh in heads]
        kk = [k_ref[rs, hsl[hh]] for hh in heads]
        g = [gx[hh][0:c] for hh in heads]
        a = [mask_ref[0] * _dot_nt(qs[hh].astype(BF16), kk[hh].astype(BF16)) for hh in heads]
        for l in range(levels):
            for hh in heads:
                x = gx[hh][(l + 1) * c:(l + 2) * c] if l < HG_TABLE_LEVELS else _minus_mid_rows(g[hh], 2 << l)
                e = jnp.exp(-jnp.abs(x))
                qa = (qs[hh] * e).astype(BF16)
                kb = (kk[hh] * e).astype(BF16)
                a[hh] = a[hh] + mask_ref[l + 1] * _dot_nt(qa, kb)
        vb = [v_ref[rs, hsl[hh]].astype(BF16) for hh in heads]
        st_old = [st[hh] for hh in heads]
        o = [_dot_nt((qs[hh] * jnp.exp(g[hh])).astype(BF16), st_old[hh].astype(BF16))
             + _dot(a[hh].astype(BF16), vb[hh]) for hh in heads]
        for hh in heads:
            g_end = g[hh][c - 1:c]
            kd = kk[hh] * jnp.exp(g_end - g[hh])
            st[hh] = st_old[hh] * jnp.exp(g_end) + _dot_tn(vb[hh], kd.astype(BF16))
        for hh in heads:
            o_ref[rs, hsl[hh]] = (_rms(o[hh], g_ref[...]) * gate_ref[rs, hsl[hh]]).astype(o_ref.dtype)

    @pl.when(ci == pl.num_programs(2) - 1)
    def _():
        for hh in range(hps):
            sn_ref[hh] = st[hh].T


def _hgrn2(hmain, hk, hg_norm_g, s0, c, tc, hps):
    b, t, _ = hmain.shape
    w, mask, levels = _hgrn_tables(c)
    nl = levels + 1
    has_s0 = s0 is not None
    kern = functools.partial(_hgrn_kernel, c=c, levels=levels, nchunk=tc // c, hps=hps, has_s0=has_s0)
    nhg = HG_HEADS // hps
    col = lambda off: pl.BlockSpec((None, tc, hps * HG_DIM), lambda bi, h, ci: (bi, ci, off * nhg + h))
    state_spec = pl.BlockSpec((None, hps, HG_DIM, HG_DIM), lambda bi, h, ci: (bi, h, 0, 0))
    in_specs = [
        pl.BlockSpec(w.shape, lambda bi, h, ci: (0, 0)),
        pl.BlockSpec((nl, c, c), lambda bi, h, ci: (0, 0, 0)),
        pl.BlockSpec((1, HG_DIM), lambda bi, h, ci: (0, 0)),
        col(0), col(1), col(2), col(3), col(0),
    ]
    args = [w, mask, hg_norm_g.reshape(1, HG_DIM), hmain, hmain, hmain, hmain, hk]
    if has_s0:
        in_specs, args = [state_spec] + in_specs, [s0] + args
    return pl.pallas_call(
        kern,
        grid=(b, nhg, t // tc),
        in_specs=in_specs,
        out_specs=[col(0), state_spec],
        out_shape=[jax.ShapeDtypeStruct((b, t, HG_HEADS * HG_DIM), BF16),
                   jax.ShapeDtypeStruct((b, HG_HEADS, HG_DIM, HG_DIM), F32)],
        scratch_shapes=[pltpu.VMEM((hps, HG_DIM, HG_DIM), F32)],
        compiler_params=_cparams(("arbitrary", "arbitrary", "arbitrary")),
        name="hgrn2",
    )(*args)


def _out_kernel(a_ref, m_ref, w_ref, x_ref, ga_ref, o_ref):
    half = a_ref.shape[1]
    mix = _dot(a_ref[...], w_ref[:half, :]) + _dot(m_ref[...], w_ref[half:, :])
    o_ref[...] = x_ref[...] + ga_ref[...] * mix


def _out_proj(a2d, m2d, w_out, x2d, ga, tm, rows_per_mod):
    n, d = x2d.shape
    half = a2d.shape[1]
    r = ga.shape[1]
    tiles_per_mod = rows_per_mod // tm
    return pl.pallas_call(
        _out_kernel,
        grid=(n // tm,),
        in_specs=[
            pl.BlockSpec((tm, half), lambda i: (i, 0)),
            pl.BlockSpec((tm, half), lambda i: (i, 0)),
            pl.BlockSpec((2 * half, d), lambda i: (0, 0)),
            pl.BlockSpec((tm, d), lambda i: (i, 0)),
            pl.BlockSpec((None, r, d), lambda i: (i // tiles_per_mod, 0, 0)),
        ],
        out_specs=pl.BlockSpec((tm, d), lambda i: (i, 0)),
        out_shape=jax.ShapeDtypeStruct((n, d), F32),
        compiler_params=_cparams(("arbitrary",)),
        name="out_proj",
    )(a2d, m2d, w_out.astype(BF16), x2d, ga)


def _router_kernel(x_ref, n2_ref, sc_ref, sh_ref, wr_ref, br_ref, tril_ref, h_ref, idx_ref, rw_ref, cnt_ref, cnt_scr):
    h = _rms(x_ref[...], n2_ref[...]) * (1.0 + sc_ref[...]) + sh_ref[...]
    h_ref[...] = h
    h_hi = h.astype(BF16)
    h_lo = (h - h_hi.astype(F32)).astype(BF16)
    w = wr_ref[...]
    w_hi = w.astype(BF16)
    w_lo = (w - w_hi.astype(F32)).astype(BF16)
    parts = []
    for rs in (slice(0, h.shape[0] // 2), slice(h.shape[0] // 2, h.shape[0])):
        parts.append(_dot(h_hi[rs], w_hi) + _dot(h_hi[rs], w_lo) + _dot(h_lo[rs], w_hi))
    lg = jnp.concatenate(parts, axis=0) + br_ref[...]

    lane = lax.broadcasted_iota(jnp.int32, lg.shape, 1)
    neg = -jnp.inf
    big = jnp.int32(LANES)

    def top(valid):
        m = jnp.max(jnp.where(valid, lg, neg), axis=-1, keepdims=True)
        idx = jnp.min(jnp.where(valid & (lg == m), lane, big), axis=-1, keepdims=True)
        return m, idx

    is_group = lane < N_GROUPS
    gmax, gidx = top(is_group)
    pg_top = 1.0 / jnp.sum(jnp.where(is_group, jnp.exp(lg - gmax), 0.0), axis=-1, keepdims=True)
    lo = ROUTE_LANE0 + EXPERTS_PER_GROUP * gidx
    in_group = (lane >= lo) & (lane < lo + EXPERTS_PER_GROUP)
    l0, e0 = top(in_group)
    l1, e1 = top(in_group & (lane != e0))
    r = jnp.exp(l1 - l0)
    w0 = pg_top / (1.0 + r)
    w1 = w0 * r

    @pl.when(pl.program_id(0) == 0)
    def _():
        cnt_scr[...] = jnp.zeros(cnt_scr.shape, F32)

    is0, is1 = lane == e0, lane == e1
    onehot = jnp.where(is0 | is1, 1.0, 0.0)
    before = _dot(tril_ref[...], onehot.astype(BF16)) + cnt_scr[...]
    rank0 = jnp.sum(jnp.where(is0, before, 0.0), axis=-1, keepdims=True).astype(jnp.int32)
    rank1 = jnp.sum(jnp.where(is1, before, 0.0), axis=-1, keepdims=True).astype(jnp.int32)
    cnt_scr[...] = cnt_scr[...] + jnp.sum(onehot, axis=0, keepdims=True)
    idx_ref[...] = jnp.where(lane == 0, e0 - ROUTE_LANE0, jnp.where(lane == 1, e1 - ROUTE_LANE0,
                             jnp.where(lane == 2, rank0, jnp.where(lane == 3, rank1, 0))))
    rw_ref[...] = jnp.where(lane == 0, w0, jnp.where(lane == 1, w1, 0.0))
    cnt_ref[...] = cnt_scr[...].astype(jnp.int32)


def _router(x1, n2, sc, sh, w_route, b_route, tm, rows_per_mod):
    n, d = x1.shape
    r = sc.shape[1]
    tiles_per_mod = rows_per_mod // tm
    mod_spec = pl.BlockSpec((None, r, d), lambda i: (i // tiles_per_mod, 0, 0))
    row_spec = lambda w: pl.BlockSpec((tm, w), lambda i: (i, 0))
    fixed = lambda a, b: pl.BlockSpec((a, b), lambda i: (0, 0))
    return pl.pallas_call(
        _router_kernel,
        grid=(n // tm,),
        in_specs=[row_spec(d), fixed(1, d), mod_spec, mod_spec, fixed(d, LANES), fixed(1, LANES), fixed(tm, tm)],
        out_specs=[row_spec(d), row_spec(LANES), row_spec(LANES), fixed(1, LANES)],
        out_shape=[jax.ShapeDtypeStruct((n, d), F32), jax.ShapeDtypeStruct((n, LANES), jnp.int32),
                   jax.ShapeDtypeStruct((n, LANES), F32), jax.ShapeDtypeStruct((1, LANES), jnp.int32)],
        scratch_shapes=[pltpu.VMEM((1, LANES), F32)],
        compiler_params=_cparams(("arbitrary",)),
        name="router",
    )(x1, n2.reshape(1, d), sc, sh, w_route, b_route,
      jnp.asarray(np.tril(np.ones((tm, tm), np.float32), -1), BF16))


def _dispatch_kernel(pos_ref, pad_ref, h_ref, hs_ref, xs_hbm, zbuf, sem, psem, *, tm, ne):
    i = pl.program_id(0)
    last = i == pl.num_programs(0) - 1

    def scatter(ref, first_token, count):
        def body(r, c):
            t = first_token + r
            src = ref.at[pl.ds(r, 1)]
            pltpu.make_async_copy(src, xs_hbm.at[pl.ds(pos_ref[2 * t], 1)], sem).start()
            pltpu.make_async_copy(src, xs_hbm.at[pl.ds(pos_ref[2 * t + 1], 1)], sem).start()
            return c

        lax.fori_loop(0, count, body, 0, unroll=8)

    def drain(ref):
        for _ in range(2):
            pltpu.make_async_copy(ref, xs_hbm.at[pl.ds(0, ref.shape[0])], sem).wait()

    scatter(h_ref, i * tm, tm)

    @pl.when(last)
    def _():
        scatter(hs_ref, pl.num_programs(0) * tm, hs_ref.shape[0])
        drain(hs_ref)
        zbuf[...] = jnp.zeros(zbuf.shape, F32)
        tile_rows = zbuf.shape[0]

        def pad_copy(e, r):
            return pltpu.make_async_copy(zbuf.at[pl.ds(0, 1)], xs_hbm.at[pl.ds(pad_ref[e] + r, 1)], psem)

        def tail_copy(t):
            return pltpu.make_async_copy(zbuf, xs_hbm.at[pl.ds(pl.multiple_of(t * tile_rows, tile_rows), tile_rows)], psem)

        def for_each_fill(fn):
            def per_expert(e, c):
                def per_row(r, cc):
                    fn(pad_copy(e, r))
                    return cc

                lax.fori_loop(0, pad_ref[ne + e], per_row, 0)
                return c

            def per_tile(t, c):
                fn(tail_copy(t))
                return c

            lax.fori_loop(0, ne, per_expert, 0)
            lax.fori_loop(pad_ref[2 * ne], xs_hbm.shape[0] // tile_rows, per_tile, 0)

        for_each_fill(lambda cp: cp.start())
        for_each_fill(lambda cp: cp.wait())

    drain(h_ref)


def _dispatch(pos, pad, h, h_small, n_rows, tile_rows, ne, tm=512):
    n, w = h.shape
    tm = min(tm, n)
    assert n % tm == 0 and n_rows % tile_rows == 0
    any_spec = pl.BlockSpec(memory_space=pl.ANY)
    return pl.pallas_call(
        functools.partial(_dispatch_kernel, tm=tm, ne=ne),
        grid_spec=pltpu.PrefetchScalarGridSpec(
            num_scalar_prefetch=2, grid=(n // tm,),
            in_specs=[pl.BlockSpec((tm, w), lambda i, p, q: (i, 0)),
                      pl.BlockSpec(h_small.shape, lambda i, p, q: (0, 0))],
            out_specs=any_spec,
            scratch_shapes=[pltpu.VMEM((tile_rows, w), F32), pltpu.SemaphoreType.DMA(()),
                            pltpu.SemaphoreType.DMA(())]),
        out_shape=jax.ShapeDtypeStruct((n_rows, w), F32),
        compiler_params=_cparams(("arbitrary",)),
        name="moe_dispatch",
    )(pos, pad, h, h_small)


def _expert_kernel(te_ref, nv_ref, x_ref, wg_ref, wu_ref, wd_ref, y_ref):
    del te_ref
    i = pl.program_id(0)

    @pl.when(i < nv_ref[0])
    def _():
        x = x_ref[...].astype(BF16)
        a = _dot(x, wg_ref[...].astype(BF16))
        u = _dot(x, wu_ref[...].astype(BF16))
        hid = a * _sigmoid(a) * u
        y_ref[...] = _dot(hid.astype(BF16), wd_ref[...].astype(BF16))

    @pl.when(i >= nv_ref[0])
    def _():
        y_ref[...] = jnp.zeros(y_ref.shape, F32)


def _experts(tile_expert, n_valid, xs, w_gate, w_up, w_down, tm):
    rows, half = xs.shape
    ne, d, f = w_gate.shape
    w_spec = lambda a, b: pl.BlockSpec((None, a, b), lambda i, te, nv: (te[i], 0, 0))
    return pl.pallas_call(
        _expert_kernel,
        grid_spec=pltpu.PrefetchScalarGridSpec(
            num_scalar_prefetch=2, grid=(rows // tm,),
            in_specs=[pl.BlockSpec((tm, half), lambda i, te, nv: (jnp.minimum(i, nv[0] - 1), 0)),
                      w_spec(d, f), w_spec(d, f), w_spec(f, d)],
            out_specs=pl.BlockSpec((tm, d), lambda i, te, nv: (i, 0))),
        out_shape=jax.ShapeDtypeStruct((rows, d), F32),
        compiler_params=_cparams(("arbitrary",)),
        name="moe_experts",
    )(tile_expert, n_valid, xs, w_gate, w_up, w_down)


def _combine_kernel(pos_ref, ys_hbm, x_ref, ga_ref, rw_ref, o_ref, buf, sem, *, tm):
    i = pl.program_id(0)
    n = pl.num_programs(0)

    def issue(tile, slot):
        def body(r, c):
            t = tile * tm + r
            pltpu.make_async_copy(ys_hbm.at[pl.ds(pos_ref[2 * t], 1)], buf.at[slot, pl.ds(r, 1)],
                                  sem.at[slot]).start()
            pltpu.make_async_copy(ys_hbm.at[pl.ds(pos_ref[2 * t + 1], 1)], buf.at[slot, pl.ds(tm + r, 1)],
                                  sem.at[slot]).start()
            return c

        lax.fori_loop(0, tm, body, 0, unroll=8)

    @pl.when(i == 0)
    def _():
        issue(0, 0)

    @pl.when(i + 1 < n)
    def _():
        issue(i + 1, (i + 1) % 2)

    slot = i % 2
    pltpu.make_async_copy(ys_hbm.at[pl.ds(0, 2 * tm)], buf.at[slot], sem.at[slot]).wait()
    w = rw_ref[...]
    y = w[:, 0:1] * buf[slot, 0:tm] + w[:, 1:2] * buf[slot, tm:2 * tm]
    o_ref[...] = x_ref[...] + ga_ref[...] * y


def _combine(pos, ys, x1, ga, rw, tm, rows_per_mod):
    n, d = x1.shape
    r = ga.shape[1]
    tiles_per_mod = rows_per_mod // tm
    return pl.pallas_call(
        functools.partial(_combine_kernel, tm=tm),
        grid_spec=pltpu.PrefetchScalarGridSpec(
            num_scalar_prefetch=1, grid=(n // tm,),
            in_specs=[pl.BlockSpec(memory_space=pl.ANY),
                      pl.BlockSpec((tm, d), lambda i, p: (i, 0)),
                      pl.BlockSpec((None, r, d), lambda i, p: (i // tiles_per_mod, 0, 0)),
                      pl.BlockSpec((tm, LANES), lambda i, p: (i, 0))],
            out_specs=pl.BlockSpec((tm, d), lambda i, p: (i, 0)),
            scratch_shapes=[pltpu.VMEM((2, 2 * tm, d), F32), pltpu.SemaphoreType.DMA((2,))]),
        out_shape=jax.ShapeDtypeStruct((n, d), F32),
        compiler_params=_cparams(("arbitrary",)),
        name="moe_combine",
    )(pos, ys, x1, ga, rw)


def _grouped_moe(big, small, w_gate, w_up, w_down, tm_e=256, tm_c=256):
    ne = w_gate.shape[0]
    n_all = big["h"].shape[0] + small["h"].shape[0]
    tm_e = min(tm_e, big["h"].shape[0])
    cnt_big = big["cnt"][0, ROUTE_LANE0:ROUTE_LANE0 + ne]
    counts = cnt_big + small["cnt"][0, ROUTE_LANE0:ROUTE_LANE0 + ne]
    tiles = (counts + tm_e - 1) // tm_e
    tile_end = jnp.cumsum(tiles)
    start_row = (tile_end - tiles) * tm_e
    n_tiles = (2 * n_all + tm_e - 1) // tm_e + ne
    tile_expert = jnp.minimum(jnp.sum(tile_end[None, :] <= jnp.arange(n_tiles)[:, None], axis=1), ne - 1)

    def positions(stream, first_rank):
        e, rank = stream["idx"][:, 0:2], stream["idx"][:, 2:4]
        return (jnp.take(start_row + first_rank, e, axis=0) + rank).reshape(-1).astype(jnp.int32)

    pos_big, pos_small = positions(big, 0), positions(small, cnt_big)
    pad = jnp.concatenate([start_row + counts, tiles * tm_e - counts, tile_end[ne - 1:]]).astype(jnp.int32)
    xs = _dispatch(jnp.concatenate([pos_big, pos_small]), pad, big["h"], small["h"], n_tiles * tm_e, tm_e, ne)
    ys = _experts(tile_expert.astype(jnp.int32), tile_end[ne - 1:].astype(jnp.int32), xs, w_gate, w_up, w_down, tm_e)
    return [_combine(pos, ys, s["x1"], s["ga"], s["rw"], min(tm_c, s["h"].shape[0]), s["rows_per_mod"])
            for s, pos in ((big, pos_big), (small, pos_small))]


def _pick_tile(n, pref):
    t = min(pref, n)
    while n % t:
        t //= 2
    return t


def _layer(x, mods, s0, attend, lw, lam_init, hg_chunk, hg_heads_per_step, per_row_mod):
    (n1, n2, w_in, gq, gk, subln, lam4, hlb, hg_g, w_out, w_route, b_route, w_gate, w_up, w_down) = lw
    b, t, d = x.shape
    n = b * t
    x2d = x.reshape(n, d)
    if per_row_mod:
        tm = n
        rows_per_mod = n
        mods = [jnp.repeat(m, t, axis=0).reshape(1, n, d) for m in mods]
    else:
        tm = _pick_tile(t, 512)
        rows_per_mod = t
        mods = [m.reshape(b, 1, d) for m in mods]
    sh1, sc1, ga1, sh2, sc2, ga2 = mods

    tm_in = tm if per_row_mod else _pick_tile(t, 1024)
    q, k, v, hmain, hk = _in_proj(x2d, n1, sc1, sh1, w_in, gq, gk, hlb, tm_in, rows_per_mod)
    a = attend(q.reshape(b, t, -1), k.reshape(b, t, -1), v.reshape(b, t, -1))
    tpad = -(-t // hg_chunk) * hg_chunk
    hm3, hk3 = hmain.reshape(b, t, -1), hk.reshape(b, t, -1)
    if tpad != t:
        hm3 = jnp.pad(hm3, ((0, 0), (0, tpad - t), (0, 0)))
        hk3 = jnp.pad(hk3, ((0, 0), (0, tpad - t), (0, 0)))
    m, s_new = _hgrn2(hm3, hk3, hg_g, s0, hg_chunk, _pick_tile(tpad, 2 * hg_chunk), hg_heads_per_step)
    m = m[:, :t]
    x1 = _out_proj(a.reshape(n, -1), m.reshape(n, -1), w_out, x2d, ga1, tm, rows_per_mod)
    hp, idx, rw, cnt = _router(x1, n2, sc2, sh2, w_route, b_route, tm, rows_per_mod)
    routed = dict(h=hp, idx=idx, rw=rw, cnt=cnt, x1=x1, ga=ga2, rows_per_mod=rows_per_mod)
    return routed, k, v, s_new


def kernel(x_prompt, x_sample, cache_k, cache_v, state_hgrn, page_table, c_prompt, c_sample, norm1_g, norm2_g, w_ada, b_ada, w_in, q_norm_g, k_norm_g, lambda_q1, lambda_k1, lambda_q2, lambda_k2, subln_g, hg_lower_bound, hg_norm_g, w_out, w_router_group, b_router_group, w_router_expert, b_router_expert, w_exp_gate, w_exp_up, w_exp_down):
    depth = norm1_g.shape[0]
    assert depth == 1, "single-layer trunk"
    l = 0
    lam_init = 0.8 - 0.6 * math.exp(-0.3 * l)
    bp, tp, d = x_prompt.shape
    bs, ts, _ = x_sample.shape

    c_all = jnp.concatenate([c_prompt, c_sample], axis=0)
    rpad = -(-c_all.shape[0] // 8) * 8
    c_all = jnp.pad(c_all, ((0, rpad - c_all.shape[0]), (0, 0)))
    mod = _adaln(c_all, w_ada[l], b_ada[l])
    mods_p = [mod[:bp, i * d:(i + 1) * d] for i in range(6)]
    mods_s = [mod[bp:bp + bs, i * d:(i + 1) * d] for i in range(6)]

    w_re = jnp.transpose(w_router_expert[l], (1, 0, 2)).reshape(d, N_EXPERTS)
    w_route = jnp.pad(jnp.concatenate([w_router_group[l], w_re], axis=1), ((0, 0), (0, LANES - N_GROUPS - N_EXPERTS)))
    b_route = jnp.pad(jnp.concatenate([b_router_group[l], b_router_expert[l].reshape(-1)]),
                      (0, LANES - N_GROUPS - N_EXPERTS)).reshape(1, LANES)
    lam4 = jnp.stack([lambda_q1[l], lambda_k1[l], lambda_q2[l], lambda_k2[l]], axis=0)

    lw = (norm1_g[l], norm2_g[l], w_in[l], q_norm_g[l], k_norm_g[l], subln_g[l], lam4, hg_lower_bound,
          hg_norm_g[l], w_out[l], w_route, b_route, w_exp_gate[l], w_exp_up[l], w_exp_down[l])

    def attend_prompt(q, k, v):
        return _prompt_attention(q, k, v, lam4, subln_g[l], lam_init)

    def attend_sample(q, k, v):
        pad = lambda a: jnp.pad(a, ((0, 0), (0, -ts % 8), (0, 0)))
        o = _sample_attention(pad(q), pad(k), pad(v), ts, cache_k[l], cache_v[l], page_table, lam4,
                              subln_g[l], lam_init)
        return o[:, :ts]

    routed_p, kp, vp, sp = _layer(x_prompt, mods_p, None, attend_prompt, lw, lam_init, 128, HG_HEADS, False)
    routed_s, kn, vn, sn = _layer(x_sample, mods_s, state_hgrn[l], attend_sample, lw, lam_init, 8, HG_HEADS, True)
    yp, ys = _grouped_moe(routed_p, routed_s, w_exp_gate[l], w_exp_up[l], w_exp_down[l])

    kv_shape = lambda b, t: (1, b, t, ATTN_KV_HEADS, 2 * QK_DIM)
    return (yp.reshape(x_prompt.shape), ys.reshape(x_sample.shape),
            kp.reshape(kv_shape(bp, tp)), vp.reshape(kv_shape(bp, tp)),
            kn.reshape(kv_shape(bs, ts)), vn.reshape(kv_shape(bs, ts)),
            sp[None], sn[None])
```

```python
import functools
import math

import jax
import jax.numpy as jnp
import numpy as np
from jax import lax
from jax.experimental import pallas as pl
from jax.experimental.pallas import tpu as pltpu

F32 = jnp.float32
BF16 = jnp.bfloat16

QK_DIM = 64
V_DIM = 128
ATTN_KV_HEADS = 4
ATTN_GROUP = 2
ATTN_HEADS = ATTN_KV_HEADS * ATTN_GROUP
HG_HEADS = 8
HG_DIM = 128
N_GROUPS = 4
EXPERTS_PER_GROUP = 4
N_EXPERTS = N_GROUPS * EXPERTS_PER_GROUP
NORM_EPS = 1e-6
ALIBI_MAX_BIAS = 8.0
LANES = 128
MXU_DIM = 256
VMEM_LIMIT = 60 * 1024 * 1024
ROUTE_LANE0 = N_GROUPS
HG_TABLE_LEVELS = 2


def _cparams(sem):
    return pltpu.CompilerParams(dimension_semantics=sem, vmem_limit_bytes=VMEM_LIMIT)


def _dot(a, b):
    return jnp.dot(a, b, preferred_element_type=F32)


def _dot_nt(a, b):
    return lax.dot_general(a, b, (((1,), (1,)), ((), ())), preferred_element_type=F32)


def _dot_tn(a, b):
    return lax.dot_general(a, b, (((0,), (0,)), ((), ())), preferred_element_type=F32)


def _split3(x):
    hi = x.astype(BF16)
    r = x - hi.astype(F32)
    mid = r.astype(BF16)
    lo = (r - mid.astype(F32)).astype(BF16)
    return hi, mid, lo


def _sigmoid(x):
    return 1.0 / (1.0 + jnp.exp(-x))


def _rms(x, g):
    return x * lax.rsqrt(jnp.mean(x * x, axis=-1, keepdims=True) + NORM_EPS) * g


def _ada_kernel(c_ref, w_ref, b_ref, o_ref):
    c = c_ref[...]
    s = c * _sigmoid(c)
    o_ref[...] = _dot(s.astype(BF16), w_ref[...].astype(BF16)) + b_ref[...]


def _adaln(c_all, w_ada, b_ada, tn=1024):
    r, d = c_all.shape
    n = w_ada.shape[1]
    return pl.pallas_call(
        _ada_kernel,
        grid=(n // tn,),
        in_specs=[
            pl.BlockSpec((r, d), lambda j: (0, 0)),
            pl.BlockSpec((d, tn), lambda j: (0, j)),
            pl.BlockSpec((1, tn), lambda j: (0, j)),
        ],
        out_specs=pl.BlockSpec((r, tn), lambda j: (0, j)),
        out_shape=jax.ShapeDtypeStruct((r, n), F32),
        compiler_params=_cparams(("arbitrary",)),
        name="adaln",
    )(c_all, w_ada, b_ada.reshape(1, n))


IN_TN = 512


def _group_sumsq(p, bd):
    x2 = p * p
    hi = x2.astype(BF16)
    lo = (x2 - hi.astype(F32)).astype(BF16)
    outs = []
    for c in range(p.shape[1] // MXU_DIM):
        sl = slice(c * MXU_DIM, (c + 1) * MXU_DIM)
        outs.append(_dot(hi[:, sl], bd) + _dot(lo[:, sl], bd))
    return jnp.concatenate(outs, axis=1)


def _in_epilogue(c, p, cs, gq_ref, gk_ref, lb_ref, bd_ref, q_ref, k_ref, v_ref, hm_ref, hk_ref):
    def qk_norm(g):
        ms = _group_sumsq(p, bd_ref[...]) * (1.0 / QK_DIM)
        return p * lax.rsqrt(ms + NORM_EPS) * g

    if c < 2:
        q_ref[:, cs] = (qk_norm(gq_ref[:, cs]) * (QK_DIM ** -0.5)).astype(BF16)
    elif c == 2:
        k_ref[:, cs] = qk_norm(gk_ref[:, cs])
    elif c == 3:
        v_ref[:, cs] = p
    elif c in (4, 5, 10, 11):
        hm_ref[:, cs] = p * _sigmoid(p)
    elif c in (6, 7):
        a = lb_ref[:, cs]
        e = jnp.exp(a - jnp.max(a, axis=0, keepdims=True))
        lb = e[0:1] / jnp.sum(e, axis=0, keepdims=True)
        sg = _sigmoid(p)
        hm_ref[:, cs] = jnp.log(lb + (1.0 - lb) * sg)
        hk_ref[:, cs] = (1.0 - lb) * (1.0 - sg)
    else:
        hm_ref[:, cs] = p


def _in_kernel(x_ref, n1_ref, sc_ref, sh_ref, w_ref, gq_ref, gk_ref, lb_ref, bd_ref,
               q_ref, k_ref, v_ref, hm_ref, hk_ref, h_scr, *, nj):
    j = pl.program_id(1)

    @pl.when(j == 0)
    def _():
        y = _rms(x_ref[...], n1_ref[...])
        h_scr[...] = (y * (1.0 + sc_ref[...]) + sh_ref[...]).astype(BF16)

    outs = (gq_ref, gk_ref, lb_ref, bd_ref, q_ref, k_ref, v_ref, hm_ref, hk_ref)
    for cols in ((0, 1), (2,), (3,), (4, 5, 10, 11), (6, 7), (8, 9)):
        cond = functools.reduce(lambda a, b: a | b, [j == c for c in cols])
        width = IN_TN if cols[0] <= 2 else MXU_DIM

        @pl.when(cond)
        def _(c=cols[0], width=width):
            for u in range(IN_TN // width):
                cs = slice(u * width, (u + 1) * width)
                p = _dot(h_scr[...], w_ref[:, cs].astype(BF16))
                _in_epilogue(c, p, cs, *outs)


def _in_proj(x2d, n1, sc, sh, w_in, gq, gk, hlb, tm, rows_per_mod):
    n, d = x2d.shape
    cols = w_in.shape[1]
    nj = cols // IN_TN
    assert nj == 12 and n % tm == 0
    r = sc.shape[1]
    tiles_per_mod = rows_per_mod // tm
    bd = jnp.asarray(np.kron(np.eye(MXU_DIM // QK_DIM), np.ones((QK_DIM, QK_DIM))), BF16)
    clamp = lambda j, lo: jnp.clip(j - lo, 0, 1)
    mod_spec = pl.BlockSpec((None, r, d), lambda i, j: (i // tiles_per_mod, 0, 0))
    return pl.pallas_call(
        functools.partial(_in_kernel, nj=nj),
        grid=(n // tm, nj),
        in_specs=[
            pl.BlockSpec((tm, d), lambda i, j: (i, 0)),
            pl.BlockSpec((1, d), lambda i, j: (0, 0)),
            mod_spec, mod_spec,
            pl.BlockSpec((d, IN_TN), lambda i, j: (0, j)),
            pl.BlockSpec((1, IN_TN), lambda i, j: (0, 0)),
            pl.BlockSpec((1, IN_TN), lambda i, j: (0, 0)),
            pl.BlockSpec((hlb.shape[0], IN_TN), lambda i, j: (0, clamp(j, 6))),
            pl.BlockSpec((MXU_DIM, MXU_DIM), lambda i, j: (0, 0)),
        ],
        out_specs=[
            pl.BlockSpec((tm, IN_TN), lambda i, j: (i, clamp(j, 0))),
            pl.BlockSpec((tm, IN_TN), lambda i, j: (i, 0)),
            pl.BlockSpec((tm, IN_TN), lambda i, j: (i, 0)),
            pl.BlockSpec((tm, IN_TN), lambda i, j: (i, jnp.clip(j - 4, 0, 7))),
            pl.BlockSpec((tm, IN_TN), lambda i, j: (i, clamp(j, 6))),
        ],
        out_shape=[
            jax.ShapeDtypeStruct((n, 2 * IN_TN), BF16),
            jax.ShapeDtypeStruct((n, IN_TN), F32),
            jax.ShapeDtypeStruct((n, IN_TN), F32),
            jax.ShapeDtypeStruct((n, 8 * IN_TN), F32),
            jax.ShapeDtypeStruct((n, 2 * IN_TN), F32),
        ],
        scratch_shapes=[pltpu.VMEM((tm, d), BF16)],
        compiler_params=_cparams(("arbitrary", "arbitrary")),
        name="in_proj",
    )(x2d, n1.reshape(1, d), sc, sh, w_in,
      jnp.tile(gq.reshape(1, QK_DIM), (1, IN_TN // QK_DIM)),
      jnp.tile(gk.reshape(1, QK_DIM), (1, IN_TN // QK_DIM)), hlb, bd)


def _alibi_slopes_np():
    h = np.arange(1, ATTN_HEADS + 1, dtype=np.float64)
    return np.exp2(-ALIBI_MAX_BIAS * h / ATTN_HEADS).reshape(ATTN_KV_HEADS, ATTN_GROUP).astype(np.float32)


def _alibi_slopes():
    out = np.zeros((ATTN_KV_HEADS, 1, LANES), np.float32)
    out[:, 0, :ATTN_GROUP] = _alibi_slopes_np()
    return jnp.asarray(out)


def _lambda(lam_ref, lam_init):
    l = lam_ref[...]
    s1 = jnp.sum(l[0:1] * l[1:2], axis=-1, keepdims=True)
    s2 = jnp.sum(l[2:3] * l[3:4], axis=-1, keepdims=True)
    return jnp.exp(s1) - jnp.exp(s2) + lam_init


def _diff_finish(o_all, t, lam, subln, lam_init, o_ref):
    for g in range(ATTN_GROUP):
        o = o_all[(2 * g) * t:(2 * g + 1) * t] - lam * o_all[(2 * g + 1) * t:(2 * g + 2) * t]
        y = _rms(o, subln) * (1.0 - lam_init)
        o_ref[:, g * LANES:(g + 1) * LANES] = y.astype(o_ref.dtype)


def _softmax_step_t(s, m_old, l_old):
    m_new = jnp.maximum(m_old, jnp.max(s, axis=0, keepdims=True))
    alpha = jnp.exp(m_old - m_new)
    p = jnp.exp(s - m_new)
    return m_new, alpha, alpha * l_old + jnp.sum(p, axis=0, keepdims=True), p.astype(BF16)


def _pattn_kernel(sl_ref, lam_ref, subln_ref, q_ref, k_ref, v_ref, o_ref, kb, vt, qs, s_a, s_b, m_scr, l_scr, acc,
                  *, tq, nblk, lam_init):
    qi = pl.program_id(2)

    @pl.when(qi == 0)
    def _():
        t = k_ref.shape[0]
        pos = lax.broadcasted_iota(jnp.int32, (t, LANES), 0)
        ln = lax.broadcasted_iota(jnp.int32, (t, LANES), 1)
        aug = jnp.where(ln == 0, pos & ~(LANES - 1), jnp.where(ln == 1, pos & (LANES - 1), 0))
        kb[:, :LANES] = k_ref[...].astype(BF16)
        kb[:, LANES:] = aug.astype(F32).astype(BF16)
        for jb in range(t // tq):
            vt[jb] = v_ref[jb * tq:(jb + 1) * tq, :].T.astype(BF16)

    rows = 4 * tq
    half = 2 * tq
    q = q_ref[...]
    lane = lax.broadcasted_iota(jnp.int32, (tq, LANES), 1)
    zero = jnp.zeros((tq, LANES), BF16)
    for g in range(ATTN_GROUP):
        qg = q[:, g * LANES:(g + 1) * LANES]
        aug = jnp.where(lane < 2, sl_ref[0:1, g:g + 1], 0.0).astype(BF16)
        qs[(2 * g) * tq:(2 * g + 1) * tq, :] = jnp.concatenate([jnp.where(lane < QK_DIM, qg, zero), aug], axis=1)
        qs[(2 * g + 1) * tq:(2 * g + 2) * tq, :] = jnp.concatenate([jnp.where(lane >= QK_DIM, qg, zero), aug], axis=1)
    t_q = lax.broadcasted_iota(jnp.int32, (1, tq), 1)
    m_scr[...] = jnp.full(m_scr.shape, -jnp.inf, F32)
    l_scr[...] = jnp.zeros(l_scr.shape, F32)
    acc[...] = jnp.zeros(acc.shape, F32)

    s_bufs = (s_a, s_b)
    groups = [slice(u * tq, (u + 1) * tq) for u in range(rows // tq)]

    def logits(jb, cs):
        s_bufs[jb % 2][:, cs] = _dot_nt(kb[jb * tq:(jb + 1) * tq, :], qs[cs, :])

    def attend(jb, cs, masked):
        s = s_bufs[jb % 2][:, cs]
        if masked:
            key = lax.broadcasted_iota(jnp.int32, (tq, 1), 0)
            s = jnp.where(key <= t_q, s, -jnp.inf)
        m_new, alpha, l_new, p = _softmax_step_t(s, m_scr[:, cs], l_scr[:, cs])
        acc[:, cs] = alpha * acc[:, cs] + _dot(vt[jb], p)
        m_scr[:, cs] = m_new
        l_scr[:, cs] = l_new

    for cs in groups:
        logits(0, cs)
    for jb in range(nblk):
        if jb + 1 < nblk:
            @pl.when(jb < qi)
            def _(jb=jb):
                for cs in groups:
                    logits(jb + 1, cs)
                    attend(jb, cs, False)

        @pl.when(jb == qi)
        def _(jb=jb):
            for cs in groups:
                attend(jb, cs, True)

    o_t = acc[...] * (1.0 / l_scr[...])
    lam = _lambda(lam_ref, lam_init)
    for g in range(ATTN_GROUP):
        d = o_t[:, (2 * g) * tq:(2 * g + 1) * tq] - lam * o_t[:, (2 * g + 1) * tq:(2 * g + 2) * tq]
        y = _rms(d.T, subln_ref[...]) * (1.0 - lam_init)
        o_ref[:, g * LANES:(g + 1) * LANES] = y.astype(o_ref.dtype)


def _prompt_attention(q, k, v, lam4, subln, lam_init, tq=512):
    b, t, _ = q.shape
    tq = min(tq, t)
    assert t % tq == 0 and t < LANES * 256
    rows = 4 * tq
    kern = functools.partial(_pattn_kernel, tq=tq, nblk=t // tq, lam_init=lam_init)
    return pl.pallas_call(
        kern,
        grid=(b, ATTN_KV_HEADS, t // tq),
        in_specs=[
            pl.BlockSpec((None, 1, LANES), lambda bi, h, qi: (h, 0, 0)),
            pl.BlockSpec((4, QK_DIM), lambda bi, h, qi: (0, 0)),
            pl.BlockSpec((1, V_DIM), lambda bi, h, qi: (0, 0)),
            pl.BlockSpec((None, tq, 2 * LANES), lambda bi, h, qi: (bi, qi, h)),
            pl.BlockSpec((None, t, LANES), lambda bi, h, qi: (bi, 0, h)),
            pl.BlockSpec((None, t, LANES), lambda bi, h, qi: (bi, 0, h)),
        ],
        out_specs=pl.BlockSpec((None, tq, 2 * LANES), lambda bi, h, qi: (bi, qi, h)),
        out_shape=jax.ShapeDtypeStruct((b, t, ATTN_HEADS * V_DIM), BF16),
        scratch_shapes=[
            pltpu.VMEM((t, 2 * LANES), BF16), pltpu.VMEM((t // tq, V_DIM, tq), BF16),
            pltpu.VMEM((rows, 2 * LANES), BF16), pltpu.VMEM((tq, rows), F32), pltpu.VMEM((tq, rows), F32),
            pltpu.VMEM((1, rows), F32), pltpu.VMEM((1, rows), F32), pltpu.VMEM((V_DIM, rows), F32),
        ],
        compiler_params=_cparams(("arbitrary", "arbitrary", "arbitrary")),
        name="prompt_attention",
    )(_alibi_slopes(), lam4, subln.reshape(1, V_DIM), q, k, v)


def _to_col(row):
    n = row.shape[1]
    eye = lax.broadcasted_iota(jnp.int32, (n, n), 0) == lax.broadcasted_iota(jnp.int32, (n, n), 1)
    return jnp.sum(jnp.where(eye, row, 0.0), axis=1, keepdims=True)


def _sattn_kernel(pt_ref, srow_ref, lam_ref, subln_ref, q_ref, kn_ref, vn_ref, *rest,
                  npg, nj, page, t, t_real, lam_init):
    del pt_ref
    kp_refs, vp_refs = rest[:npg], rest[npg:2 * npg]
    o_ref = rest[2 * npg]
    kc, vc, qf_scr, qbd_scr, m_scr, l_scr, acc = rest[2 * npg + 1:]
    j = pl.program_id(1)
    ntok = npg * page
    past = nj * ntok
    nq = qbd_scr.shape[0]
    width = ATTN_KV_HEADS * LANES
    srow = srow_ref[...]

    @pl.when(j == 0)
    def _():
        m_scr[...] = jnp.full(m_scr.shape, -jnp.inf, F32)
        l_scr[...] = jnp.zeros(l_scr.shape, F32)
        acc[...] = jnp.zeros(acc.shape, F32)
        qf = q_ref[...].astype(F32)
        lane = lax.broadcasted_iota(jnp.int32, (t, LANES), 1)
        qf_scr[...] = jnp.zeros(qf_scr.shape, F32)
        for hg in range(ATTN_HEADS):
            h = hg // ATTN_GROUP
            qh = qf[:, hg * LANES:(hg + 1) * LANES]
            qf_scr[(2 * hg) * t:(2 * hg + 1) * t, h * LANES:(h + 1) * LANES] = jnp.where(lane < QK_DIM, qh, 0.0)
            qf_scr[(2 * hg + 1) * t:(2 * hg + 2) * t, h * LANES:(h + 1) * LANES] = jnp.where(lane >= QK_DIM, qh, 0.0)
        qbd_scr[...] = qf_scr[...].astype(BF16)

    qbd = qbd_scr[...]

    for p in range(npg):
        kc[p * page:(p + 1) * page, :] = kp_refs[p][...].reshape(page, width).astype(BF16)
    kpos = lax.broadcasted_iota(jnp.int32, (ntok, nq), 0) + j * ntok
    hw = ntok // 2
    s = jnp.concatenate([_dot_nt(kc[:hw, :], qbd), _dot_nt(kc[hw:, :], qbd)], axis=0)
    s = s + kpos.astype(F32) * srow
    for p in range(npg):
        vc[p * page:(p + 1) * page, :] = vp_refs[p][...].reshape(page, width).astype(BF16)
    m_new, alpha, l_new, p = _softmax_step_t(s, m_scr[...], l_scr[...])
    pv = _dot_tn(p[:hw], vc[:hw, :]) + _dot_tn(p[hw:], vc[hw:, :])
    acc[...] = _to_col(alpha) * acc[...] + pv
    m_scr[...] = m_new
    l_scr[...] = l_new

    @pl.when(j == nj - 1)
    def _():
        t_q = lax.broadcasted_iota(jnp.int32, (1, nq), 1) % t
        c = lax.broadcasted_iota(jnp.int32, (t, 1), 0)
        sn = _dot_nt(kn_ref[...].astype(BF16), qbd) + (past + c).astype(F32) * srow
        sn = jnp.where((c <= t_q) & (c < t_real), sn, -jnp.inf)
        m_fin, alpha_n, l_fin, _ = _softmax_step_t(sn, m_scr[...], l_scr[...])
        pn = jnp.exp(sn - m_fin)
        o = _to_col(alpha_n) * acc[...]
        vn = vn_ref[...]
        for u in range(t_real):
            o = o + _to_col(pn[u:u + 1]) * vn[u:u + 1]
        o = o * _to_col(1.0 / l_fin)
        lam = _lambda(lam_ref, lam_init)
        rq = nq // ATTN_KV_HEADS
        for h in range(ATTN_KV_HEADS):
            o_h = o[h * rq:(h + 1) * rq, h * LANES:(h + 1) * LANES]
            _diff_finish(o_h, t, lam, subln_ref[...], lam_init, o_ref.at[:, 2 * h * LANES:(2 * h + 2) * LANES])


def _sample_attention(q, k_new, v_new, t_real, cache_k, cache_v, page_table, lam4, subln, lam_init, npg=32):
    b, t, _ = q.shape
    tpad = t
    page = cache_k.shape[1]
    n_pages = page_table.shape[1]
    npg = min(npg, n_pages)
    assert n_pages % npg == 0
    ntok = npg * page
    width = ATTN_KV_HEADS * LANES
    nq = ATTN_KV_HEADS * ATTN_GROUP * 2 * t
    nj = n_pages // npg
    kern = functools.partial(_sattn_kernel, npg=npg, nj=nj, page=page, t=t, t_real=t_real, lam_init=lam_init)
    slope_row = jnp.asarray(np.repeat(_alibi_slopes_np().reshape(-1), 2 * t)[None, :])

    def page_spec(p):
        return pl.BlockSpec((None, page, ATTN_KV_HEADS, LANES),
                            lambda bi, j, pt: (pt[bi * n_pages + j * npg + p], 0, 0, 0))

    grid_spec = pltpu.PrefetchScalarGridSpec(
        num_scalar_prefetch=1,
        grid=(b, nj),
        in_specs=[
            pl.BlockSpec((1, nq), lambda bi, j, pt: (0, 0)),
            pl.BlockSpec((4, QK_DIM), lambda bi, j, pt: (0, 0)),
            pl.BlockSpec((1, V_DIM), lambda bi, j, pt: (0, 0)),
            pl.BlockSpec((None, t, ATTN_HEADS * V_DIM), lambda bi, j, pt: (bi, 0, 0)),
            pl.BlockSpec((None, tpad, width), lambda bi, j, pt: (bi, 0, 0)),
            pl.BlockSpec((None, tpad, width), lambda bi, j, pt: (bi, 0, 0)),
        ] + [page_spec(p) for p in range(npg)] * 2,
        out_specs=pl.BlockSpec((None, t, ATTN_HEADS * V_DIM), lambda bi, j, pt: (bi, 0, 0)),
        scratch_shapes=[
            pltpu.VMEM((ntok, width), BF16), pltpu.VMEM((ntok, width), BF16),
            pltpu.VMEM((nq, width), F32), pltpu.VMEM((nq, width), BF16),
            pltpu.VMEM((1, nq), F32), pltpu.VMEM((1, nq), F32), pltpu.VMEM((nq, width), F32),
        ],
    )
    return pl.pallas_call(
        kern,
        grid_spec=grid_spec,
        out_shape=jax.ShapeDtypeStruct((b, t, ATTN_HEADS * V_DIM), BF16),
        compiler_params=_cparams(("arbitrary", "arbitrary")),
        name="sample_attention",
    )(page_table.reshape(-1), slope_row, lam4, subln.reshape(1, V_DIM), q, k_new, v_new,
      *([cache_k] * npg), *([cache_v] * npg))


def _hgrn_tables(c):
    levels = int(math.log2(c))
    tril = np.tril(np.ones((c, c), np.float32))
    w = [tril]
    mask = [np.eye(c, dtype=np.float32)]
    idx = np.arange(c)
    for l in range(levels):
        bs = 2 << l
        mid = (idx // bs) * bs + bs // 2
        if l < HG_TABLE_LEVELS:
            w.append(tril - tril[mid])
        upper = (idx % bs) >= bs // 2
        same = (idx[:, None] // bs) == (idx[None, :] // bs)
        mask.append((same & upper[:, None] & ~upper[None, :]).astype(np.float32))
    return jnp.asarray(np.concatenate(w, 0), BF16), jnp.asarray(np.stack(mask, 0)), levels


def _minus_mid_rows(g, bs):
    c, n = g.shape
    g3 = g.reshape(c // bs, bs, n)
    return (g3 - g3[:, bs // 2:bs // 2 + 1, :]).reshape(c, n)


def _hgrn_kernel(*refs, c, levels, nchunk, hps, has_s0):
    if has_s0:
        s0_ref, refs = refs[0], refs[1:]
    (w_ref, mask_ref, g_ref, q_ref, lf_ref, v_ref, gate_ref, k_ref, o_ref, sn_ref, st) = refs
    ci = pl.program_id(2)

    @pl.when(ci == 0)
    def _():
        for hh in range(hps):
            st[hh] = s0_ref[hh].T if has_s0 else jnp.zeros((HG_DIM, HG_DIM), F32)

    w = w_ref[...]
    pair = 2 * HG_DIM
    heads = range(hps)
    hsl = [slice(hh * HG_DIM, (hh + 1) * HG_DIM) for hh in heads]
    for n in range(nchunk):
        rs = slice(n * c, (n + 1) * c)
        gx = []
        for hh in range(0, hps, 2):
            l_hi, l_mid, l_lo = _split3(lf_ref[rs, hh * HG_DIM:hh * HG_DIM + pair])
            gx2 = _dot(w, l_hi) + _dot(w, l_mid) + _dot(w, l_lo)
            gx += [gx2[:, :HG_DIM], gx2[:, HG_DIM:]]
        qs = [q_ref[rs, hsl[hh]] for hh in heads]
        kk = [k_ref[rs, hsl[hh]] for hh in heads]
        g = [gx[hh][0:c] for hh in heads]
        a = [mask_ref[0] * _dot_nt(qs[hh].astype(BF16), kk[hh].astype(BF16)) for hh in heads]
        for l in range(levels):
            for hh in heads:
                x = gx[hh][(l + 1) * c:(l + 2) * c] if l < HG_TABLE_LEVELS else _minus_mid_rows(g[hh], 2 << l)
                e = jnp.exp(-jnp.abs(x))
                qa = (qs[hh] * e).astype(BF16)
                kb = (kk[hh] * e).astype(BF16)
                a[hh] = a[hh] + mask_ref[l + 1] * _dot_nt(qa, kb)
        vb = [v_ref[rs, hsl[hh]].astype(BF16) for hh in heads]
        st_old = [st[hh] for hh in heads]
        o = [_dot_nt((qs[hh] * jnp.exp(g[hh])).astype(BF16), st_old[hh].astype(BF16))
             + _dot(a[hh].astype(BF16), vb[hh]) for hh in heads]
        for hh in heads:
            g_end = g[hh][c - 1:c]
            kd = kk[hh] * jnp.exp(g_end - g[hh])
            st[hh] = st_old[hh] * jnp.exp(g_end) + _dot_tn(vb[hh], kd.astype(BF16))
        for hh in heads:
            o_ref[rs, hsl[hh]] = (_rms(o[hh], g_ref[...]) * gate_ref[rs, hsl[hh]]).astype(o_ref.dtype)

    @pl.when(ci == pl.num_programs(2) - 1)
    def _():
        for hh in range(hps):
            sn_ref[hh] = st[hh].T


def _hgrn2(hmain, hk, hg_norm_g, s0, c, tc, hps):
    b, t, _ = hmain.shape
    w, mask, levels = _hgrn_tables(c)
    nl = levels + 1
    has_s0 = s0 is not None
    kern = functools.partial(_hgrn_kernel, c=c, levels=levels, nchunk=tc // c, hps=hps, has_s0=has_s0)
    nhg = HG_HEADS // hps
    col = lambda off: pl.BlockSpec((None, tc, hps * HG_DIM), lambda bi, h, ci: (bi, ci, off * nhg + h))
    state_spec = pl.BlockSpec((None, hps, HG_DIM, HG_DIM), lambda bi, h, ci: (bi, h, 0, 0))
    in_specs = [
        pl.BlockSpec(w.shape, lambda bi, h, ci: (0, 0)),
        pl.BlockSpec((nl, c, c), lambda bi, h, ci: (0, 0, 0)),
        pl.BlockSpec((1, HG_DIM), lambda bi, h, ci: (0, 0)),
        col(0), col(1), col(2), col(3), col(0),
    ]
    args = [w, mask, hg_norm_g.reshape(1, HG_DIM), hmain, hmain, hmain, hmain, hk]
    if has_s0:
        in_specs, args = [state_spec] + in_specs, [s0] + args
    return pl.pallas_call(
        kern,
        grid=(b, nhg, t // tc),
        in_specs=in_specs,
        out_specs=[col(0), state_spec],
        out_shape=[jax.ShapeDtypeStruct((b, t, HG_HEADS * HG_DIM), BF16),
                   jax.ShapeDtypeStruct((b, HG_HEADS, HG_DIM, HG_DIM), F32)],
        scratch_shapes=[pltpu.VMEM((hps, HG_DIM, HG_DIM), F32)],
        compiler_params=_cparams(("arbitrary", "arbitrary", "arbitrary")),
        name="hgrn2",
    )(*args)


def _out_kernel(a_ref, m_ref, w_ref, x_ref, ga_ref, o_ref):
    half = a_ref.shape[1]
    mix = _dot(a_ref[...], w_ref[:half, :]) + _dot(m_ref[...], w_ref[half:, :])
    o_ref[...] = x_ref[...] + ga_ref[...] * mix


def _out_proj(a2d, m2d, w_out, x2d, ga, tm, rows_per_mod):
    n, d = x2d.shape
    half = a2d.shape[1]
    r = ga.shape[1]
    tiles_per_mod = rows_per_mod // tm
    return pl.pallas_call(
        _out_kernel,
        grid=(n // tm,),
        in_specs=[
            pl.BlockSpec((tm, half), lambda i: (i, 0)),
            pl.BlockSpec((tm, half), lambda i: (i, 0)),
            pl.BlockSpec((2 * half, d), lambda i: (0, 0)),
            pl.BlockSpec((tm, d), lambda i: (i, 0)),
            pl.BlockSpec((None, r, d), lambda i: (i // tiles_per_mod, 0, 0)),
        ],
        out_specs=pl.BlockSpec((tm, d), lambda i: (i, 0)),
        out_shape=jax.ShapeDtypeStruct((n, d), F32),
        compiler_params=_cparams(("arbitrary",)),
        name="out_proj",
    )(a2d, m2d, w_out.astype(BF16), x2d, ga)


def _router_kernel(x_ref, n2_ref, sc_ref, sh_ref, wr_ref, br_ref, tril_ref, h_ref, idx_ref, rw_ref, cnt_ref, cnt_scr):
    h = _rms(x_ref[...], n2_ref[...]) * (1.0 + sc_ref[...]) + sh_ref[...]
    h_ref[...] = h
    h_hi = h.astype(BF16)
    h_lo = (h - h_hi.astype(F32)).astype(BF16)
    w = wr_ref[...]
    w_hi = w.astype(BF16)
    w_lo = (w - w_hi.astype(F32)).astype(BF16)
    parts = []
    for rs in (slice(0, h.shape[0] // 2), slice(h.shape[0] // 2, h.shape[0])):
        parts.append(_dot(h_hi[rs], w_hi) + _dot(h_hi[rs], w_lo) + _dot(h_lo[rs], w_hi))
    lg = jnp.concatenate(parts, axis=0) + br_ref[...]

    lane = lax.broadcasted_iota(jnp.int32, lg.shape, 1)
    neg = -jnp.inf
    big = jnp.int32(LANES)

    def top(valid):
        m = jnp.max(jnp.where(valid, lg, neg), axis=-1, keepdims=True)
        idx = jnp.min(jnp.where(valid & (lg == m), lane, big), axis=-1, keepdims=True)
        return m, idx

    is_group = lane < N_GROUPS
    gmax, gidx = top(is_group)
    pg_top = 1.0 / jnp.sum(jnp.where(is_group, jnp.exp(lg - gmax), 0.0), axis=-1, keepdims=True)
    lo = ROUTE_LANE0 + EXPERTS_PER_GROUP * gidx
    in_group = (lane >= lo) & (lane < lo + EXPERTS_PER_GROUP)
    l0, e0 = top(in_group)
    l1, e1 = top(in_group & (lane != e0))
    r = jnp.exp(l1 - l0)
    w0 = pg_top / (1.0 + r)
    w1 = w0 * r

    @pl.when(pl.program_id(0) == 0)
    def _():
        cnt_scr[...] = jnp.zeros(cnt_scr.shape, F32)

    is0, is1 = lane == e0, lane == e1
    onehot = jnp.where(is0 | is1, 1.0, 0.0)
    before = _dot(tril_ref[...], onehot.astype(BF16)) + cnt_scr[...]
    rank0 = jnp.sum(jnp.where(is0, before, 0.0), axis=-1, keepdims=True).astype(jnp.int32)
    rank1 = jnp.sum(jnp.where(is1, before, 0.0), axis=-1, keepdims=True).astype(jnp.int32)
    cnt_scr[...] = cnt_scr[...] + jnp.sum(onehot, axis=0, keepdims=True)
    idx_ref[...] = jnp.where(lane == 0, e0 - ROUTE_LANE0, jnp.where(lane == 1, e1 - ROUTE_LANE0,
                             jnp.where(lane == 2, rank0, jnp.where(lane == 3, rank1, 0))))
    rw_ref[...] = jnp.where(lane == 0, w0, jnp.where(lane == 1, w1, 0.0))
    cnt_ref[...] = cnt_scr[...].astype(jnp.int32)


def _router(x1, n2, sc, sh, w_route, b_route, tm, rows_per_mod):
    n, d = x1.shape
    r = sc.shape[1]
    tiles_per_mod = rows_per_mod // tm
    mod_spec = pl.BlockSpec((None, r, d), lambda i: (i // tiles_per_mod, 0, 0))
    row_spec = lambda w: pl.BlockSpec((tm, w), lambda i: (i, 0))
    fixed = lambda a, b: pl.BlockSpec((a, b), lambda i: (0, 0))
    return pl.pallas_call(
        _router_kernel,
        grid=(n // tm,),
        in_specs=[row_spec(d), fixed(1, d), mod_spec, mod_spec, fixed(d, LANES), fixed(1, LANES), fixed(tm, tm)],
        out_specs=[row_spec(d), row_spec(LANES), row_spec(LANES), fixed(1, LANES)],
        out_shape=[jax.ShapeDtypeStruct((n, d), F32), jax.ShapeDtypeStruct((n, LANES), jnp.int32),
                   jax.ShapeDtypeStruct((n, LANES), F32), jax.ShapeDtypeStruct((1, LANES), jnp.int32)],
        scratch_shapes=[pltpu.VMEM((1, LANES), F32)],
        compiler_params=_cparams(("arbitrary",)),
        name="router",
    )(x1, n2.reshape(1, d), sc, sh, w_route, b_route,
      jnp.asarray(np.tril(np.ones((tm, tm), np.float32), -1), BF16))


def _dispatch_kernel(pos_ref, pad_ref, h_ref, hs_ref, xs_hbm, zbuf, sem, psem, *, tm, ne):
    i = pl.program_id(0)
    last = i == pl.num_programs(0) - 1

    def scatter(ref, first_token, count):
        def body(r, c):
            t = first_token + r
            src = ref.at[pl.ds(r, 1)]
            pltpu.make_async_copy(src, xs_hbm.at[pl.ds(pos_ref[2 * t], 1)], sem).start()
            pltpu.make_async_copy(src, xs_hbm.at[pl.ds(pos_ref[2 * t + 1], 1)], sem).start()
            return c

        lax.fori_loop(0, count, body, 0, unroll=8)

    def drain(ref):
        for _ in range(2):
            pltpu.make_async_copy(ref, xs_hbm.at[pl.ds(0, ref.shape[0])], sem).wait()

    scatter(h_ref, i * tm, tm)

    @pl.when(last)
    def _():
        scatter(hs_ref, pl.num_programs(0) * tm, hs_ref.shape[0])
        drain(hs_ref)
        zbuf[...] = jnp.zeros(zbuf.shape, F32)
        tile_rows = zbuf.shape[0]

        def pad_copy(e, r):
            return pltpu.make_async_copy(zbuf.at[pl.ds(0, 1)], xs_hbm.at[pl.ds(pad_ref[e] + r, 1)], psem)

        def tail_copy(t):
            return pltpu.make_async_copy(zbuf, xs_hbm.at[pl.ds(pl.multiple_of(t * tile_rows, tile_rows), tile_rows)], psem)

        def for_each_fill(fn):
            def per_expert(e, c):
                def per_row(r, cc):
                    fn(pad_copy(e, r))
                    return cc

                lax.fori_loop(0, pad_ref[ne + e], per_row, 0)
                return c

            def per_tile(t, c):
                fn(tail_copy(t))
                return c

            lax.fori_loop(0, ne, per_expert, 0)
            lax.fori_loop(pad_ref[2 * ne], xs_hbm.shape[0] // tile_rows, per_tile, 0)

        for_each_fill(lambda cp: cp.start())
        for_each_fill(lambda cp: cp.wait())

    drain(h_ref)


def _dispatch(pos, pad, h, h_small, n_rows, tile_rows, ne, tm=1024):
    n, w = h.shape
    tm = min(tm, n)
    assert n % tm == 0 and n_rows % tile_rows == 0
    any_spec = pl.BlockSpec(memory_space=pl.ANY)
    return pl.pallas_call(
        functools.partial(_dispatch_kernel, tm=tm, ne=ne),
        grid_spec=pltpu.PrefetchScalarGridSpec(
            num_scalar_prefetch=2, grid=(n // tm,),
            in_specs=[pl.BlockSpec((tm, w), lambda i, p, q: (i, 0)),
                      pl.BlockSpec(h_small.shape, lambda i, p, q: (0, 0))],
            out_specs=any_spec,
            scratch_shapes=[pltpu.VMEM((tile_rows, w), F32), pltpu.SemaphoreType.DMA(()),
                            pltpu.SemaphoreType.DMA(())]),
        out_shape=jax.ShapeDtypeStruct((n_rows, w), F32),
        compiler_params=_cparams(("arbitrary",)),
        name="moe_dispatch",
    )(pos, pad, h, h_small)


def _expert_kernel(te_ref, nv_ref, x_ref, wg_ref, wu_ref, wd_ref, y_ref):
    del te_ref
    i = pl.program_id(0)

    @pl.when(i < nv_ref[0])
    def _():
        x = x_ref[...].astype(BF16)
        a = _dot(x, wg_ref[...].astype(BF16))
        u = _dot(x, wu_ref[...].astype(BF16))
        hid = a * _sigmoid(a) * u
        y_ref[...] = _dot(hid.astype(BF16), wd_ref[...].astype(BF16))

    @pl.when(i >= nv_ref[0])
    def _():
        y_ref[...] = jnp.zeros(y_ref.shape, F32)


def _experts(tile_expert, n_valid, xs, w_gate, w_up, w_down, tm):
    rows, half = xs.shape
    ne, d, f = w_gate.shape
    w_spec = lambda a, b: pl.BlockSpec((None, a, b), lambda i, te, nv: (te[i], 0, 0))
    return pl.pallas_call(
        _expert_kernel,
        grid_spec=pltpu.PrefetchScalarGridSpec(
            num_scalar_prefetch=2, grid=(rows // tm,),
            in_specs=[pl.BlockSpec((tm, half), lambda i, te, nv: (jnp.minimum(i, nv[0] - 1), 0)),
                      w_spec(d, f), w_spec(d, f), w_spec(f, d)],
            out_specs=pl.BlockSpec((tm, d), lambda i, te, nv: (i, 0))),
        out_shape=jax.ShapeDtypeStruct((rows, d), F32),
        compiler_params=_cparams(("arbitrary",)),
        name="moe_experts",
    )(tile_expert, n_valid, xs, w_gate, w_up, w_down)


def _combine_kernel(pos_ref, ys_hbm, x_ref, ga_ref, rw_ref, o_ref, buf, sem, *, tm):
    i = pl.program_id(0)
    n = pl.num_programs(0)

    def issue(tile, slot):
        def body(r, c):
            t = tile * tm + r
            pltpu.make_async_copy(ys_hbm.at[pl.ds(pos_ref[2 * t], 1)], buf.at[slot, pl.ds(r, 1)],
                                  sem.at[slot]).start()
            pltpu.make_async_copy(ys_hbm.at[pl.ds(pos_ref[2 * t + 1], 1)], buf.at[slot, pl.ds(tm + r, 1)],
                                  sem.at[slot]).start()
            return c

        lax.fori_loop(0, tm, body, 0, unroll=8)

    @pl.when(i == 0)
    def _():
        issue(0, 0)

    @pl.when(i + 1 < n)
    def _():
        issue(i + 1, (i + 1) % 2)

    slot = i % 2
    pltpu.make_async_copy(ys_hbm.at[pl.ds(0, 2 * tm)], buf.at[slot], sem.at[slot]).wait()
    w = rw_ref[...]
    y = w[:, 0:1] * buf[slot, 0:tm] + w[:, 1:2] * buf[slot, tm:2 * tm]
    o_ref[...] = x_ref[...] + ga_ref[...] * y


def _combine(pos, ys, x1, ga, rw, tm, rows_per_mod):
    n, d = x1.shape
    r = ga.shape[1]
    tiles_per_mod = rows_per_mod // tm
    return pl.pallas_call(
        functools.partial(_combine_kernel, tm=tm),
        grid_spec=pltpu.PrefetchScalarGridSpec(
            num_scalar_prefetch=1, grid=(n // tm,),
            in_specs=[pl.BlockSpec(memory_space=pl.ANY),
                      pl.BlockSpec((tm, d), lambda i, p: (i, 0)),
                      pl.BlockSpec((None, r, d), lambda i, p: (i // tiles_per_mod, 0, 0)),
                      pl.BlockSpec((tm, LANES), lambda i, p: (i, 0))],
            out_specs=pl.BlockSpec((tm, d), lambda i, p: (i, 0)),
            scratch_shapes=[pltpu.VMEM((2, 2 * tm, d), F32), pltpu.SemaphoreType.DMA((2,))]),
        out_shape=jax.ShapeDtypeStruct((n, d), F32),
        compiler_params=_cparams(("arbitrary",)),
        name="moe_combine",
    )(pos, ys, x1, ga, rw)


def _grouped_moe(big, small, w_gate, w_up, w_down, tm_e=256, tm_c=512):
    ne = w_gate.shape[0]
    n_all = big["h"].shape[0] + small["h"].shape[0]
    tm_e = min(tm_e, big["h"].shape[0])
    cnt_big = big["cnt"][0, ROUTE_LANE0:ROUTE_LANE0 + ne]
    counts = cnt_big + small["cnt"][0, ROUTE_LANE0:ROUTE_LANE0 + ne]
    tiles = (counts + tm_e - 1) // tm_e
    tile_end = jnp.cumsum(tiles)
    start_row = (tile_end - tiles) * tm_e
    n_tiles = (2 * n_all + tm_e - 1) // tm_e + ne
    tile_expert = jnp.minimum(jnp.sum(tile_end[None, :] <= jnp.arange(n_tiles)[:, None], axis=1), ne - 1)

    def positions(stream, first_rank):
        e, rank = stream["idx"][:, 0:2], stream["idx"][:, 2:4]
        return (jnp.take(start_row + first_rank, e, axis=0) + rank).reshape(-1).astype(jnp.int32)

    pos_big, pos_small = positions(big, 0), positions(small, cnt_big)
    pad = jnp.concatenate([start_row + counts, tiles * tm_e - counts, tile_end[ne - 1:]]).astype(jnp.int32)
    xs = _dispatch(jnp.concatenate([pos_big, pos_small]), pad, big["h"], small["h"], n_tiles * tm_e, tm_e, ne)
    ys = _experts(tile_expert.astype(jnp.int32), tile_end[ne - 1:].astype(jnp.int32), xs, w_gate, w_up, w_down, tm_e)
    return [_combine(pos, ys, s["x1"], s["ga"], s["rw"], min(tm_c, s["h"].shape[0]), s["rows_per_mod"])
            for s, pos in ((big, pos_big), (small, pos_small))]


def _pick_tile(n, pref):
    t = min(pref, n)
    while n % t:
        t //= 2
    return t


def _layer(x, mods, s0, attend, lw, lam_init, hg_chunk, hg_heads_per_step, per_row_mod):
    (n1, n2, w_in, gq, gk, subln, lam4, hlb, hg_g, w_out, w_route, b_route, w_gate, w_up, w_down) = lw
    b, t, d = x.shape
    n = b * t
    x2d = x.reshape(n, d)
    if per_row_mod:
        tm = n
        rows_per_mod = n
        mods = [jnp.repeat(m, t, axis=0).reshape(1, n, d) for m in mods]
    else:
        tm = _pick_tile(t, 512)
        rows_per_mod = t
        mods = [m.reshape(b, 1, d) for m in mods]
    sh1, sc1, ga1, sh2, sc2, ga2 = mods

    tm_in = tm if per_row_mod else _pick_tile(t, 1024)
    q, k, v, hmain, hk = _in_proj(x2d, n1, sc1, sh1, w_in, gq, gk, hlb, tm_in, rows_per_mod)
    a = attend(q.reshape(b, t, -1), k.reshape(b, t, -1), v.reshape(b, t, -1))
    tpad = -(-t // hg_chunk) * hg_chunk
    hm3, hk3 = hmain.reshape(b, t, -1), hk.reshape(b, t, -1)
    if tpad != t:
        hm3 = jnp.pad(hm3, ((0, 0), (0, tpad - t), (0, 0)))
        hk3 = jnp.pad(hk3, ((0, 0), (0, tpad - t), (0, 0)))
    m, s_new = _hgrn2(hm3, hk3, hg_g, s0, hg_chunk, _pick_tile(tpad, 2 * hg_chunk), hg_heads_per_step)
    m = m[:, :t]
    x1 = _out_proj(a.reshape(n, -1), m.reshape(n, -1), w_out, x2d, ga1, tm, rows_per_mod)
    hp, idx, rw, cnt = _router(x1, n2, sc2, sh2, w_route, b_route, tm, rows_per_mod)
    routed = dict(h=hp, idx=idx, rw=rw, cnt=cnt, x1=x1, ga=ga2, rows_per_mod=rows_per_mod)
    return routed, k, v, s_new


def kernel(x_prompt, x_sample, cache_k, cache_v, state_hgrn, page_table, c_prompt, c_sample, norm1_g, norm2_g, w_ada, b_ada, w_in, q_norm_g, k_norm_g, lambda_q1, lambda_k1, lambda_q2, lambda_k2, subln_g, hg_lower_bound, hg_norm_g, w_out, w_router_group, b_router_group, w_router_expert, b_router_expert, w_exp_gate, w_exp_up, w_exp_down):
    depth = norm1_g.shape[0]
    assert depth == 1, "single-layer trunk"
    l = 0
    lam_init = 0.8 - 0.6 * math.exp(-0.3 * l)
    bp, tp, d = x_prompt.shape
    bs, ts, _ = x_sample.shape

    c_all = jnp.concatenate([c_prompt, c_sample], axis=0)
    rpad = -(-c_all.shape[0] // 8) * 8
    c_all = jnp.pad(c_all, ((0, rpad - c_all.shape[0]), (0, 0)))
    mod = _adaln(c_all, w_ada[l], b_ada[l])
    mods_p = [mod[:bp, i * d:(i + 1) * d] for i in range(6)]
    mods_s = [mod[bp:bp + bs, i * d:(i + 1) * d] for i in range(6)]

    w_re = jnp.transpose(w_router_expert[l], (1, 0, 2)).reshape(d, N_EXPERTS)
    w_route = jnp.pad(jnp.concatenate([w_router_group[l], w_re], axis=1), ((0, 0), (0, LANES - N_GROUPS - N_EXPERTS)))
    b_route = jnp.pad(jnp.concatenate([b_router_group[l], b_router_expert[l].reshape(-1)]),
                      (0, LANES - N_GROUPS - N_EXPERTS)).reshape(1, LANES)
    lam4 = jnp.stack([lambda_q1[l], lambda_k1[l], lambda_q2[l], lambda_k2[l]], axis=0)

    lw = (norm1_g[l], norm2_g[l], w_in[l], q_norm_g[l], k_norm_g[l], subln_g[l], lam4, hg_lower_bound,
          hg_norm_g[l], w_out[l], w_route, b_route, w_exp_gate[l], w_exp_up[l], w_exp_down[l])

    def attend_prompt(q, k, v):
        return _prompt_attention(q, k, v, lam4, subln_g[l], lam_init)

    def attend_sample(q, k, v):
        pad = lambda a: jnp.pad(a, ((0, 0), (0, -ts % 8), (0, 0)))
        o = _sample_attention(pad(q), pad(k), pad(v), ts, cache_k[l], cache_v[l], page_table, lam4,
                              subln_g[l], lam_init)
        return o[:, :ts]

    routed_p, kp, vp, sp = _layer(x_prompt, mods_p, None, attend_prompt, lw, lam_init, 128, HG_HEADS, False)
    routed_s, kn, vn, sn = _layer(x_sample, mods_s, state_hgrn[l], attend_sample, lw, lam_init, 8, HG_HEADS, True)
    yp, ys = _grouped_moe(routed_p, routed_s, w_exp_gate[l], w_exp_up[l], w_exp_down[l])

    kv_shape = lambda b, t: (1, b, t, ATTN_KV_HEADS, 2 * QK_DIM)
    return (yp.reshape(x_prompt.shape), ys.reshape(x_sample.shape),
            kp.reshape(kv_shape(bp, tp)), vp.reshape(kv_shape(bp, tp)),
            kn.reshape(kv_shape(bs, ts)), vn.reshape(kv_shape(bs, ts)),
            sp[None], sn[None])
```

```python
import functools
import math

import jax
import jax.numpy as jnp
import numpy as np
from jax import lax
from jax.experimental import pallas as pl
from jax.experimental.pallas import tpu as pltpu

F32 = jnp.float32
BF16 = jnp.bfloat16

QK_DIM = 64
V_DIM = 128
ATTN_KV_HEADS = 4
ATTN_GROUP = 2
ATTN_HEADS = ATTN_KV_HEADS * ATTN_GROUP
HG_HEADS = 8
HG_DIM = 128
N_GROUPS = 4
EXPERTS_PER_GROUP = 4
N_EXPERTS = N_GROUPS * EXPERTS_PER_GROUP
NORM_EPS = 1e-6
ALIBI_MAX_BIAS = 8.0
LANES = 128
MXU_DIM = 256
VMEM_LIMIT = 60 * 1024 * 1024
ROUTE_LANE0 = N_GROUPS
HG_TABLE_LEVELS = 2


def _cparams(sem):
    return pltpu.CompilerParams(dimension_semantics=sem, vmem_limit_bytes=VMEM_LIMIT)


def _dot(a, b):
    return jnp.dot(a, b, preferred_element_type=F32)


def _dot_nt(a, b):
    return lax.dot_general(a, b, (((1,), (1,)), ((), ())), preferred_element_type=F32)


def _dot_tn(a, b):
    return lax.dot_general(a, b, (((0,), (0,)), ((), ())), preferred_element_type=F32)


def _split3(x):
    hi = x.astype(BF16)
    r = x - hi.astype(F32)
    mid = r.astype(BF16)
    lo = (r - mid.astype(F32)).astype(BF16)
    return hi, mid, lo


def _sigmoid(x):
    return 1.0 / (1.0 + jnp.exp(-x))


def _rms(x, g):
    return x * lax.rsqrt(jnp.mean(x * x, axis=-1, keepdims=True) + NORM_EPS) * g


def _ada_kernel(c_ref, w_ref, b_ref, o_ref):
    c = c_ref[...]
    s = c * _sigmoid(c)
    o_ref[...] = _dot(s.astype(BF16), w_ref[...].astype(BF16)) + b_ref[...]


def _adaln(c_all, w_ada, b_ada, tn=1024):
    r, d = c_all.shape
    n = w_ada.shape[1]
    return pl.pallas_call(
        _ada_kernel,
        grid=(n // tn,),
        in_specs=[
            pl.BlockSpec((r, d), lambda j: (0, 0)),
            pl.BlockSpec((d, tn), lambda j: (0, j)),
            pl.BlockSpec((1, tn), lambda j: (0, j)),
        ],
        out_specs=pl.BlockSpec((r, tn), lambda j: (0, j)),
        out_shape=jax.ShapeDtypeStruct((r, n), F32),
        compiler_params=_cparams(("arbitrary",)),
        name="adaln",
    )(c_all, w_ada, b_ada.reshape(1, n))


IN_TN = 512


def _group_sumsq(p, bd):
    x2 = p * p
    hi = x2.astype(BF16)
    lo = (x2 - hi.astype(F32)).astype(BF16)
    outs = []
    for c in range(p.shape[1] // MXU_DIM):
        sl = slice(c * MXU_DIM, (c + 1) * MXU_DIM)
        outs.append(_dot(hi[:, sl], bd) + _dot(lo[:, sl], bd))
    return jnp.concatenate(outs, axis=1)


def _in_epilogue(c, p, cs, gq_ref, gk_ref, lb_ref, bd_ref, q_ref, k_ref, v_ref, hm_ref, hk_ref):
    def qk_norm(g):
        ms = _group_sumsq(p, bd_ref[...]) * (1.0 / QK_DIM)
        return p * lax.rsqrt(ms + NORM_EPS) * g

    if c < 2:
        q_ref[:, cs] = (qk_norm(gq_ref[:, cs]) * (QK_DIM ** -0.5)).astype(BF16)
    elif c == 2:
        k_ref[:, cs] = qk_norm(gk_ref[:, cs])
    elif c == 3:
        v_ref[:, cs] = p
    elif c in (4, 5, 10, 11):
        hm_ref[:, cs] = p * _sigmoid(p)
    elif c in (6, 7):
        a = lb_ref[:, cs]
        e = jnp.exp(a - jnp.max(a, axis=0, keepdims=True))
        lb = e[0:1] / jnp.sum(e, axis=0, keepdims=True)
        sg = _sigmoid(p)
        hm_ref[:, cs] = jnp.log(lb + (1.0 - lb) * sg)
        hk_ref[:, cs] = (1.0 - lb) * (1.0 - sg)
    else:
        hm_ref[:, cs] = p


def _in_kernel(x_ref, n1_ref, sc_ref, sh_ref, w_ref, gq_ref, gk_ref, lb_ref, bd_ref,
               q_ref, k_ref, v_ref, hm_ref, hk_ref, h_scr, *, nj):
    j = pl.program_id(1)

    @pl.when(j == 0)
    def _():
        y = _rms(x_ref[...], n1_ref[...])
        h_scr[...] = (y * (1.0 + sc_ref[...]) + sh_ref[...]).astype(BF16)

    outs = (gq_ref, gk_ref, lb_ref, bd_ref, q_ref, k_ref, v_ref, hm_ref, hk_ref)
    for cols in ((0, 1), (2,), (3,), (4, 5, 10, 11), (6, 7), (8, 9)):
        cond = functools.reduce(lambda a, b: a | b, [j == c for c in cols])
        width = IN_TN if cols[0] <= 2 else MXU_DIM

        @pl.when(cond)
        def _(c=cols[0], width=width):
            for u in range(IN_TN // width):
                cs = slice(u * width, (u + 1) * width)
                p = _dot(h_scr[...], w_ref[:, cs].astype(BF16))
                _in_epilogue(c, p, cs, *outs)


def _in_proj(x2d, n1, sc, sh, w_in, gq, gk, hlb, tm, rows_per_mod):
    n, d = x2d.shape
    cols = w_in.shape[1]
    nj = cols // IN_TN
    assert nj == 12 and n % tm == 0
    r = sc.shape[1]
    tiles_per_mod = rows_per_mod // tm
    bd = jnp.asarray(np.kron(np.eye(MXU_DIM // QK_DIM), np.ones((QK_DIM, QK_DIM))), BF16)
    clamp = lambda j, lo: jnp.clip(j - lo, 0, 1)
    mod_spec = pl.BlockSpec((None, r, d), lambda i, j: (i // tiles_per_mod, 0, 0))
    return pl.pallas_call(
        functools.partial(_in_kernel, nj=nj),
        grid=(n // tm, nj),
        in_specs=[
            pl.BlockSpec((tm, d), lambda i, j: (i, 0)),
            pl.BlockSpec((1, d), lambda i, j: (0, 0)),
            mod_spec, mod_spec,
            pl.BlockSpec((d, IN_TN), lambda i, j: (0, j)),
            pl.BlockSpec((1, IN_TN), lambda i, j: (0, 0)),
            pl.BlockSpec((1, IN_TN), lambda i, j: (0, 0)),
            pl.BlockSpec((hlb.shape[0], IN_TN), lambda i, j: (0, clamp(j, 6))),
            pl.BlockSpec((MXU_DIM, MXU_DIM), lambda i, j: (0, 0)),
        ],
        out_specs=[
            pl.BlockSpec((tm, IN_TN), lambda i, j: (i, clamp(j, 0))),
            pl.BlockSpec((tm, IN_TN), lambda i, j: (i, 0)),
            pl.BlockSpec((tm, IN_TN), lambda i, j: (i, 0)),
            pl.BlockSpec((tm, IN_TN), lambda i, j: (i, jnp.clip(j - 4, 0, 7))),
            pl.BlockSpec((tm, IN_TN), lambda i, j: (i, clamp(j, 6))),
        ],
        out_shape=[
            jax.ShapeDtypeStruct((n, 2 * IN_TN), BF16),
            jax.ShapeDtypeStruct((n, IN_TN), F32),
            jax.ShapeDtypeStruct((n, IN_TN), F32),
            jax.ShapeDtypeStruct((n, 8 * IN_TN), F32),
            jax.ShapeDtypeStruct((n, 2 * IN_TN), F32),
        ],
        scratch_shapes=[pltpu.VMEM((tm, d), BF16)],
        compiler_params=_cparams(("arbitrary", "arbitrary")),
        name="in_proj",
    )(x2d, n1.reshape(1, d), sc, sh, w_in,
      jnp.tile(gq.reshape(1, QK_DIM), (1, IN_TN // QK_DIM)),
      jnp.tile(gk.reshape(1, QK_DIM), (1, IN_TN // QK_DIM)), hlb, bd)


def _alibi_slopes_np():
    h = np.arange(1, ATTN_HEADS + 1, dtype=np.float64)
    return np.exp2(-ALIBI_MAX_BIAS * h / ATTN_HEADS).reshape(ATTN_KV_HEADS, ATTN_GROUP).astype(np.float32)


def _alibi_slopes():
    out = np.zeros((ATTN_KV_HEADS, 1, LANES), np.float32)
    out[:, 0, :ATTN_GROUP] = _alibi_slopes_np()
    return jnp.asarray(out)


def _lambda(lam_ref, lam_init):
    l = lam_ref[...]
    s1 = jnp.sum(l[0:1] * l[1:2], axis=-1, keepdims=True)
    s2 = jnp.sum(l[2:3] * l[3:4], axis=-1, keepdims=True)
    return jnp.exp(s1) - jnp.exp(s2) + lam_init


def _diff_finish(o_all, t, lam, subln, lam_init, o_ref):
    for g in range(ATTN_GROUP):
        o = o_all[(2 * g) * t:(2 * g + 1) * t] - lam * o_all[(2 * g + 1) * t:(2 * g + 2) * t]
        y = _rms(o, subln) * (1.0 - lam_init)
        o_ref[:, g * LANES:(g + 1) * LANES] = y.astype(o_ref.dtype)


def _softmax_step_t(s, m_old, l_old):
    m_new = jnp.maximum(m_old, jnp.max(s, axis=0, keepdims=True))
    alpha = jnp.exp(m_old - m_new)
    p = jnp.exp(s - m_new)
    return m_new, alpha, alpha * l_old + jnp.sum(p, axis=0, keepdims=True), p.astype(BF16)


def _pattn_kernel(sl_ref, lam_ref, subln_ref, q_ref, k_ref, v_ref, o_ref, kb, vt, qs, s_a, s_b, m_scr, l_scr, acc,
                  *, tq, nblk, lam_init):
    qi = pl.program_id(2)

    @pl.when(qi == 0)
    def _():
        t = k_ref.shape[0]
        pos = lax.broadcasted_iota(jnp.int32, (t, LANES), 0)
        ln = lax.broadcasted_iota(jnp.int32, (t, LANES), 1)
        aug = jnp.where(ln == 0, pos & ~(LANES - 1), jnp.where(ln == 1, pos & (LANES - 1), 0))
        kb[:, :LANES] = k_ref[...].astype(BF16)
        kb[:, LANES:] = aug.astype(F32).astype(BF16)
        for jb in range(t // tq):
            vt[jb] = v_ref[jb * tq:(jb + 1) * tq, :].T.astype(BF16)

    rows = 4 * tq
    half = 2 * tq
    q = q_ref[...]
    lane = lax.broadcasted_iota(jnp.int32, (tq, LANES), 1)
    zero = jnp.zeros((tq, LANES), BF16)
    for g in range(ATTN_GROUP):
        qg = q[:, g * LANES:(g + 1) * LANES]
        aug = jnp.where(lane < 2, sl_ref[0:1, g:g + 1], 0.0).astype(BF16)
        qs[(2 * g) * tq:(2 * g + 1) * tq, :] = jnp.concatenate([jnp.where(lane < QK_DIM, qg, zero), aug], axis=1)
        qs[(2 * g + 1) * tq:(2 * g + 2) * tq, :] = jnp.concatenate([jnp.where(lane >= QK_DIM, qg, zero), aug], axis=1)
    t_q = lax.broadcasted_iota(jnp.int32, (1, tq), 1)
    m_scr[...] = jnp.full(m_scr.shape, -jnp.inf, F32)
    l_scr[...] = jnp.zeros(l_scr.shape, F32)
    acc[...] = jnp.zeros(acc.shape, F32)

    s_bufs = (s_a, s_b)
    groups = [slice(u * tq, (u + 1) * tq) for u in range(rows // tq)]

    def logits(jb, cs):
        s_bufs[jb % 2][:, cs] = _dot_nt(kb[jb * tq:(jb + 1) * tq, :], qs[cs, :])

    def attend(jb, cs, masked):
        s = s_bufs[jb % 2][:, cs]
        if masked:
            key = lax.broadcasted_iota(jnp.int32, (tq, 1), 0)
            s = jnp.where(key <= t_q, s, -jnp.inf)
        m_new, alpha, l_new, p = _softmax_step_t(s, m_scr[:, cs], l_scr[:, cs])
        acc[:, cs] = alpha * acc[:, cs] + _dot(vt[jb], p)
        m_scr[:, cs] = m_new
        l_scr[:, cs] = l_new

    for cs in groups:
        logits(0, cs)
    for jb in range(nblk):
        if jb + 1 < nblk:
            @pl.when(jb < qi)
            def _(jb=jb):
                for cs in groups:
                    logits(jb + 1, cs)
                    attend(jb, cs, False)

        @pl.when(jb == qi)
        def _(jb=jb):
            for cs in groups:
                attend(jb, cs, True)

    o_t = acc[...] * (1.0 / l_scr[...])
    lam = _lambda(lam_ref, lam_init)
    for g in range(ATTN_GROUP):
        d = o_t[:, (2 * g) * tq:(2 * g + 1) * tq] - lam * o_t[:, (2 * g + 1) * tq:(2 * g + 2) * tq]
        y = _rms(d.T, subln_ref[...]) * (1.0 - lam_init)
        o_ref[:, g * LANES:(g + 1) * LANES] = y.astype(o_ref.dtype)


def _prompt_attention(q, k, v, lam4, subln, lam_init, tq=512):
    b, t, _ = q.shape
    tq = min(tq, t)
    assert t % tq == 0 and t < LANES * 256
    rows = 4 * tq
    kern = functools.partial(_pattn_kernel, tq=tq, nblk=t // tq, lam_init=lam_init)
    return pl.pallas_call(
        kern,
        grid=(b, ATTN_KV_HEADS, t // tq),
        in_specs=[
            pl.BlockSpec((None, 1, LANES), lambda bi, h, qi: (h, 0, 0)),
            pl.BlockSpec((4, QK_DIM), lambda bi, h, qi: (0, 0)),
            pl.BlockSpec((1, V_DIM), lambda bi, h, qi: (0, 0)),
            pl.BlockSpec((None, tq, 2 * LANES), lambda bi, h, qi: (bi, qi, h)),
            pl.BlockSpec((None, t, LANES), lambda bi, h, qi: (bi, 0, h)),
            pl.BlockSpec((None, t, LANES), lambda bi, h, qi: (bi, 0, h)),
        ],
        out_specs=pl.BlockSpec((None, tq, 2 * LANES), lambda bi, h, qi: (bi, qi, h)),
        out_shape=jax.ShapeDtypeStruct((b, t, ATTN_HEADS * V_DIM), BF16),
        scratch_shapes=[
            pltpu.VMEM((t, 2 * LANES), BF16), pltpu.VMEM((t // tq, V_DIM, tq), BF16),
            pltpu.VMEM((rows, 2 * LANES), BF16), pltpu.VMEM((tq, rows), F32), pltpu.VMEM((tq, rows), F32),
            pltpu.VMEM((1, rows), F32), pltpu.VMEM((1, rows), F32), pltpu.VMEM((V_DIM, rows), F32),
        ],
        compiler_params=_cparams(("arbitrary", "arbitrary", "arbitrary")),
        name="prompt_attention",
    )(_alibi_slopes(), lam4, subln.reshape(1, V_DIM), q, k, v)


def _to_col(row):
    n = row.shape[1]
    eye = lax.broadcasted_iota(jnp.int32, (n, n), 0) == lax.broadcasted_iota(jnp.int32, (n, n), 1)
    return jnp.sum(jnp.where(eye, row, 0.0), axis=1, keepdims=True)


def _sattn_kernel(pt_ref, srow_ref, lam_ref, subln_ref, q_ref, kn_ref, vn_ref, *rest,
                  npg, nj, page, t, t_real, lam_init):
    del pt_ref
    kp_refs, vp_refs = rest[:npg], rest[npg:2 * npg]
    o_ref = rest[2 * npg]
    kc, vc, qf_scr, qbd_scr, m_scr, l_scr, acc = rest[2 * npg + 1:]
    j = pl.program_id(1)
    ntok = npg * page
    past = nj * ntok
    nq = qbd_scr.shape[0]
    width = ATTN_KV_HEADS * LANES
    srow = srow_ref[...]

    @pl.when(j == 0)
    def _():
        m_scr[...] = jnp.full(m_scr.shape, -jnp.inf, F32)
        l_scr[...] = jnp.zeros(l_scr.shape, F32)
        acc[...] = jnp.zeros(acc.shape, F32)
        qf = q_ref[...].astype(F32)
        lane = lax.broadcasted_iota(jnp.int32, (t, LANES), 1)
        qf_scr[...] = jnp.zeros(qf_scr.shape, F32)
        for hg in range(ATTN_HEADS):
            h = hg // ATTN_GROUP
            qh = qf[:, hg * LANES:(hg + 1) * LANES]
            qf_scr[(2 * hg) * t:(2 * hg + 1) * t, h * LANES:(h + 1) * LANES] = jnp.where(lane < QK_DIM, qh, 0.0)
            qf_scr[(2 * hg + 1) * t:(2 * hg + 2) * t, h * LANES:(h + 1) * LANES] = jnp.where(lane >= QK_DIM, qh, 0.0)
        qbd_scr[...] = qf_scr[...].astype(BF16)

    qbd = qbd_scr[...]

    for p in range(npg):
        kc[p * page:(p + 1) * page, :] = kp_refs[p][...].reshape(page, width).astype(BF16)
    kpos = lax.broadcasted_iota(jnp.int32, (ntok, nq), 0) + j * ntok
    hw = ntok // 2
    s = jnp.concatenate([_dot_nt(kc[:hw, :], qbd), _dot_nt(kc[hw:, :], qbd)], axis=0)
    s = s + kpos.astype(F32) * srow
    for p in range(npg):
        vc[p * page:(p + 1) * page, :] = vp_refs[p][...].reshape(page, width).astype(BF16)
    m_new, alpha, l_new, p = _softmax_step_t(s, m_scr[...], l_scr[...])
    pv = _dot_tn(p[:hw], vc[:hw, :]) + _dot_tn(p[hw:], vc[hw:, :])
    acc[...] = _to_col(alpha) * acc[...] + pv
    m_scr[...] = m_new
    l_scr[...] = l_new

    @pl.when(j == nj - 1)
    def _():
        t_q = lax.broadcasted_iota(jnp.int32, (1, nq), 1) % t
        c = lax.broadcasted_iota(jnp.int32, (t, 1), 0)
        sn = _dot_nt(kn_ref[...].astype(BF16), qbd) + (past + c).astype(F32) * srow
        sn = jnp.where((c <= t_q) & (c < t_real), sn, -jnp.inf)
        m_fin, alpha_n, l_fin, _ = _softmax_step_t(sn, m_scr[...], l_scr[...])
        pn = jnp.exp(sn - m_fin)
        o = _to_col(alpha_n) * acc[...]
        vn = vn_ref[...]
        for u in range(t_real):
            o = o + _to_col(pn[u:u + 1]) * vn[u:u + 1]
        o = o * _to_col(1.0 / l_fin)
        lam = _lambda(lam_ref, lam_init)
        rq = nq // ATTN_KV_HEADS
        for h in range(ATTN_KV_HEADS):
            o_h = o[h * rq:(h + 1) * rq, h * LANES:(h + 1) * LANES]
            _diff_finish(o_h, t, lam, subln_ref[...], lam_init, o_ref.at[:, 2 * h * LANES:(2 * h + 2) * LANES])


def _sample_attention(q, k_new, v_new, t_real, cache_k, cache_v, page_table, lam4, subln, lam_init, npg=32):
    b, t, _ = q.shape
    tpad = t
    page = cache_k.shape[1]
    n_pages = page_table.shape[1]
    npg = min(npg, n_pages)
    assert n_pages % npg == 0
    ntok = npg * page
    width = ATTN_KV_HEADS * LANES
    nq = ATTN_KV_HEADS * ATTN_GROUP * 2 * t
    nj = n_pages // npg
    kern = functools.partial(_sattn_kernel, npg=npg, nj=nj, page=page, t=t, t_real=t_real, lam_init=lam_init)
    slope_row = jnp.asarray(np.repeat(_alibi_slopes_np().reshape(-1), 2 * t)[None, :])

    def page_spec(p):
        return pl.BlockSpec((None, page, ATTN_KV_HEADS, LANES),
                            lambda bi, j, pt: (pt[bi * n_pages + j * npg + p], 0, 0, 0))

    grid_spec = pltpu.PrefetchScalarGridSpec(
        num_scalar_prefetch=1,
        grid=(b, nj),
        in_specs=[
            pl.BlockSpec((1, nq), lambda bi, j, pt: (0, 0)),
            pl.BlockSpec((4, QK_DIM), lambda bi, j, pt: (0, 0)),
            pl.BlockSpec((1, V_DIM), lambda bi, j, pt: (0, 0)),
            pl.BlockSpec((None, t, ATTN_HEADS * V_DIM), lambda bi, j, pt: (bi, 0, 0)),
            pl.BlockSpec((None, tpad, width), lambda bi, j, pt: (bi, 0, 0)),
            pl.BlockSpec((None, tpad, width), lambda bi, j, pt: (bi, 0, 0)),
        ] + [page_spec(p) for p in range(npg)] * 2,
        out_specs=pl.BlockSpec((None, t, ATTN_HEADS * V_DIM), lambda bi, j, pt: (bi, 0, 0)),
        scratch_shapes=[
            pltpu.VMEM((ntok, width), BF16), pltpu.VMEM((ntok, width), BF16),
            pltpu.VMEM((nq, width), F32), pltpu.VMEM((nq, width), BF16),
            pltpu.VMEM((1, nq), F32), pltpu.VMEM((1, nq), F32), pltpu.VMEM((nq, width), F32),
        ],
    )
    return pl.pallas_call(
        kern,
        grid_spec=grid_spec,
        out_shape=jax.ShapeDtypeStruct((b, t, ATTN_HEADS * V_DIM), BF16),
        compiler_params=_cparams(("arbitrary", "arbitrary")),
        name="sample_attention",
    )(page_table.reshape(-1), slope_row, lam4, subln.reshape(1, V_DIM), q, k_new, v_new,
      *([cache_k] * npg), *([cache_v] * npg))


def _hgrn_tables(c):
    levels = int(math.log2(c))
    tril = np.tril(np.ones((c, c), np.float32))
    w = [tril]
    mask = [np.eye(c, dtype=np.float32)]
    idx = np.arange(c)
    for l in range(levels):
        bs = 2 << l
        mid = (idx // bs) * bs + bs // 2
        if l < HG_TABLE_LEVELS:
            w.append(tril - tril[mid])
        upper = (idx % bs) >= bs // 2
        same = (idx[:, None] // bs) == (idx[None, :] // bs)
        mask.append((same & upper[:, None] & ~upper[None, :]).astype(np.float32))
    return jnp.asarray(np.concatenate(w, 0), BF16), jnp.asarray(np.stack(mask, 0)), levels


def _minus_mid_rows(g, bs):
    c, n = g.shape
    g3 = g.reshape(c // bs, bs, n)
    return (g3 - g3[:, bs // 2:bs // 2 + 1, :]).reshape(c, n)


def _hgrn_kernel(*refs, c, levels, nchunk, hps, has_s0):
    if has_s0:
        s0_ref, refs = refs[0], refs[1:]
    (w_ref, mask_ref, g_ref, q_ref, lf_ref, v_ref, gate_ref, k_ref, o_ref, sn_ref, st) = refs
    ci = pl.program_id(2)

    @pl.when(ci == 0)
    def _():
        for hh in range(hps):
            st[hh] = s0_ref[hh].T if has_s0 else jnp.zeros((HG_DIM, HG_DIM), F32)

    w = w_ref[...]
    pair = 2 * HG_DIM
    heads = range(hps)
    hsl = [slice(hh * HG_DIM, (hh + 1) * HG_DIM) for hh in heads]
    for n in range(nchunk):
        rs = slice(n * c, (n + 1) * c)
        gx = []
        for hh in range(0, hps, 2):
            l_hi, l_mid, l_lo = _split3(lf_ref[rs, hh * HG_DIM:hh * HG_DIM + pair])
            gx2 = _dot(w, l_hi) + _dot(w, l_mid) + _dot(w, l_lo)
            gx += [gx2[:, :HG_DIM], gx2[:, HG_DIM:]]
        qs = [q_ref[rs, hsl[hh]] for hh in heads]
        kk = [k_ref[rs, hsl[hh]] for hh in heads]
        g = [gx[hh][0:c] for hh in heads]
        a = [mask_ref[0] * _dot_nt(qs[hh].astype(BF16), kk[hh].astype(BF16)) for hh in heads]
        for l in range(levels):
            for hh in heads:
                x = gx[hh][(l + 1) * c:(l + 2) * c] if l < HG_TABLE_LEVELS else _minus_mid_rows(g[hh], 2 << l)
                e = jnp.exp(-jnp.abs(x))
                qa = (qs[hh] * e).astype(BF16)
                kb = (kk[hh] * e).astype(BF16)
                a[hh] = a[hh] + mask_ref[l + 1] * _dot_nt(qa, kb)
        vb = [v_ref[rs, hsl[hh]].astype(BF16) for hh in heads]
        st_old = [st[hh] for hh in heads]
        o = [_dot_nt((qs[hh] * jnp.exp(g[hh])).astype(BF16), st_old[hh].astype(BF16))
             + _dot(a[hh].astype(BF16), vb[hh]) for hh in heads]
        for hh in heads:
            g_end = g[hh][c - 1:c]
            kd = kk[hh] * jnp.exp(g_end - g[hh])
            st[hh] = st_old[hh] * jnp.exp(g_end) + _dot_tn(vb[hh], kd.astype(BF16))
        for hh in heads:
            o_ref[rs, hsl[hh]] = (_rms(o[hh], g_ref[...]) * gate_ref[rs, hsl[hh]]).astype(o_ref.dtype)

    @pl.when(ci == pl.num_programs(2) - 1)
    def _():
        for hh in range(hps):
            sn_ref[hh] = st[hh].T


def _hgrn2(hmain, hk, hg_norm_g, s0, c, tc, hps):
    b, t, _ = hmain.shape
    w, mask, levels = _hgrn_tables(c)
    nl = levels + 1
    has_s0 = s0 is not None
    kern = functools.partial(_hgrn_kernel, c=c, levels=levels, nchunk=tc // c, hps=hps, has_s0=has_s0)
    nhg = HG_HEADS // hps
    col = lambda off: pl.BlockSpec((None, tc, hps * HG_DIM), lambda bi, h, ci: (bi, ci, off * nhg + h))
    state_spec = pl.BlockSpec((None, hps, HG_DIM, HG_DIM), lambda bi, h, ci: (bi, h, 0, 0))
    in_specs = [
        pl.BlockSpec(w.shape, lambda bi, h, ci: (0, 0)),
        pl.BlockSpec((nl, c, c), lambda bi, h, ci: (0, 0, 0)),
        pl.BlockSpec((1, HG_DIM), lambda bi, h, ci: (0, 0)),
        col(0), col(1), col(2), col(3), col(0),
    ]
    args = [w, mask, hg_norm_g.reshape(1, HG_DIM), hmain, hmain, hmain, hmain, hk]
    if has_s0:
        in_specs, args = [state_spec] + in_specs, [s0] + args
    return pl.pallas_call(
        kern,
        grid=(b, nhg, t // tc),
        in_specs=in_specs,
        out_specs=[col(0), state_spec],
        out_shape=[jax.ShapeDtypeStruct((b, t, HG_HEADS * HG_DIM), BF16),
                   jax.ShapeDtypeStruct((b, HG_HEADS, HG_DIM, HG_DIM), F32)],
        scratch_shapes=[pltpu.VMEM((hps, HG_DIM, HG_DIM), F32)],
        compiler_params=_cparams(("arbitrary", "arbitrary", "arbitrary")),
        name="hgrn2",
    )(*args)


def _out_kernel(a_ref, m_ref, w_ref, x_ref, ga_ref, o_ref):
    half = a_ref.shape[1]
    mix = _dot(a_ref[...], w_ref[:half, :]) + _dot(m_ref[...], w_ref[half:, :])
    o_ref[...] = x_ref[...] + ga_ref[...] * mix


def _out_proj(a2d, m2d, w_out, x2d, ga, tm, rows_per_mod):
    n, d = x2d.shape
    half = a2d.shape[1]
    r = ga.shape[1]
    tiles_per_mod = rows_per_mod // tm
    return pl.pallas_call(
        _out_kernel,
        grid=(n // tm,),
        in_specs=[
            pl.BlockSpec((tm, half), lambda i: (i, 0)),
            pl.BlockSpec((tm, half), lambda i: (i, 0)),
            pl.BlockSpec((2 * half, d), lambda i: (0, 0)),
            pl.BlockSpec((tm, d), lambda i: (i, 0)),
            pl.BlockSpec((None, r, d), lambda i: (i // tiles_per_mod, 0, 0)),
        ],
        out_specs=pl.BlockSpec((tm, d), lambda i: (i, 0)),
        out_shape=jax.ShapeDtypeStruct((n, d), F32),
        compiler_params=_cparams(("arbitrary",)),
        name="out_proj",
    )(a2d, m2d, w_out.astype(BF16), x2d, ga)


def _out_router_kernel(a_ref, m_ref, w_ref, x_ref, ga_ref, n2_ref, sc_ref, sh_ref, wr_ref, br_ref, tril_ref,
                       x1_ref, h_ref, idx_ref, rw_ref, cnt_ref, cnt_scr):
    half = a_ref.shape[1]
    mix = _dot(a_ref[...], w_ref[:half, :]) + _dot(m_ref[...], w_ref[half:, :])
    x1 = x_ref[...] + ga_ref[...] * mix
    x1_ref[...] = x1
    _route_tile(x1, n2_ref, sc_ref, sh_ref, wr_ref, br_ref, tril_ref, h_ref, idx_ref, rw_ref, cnt_ref, cnt_scr)


def _route_tile(x, n2_ref, sc_ref, sh_ref, wr_ref, br_ref, tril_ref, h_ref, idx_ref, rw_ref, cnt_ref, cnt_scr):
    h = _rms(x, n2_ref[...]) * (1.0 + sc_ref[...]) + sh_ref[...]
    h_ref[...] = h
    h_hi = h.astype(BF16)
    h_lo = (h - h_hi.astype(F32)).astype(BF16)
    w = wr_ref[...]
    w_hi = w.astype(BF16)
    w_lo = (w - w_hi.astype(F32)).astype(BF16)
    parts = []
    for rs in (slice(0, h.shape[0] // 2), slice(h.shape[0] // 2, h.shape[0])):
        parts.append(_dot(h_hi[rs], w_hi) + _dot(h_hi[rs], w_lo) + _dot(h_lo[rs], w_hi))
    lg = jnp.concatenate(parts, axis=0) + br_ref[...]

    lane = lax.broadcasted_iota(jnp.int32, lg.shape, 1)
    neg = -jnp.inf
    big = jnp.int32(LANES)

    def top(valid):
        m = jnp.max(jnp.where(valid, lg, neg), axis=-1, keepdims=True)
        idx = jnp.min(jnp.where(valid & (lg == m), lane, big), axis=-1, keepdims=True)
        return m, idx

    is_group = lane < N_GROUPS
    gmax, gidx = top(is_group)
    pg_top = 1.0 / jnp.sum(jnp.where(is_group, jnp.exp(lg - gmax), 0.0), axis=-1, keepdims=True)
    lo = ROUTE_LANE0 + EXPERTS_PER_GROUP * gidx
    in_group = (lane >= lo) & (lane < lo + EXPERTS_PER_GROUP)
    l0, e0 = top(in_group)
    l1, e1 = top(in_group & (lane != e0))
    r = jnp.exp(l1 - l0)
    w0 = pg_top / (1.0 + r)
    w1 = w0 * r

    @pl.when(pl.program_id(0) == 0)
    def _():
        cnt_scr[...] = jnp.zeros(cnt_scr.shape, F32)

    is0, is1 = lane == e0, lane == e1
    onehot = jnp.where(is0 | is1, 1.0, 0.0)
    before = _dot(tril_ref[...], onehot.astype(BF16)) + cnt_scr[...]
    rank0 = jnp.sum(jnp.where(is0, before, 0.0), axis=-1, keepdims=True).astype(jnp.int32)
    rank1 = jnp.sum(jnp.where(is1, before, 0.0), axis=-1, keepdims=True).astype(jnp.int32)
    cnt_scr[...] = cnt_scr[...] + jnp.sum(onehot, axis=0, keepdims=True)
    idx_ref[...] = jnp.where(lane == 0, e0 - ROUTE_LANE0, jnp.where(lane == 1, e1 - ROUTE_LANE0,
                             jnp.where(lane == 2, rank0, jnp.where(lane == 3, rank1, 0))))
    rw_ref[...] = jnp.where(lane == 0, w0, jnp.where(lane == 1, w1, 0.0))
    cnt_ref[...] = cnt_scr[...].astype(jnp.int32)


def _out_proj_router(a2d, m2d, w_out, x2d, ga, n2, sc, sh, w_route, b_route, tm, rows_per_mod):
    n, d = x2d.shape
    half = a2d.shape[1]
    r = sc.shape[1]
    tiles_per_mod = rows_per_mod // tm
    mod_spec = pl.BlockSpec((None, r, d), lambda i: (i // tiles_per_mod, 0, 0))
    row_spec = lambda w: pl.BlockSpec((tm, w), lambda i: (i, 0))
    fixed = lambda a, b: pl.BlockSpec((a, b), lambda i: (0, 0))
    return pl.pallas_call(
        _out_router_kernel,
        grid=(n // tm,),
        in_specs=[row_spec(half), row_spec(half), fixed(2 * half, d), row_spec(d), mod_spec,
                  fixed(1, d), mod_spec, mod_spec, fixed(d, LANES), fixed(1, LANES), fixed(tm, tm)],
        out_specs=[row_spec(d), row_spec(d), row_spec(LANES), row_spec(LANES), fixed(1, LANES)],
        out_shape=[jax.ShapeDtypeStruct((n, d), F32), jax.ShapeDtypeStruct((n, d), F32),
                   jax.ShapeDtypeStruct((n, LANES), jnp.int32), jax.ShapeDtypeStruct((n, LANES), F32),
                   jax.ShapeDtypeStruct((1, LANES), jnp.int32)],
        scratch_shapes=[pltpu.VMEM((1, LANES), F32)],
        compiler_params=_cparams(("arbitrary",)),
        name="out_proj_router",
    )(a2d, m2d, w_out.astype(BF16), x2d, ga, n2.reshape(1, d), sc, sh, w_route, b_route,
      jnp.asarray(np.tril(np.ones((tm, tm), np.float32), -1), BF16))


def _dispatch_kernel(pos_ref, pad_ref, h_ref, hs_ref, xs_hbm, zbuf, sem, psem, *, tm, ne):
    i = pl.program_id(0)
    last = i == pl.num_programs(0) - 1

    def scatter(ref, first_token, count):
        def body(r, c):
            t = first_token + r
            src = ref.at[pl.ds(r, 1)]
            pltpu.make_async_copy(src, xs_hbm.at[pl.ds(pos_ref[2 * t], 1)], sem).start()
            pltpu.make_async_copy(src, xs_hbm.at[pl.ds(pos_ref[2 * t + 1], 1)], sem).start()
            return c

        lax.fori_loop(0, count, body, 0, unroll=8)

    def drain(ref):
        for _ in range(2):
            pltpu.make_async_copy(ref, xs_hbm.at[pl.ds(0, ref.shape[0])], sem).wait()

    scatter(h_ref, i * tm, tm)

    @pl.when(last)
    def _():
        scatter(hs_ref, pl.num_programs(0) * tm, hs_ref.shape[0])
        drain(hs_ref)
        zbuf[...] = jnp.zeros(zbuf.shape, F32)
        tile_rows = zbuf.shape[0]

        def pad_copy(e, r):
            return pltpu.make_async_copy(zbuf.at[pl.ds(0, 1)], xs_hbm.at[pl.ds(pad_ref[e] + r, 1)], psem)

        def tail_copy(t):
            return pltpu.make_async_copy(zbuf, xs_hbm.at[pl.ds(pl.multiple_of(t * tile_rows, tile_rows), tile_rows)], psem)

        def for_each_fill(fn):
            def per_expert(e, c):
                def per_row(r, cc):
                    fn(pad_copy(e, r))
                    return cc

                lax.fori_loop(0, pad_ref[ne + e], per_row, 0)
                return c

            def per_tile(t, c):
                fn(tail_copy(t))
                return c

            lax.fori_loop(0, ne, per_expert, 0)
            lax.fori_loop(pad_ref[2 * ne], xs_hbm.shape[0] // tile_rows, per_tile, 0)

        for_each_fill(lambda cp: cp.start())
        for_each_fill(lambda cp: cp.wait())

    drain(h_ref)


def _dispatch(pos, pad, h, h_small, n_rows, tile_rows, ne, tm=1024):
    n, w = h.shape
    tm = min(tm, n)
    assert n % tm == 0 and n_rows % tile_rows == 0
    any_spec = pl.BlockSpec(memory_space=pl.ANY)
    return pl.pallas_call(
        functools.partial(_dispatch_kernel, tm=tm, ne=ne),
        grid_spec=pltpu.PrefetchScalarGridSpec(
            num_scalar_prefetch=2, grid=(n // tm,),
            in_specs=[pl.BlockSpec((tm, w), lambda i, p, q: (i, 0)),
                      pl.BlockSpec(h_small.shape, lambda i, p, q: (0, 0))],
            out_specs=any_spec,
            scratch_shapes=[pltpu.VMEM((tile_rows, w), F32), pltpu.SemaphoreType.DMA(()),
                            pltpu.SemaphoreType.DMA(())]),
        out_shape=jax.ShapeDtypeStruct((n_rows, w), F32),
        compiler_params=_cparams(("arbitrary",)),
        name="moe_dispatch",
    )(pos, pad, h, h_small)


def _expert_kernel(te_ref, nv_ref, x_ref, wg_ref, wu_ref, wd_ref, y_ref):
    del te_ref
    i = pl.program_id(0)

    @pl.when(i < nv_ref[0])
    def _():
        x = x_ref[...].astype(BF16)
        a = _dot(x, wg_ref[...].astype(BF16))
        u = _dot(x, wu_ref[...].astype(BF16))
        hid = a * _sigmoid(a) * u
        y_ref[...] = _dot(hid.astype(BF16), wd_ref[...].astype(BF16))

    @pl.when(i >= nv_ref[0])
    def _():
        y_ref[...] = jnp.zeros(y_ref.shape, F32)


def _experts(tile_expert, n_valid, xs, w_gate, w_up, w_down, tm):
    rows, half = xs.shape
    ne, d, f = w_gate.shape
    w_spec = lambda a, b: pl.BlockSpec((None, a, b), lambda i, te, nv: (te[i], 0, 0))
    return pl.pallas_call(
        _expert_kernel,
        grid_spec=pltpu.PrefetchScalarGridSpec(
            num_scalar_prefetch=2, grid=(rows // tm,),
            in_specs=[pl.BlockSpec((tm, half), lambda i, te, nv: (jnp.minimum(i, nv[0] - 1), 0)),
                      w_spec(d, f), w_spec(d, f), w_spec(f, d)],
            out_specs=pl.BlockSpec((tm, d), lambda i, te, nv: (i, 0))),
        out_shape=jax.ShapeDtypeStruct((rows, d), F32),
        compiler_params=_cparams(("arbitrary",)),
        name="moe_experts",
    )(tile_expert, n_valid, xs, w_gate, w_up, w_down)


def _combine_kernel(pos_ref, ys_hbm, x_ref, ga_ref, rw_ref, o_ref, buf, sem, *, tm):
    i = pl.program_id(0)
    n = pl.num_programs(0)

    def issue(tile, slot):
        def body(r, c):
            t = tile * tm + r
            pltpu.make_async_copy(ys_hbm.at[pl.ds(pos_ref[2 * t], 1)], buf.at[slot, pl.ds(r, 1)],
                                  sem.at[slot]).start()
            pltpu.make_async_copy(ys_hbm.at[pl.ds(pos_ref[2 * t + 1], 1)], buf.at[slot, pl.ds(tm + r, 1)],
                                  sem.at[slot]).start()
            return c

        lax.fori_loop(0, tm, body, 0, unroll=8)

    @pl.when(i == 0)
    def _():
        issue(0, 0)

    @pl.when(i + 1 < n)
    def _():
        issue(i + 1, (i + 1) % 2)

    slot = i % 2
    pltpu.make_async_copy(ys_hbm.at[pl.ds(0, 2 * tm)], buf.at[slot], sem.at[slot]).wait()
    w = rw_ref[...]
    y = w[:, 0:1] * buf[slot, 0:tm] + w[:, 1:2] * buf[slot, tm:2 * tm]
    o_ref[...] = x_ref[...] + ga_ref[...] * y


def _combine(pos, ys, x1, ga, rw, tm, rows_per_mod):
    n, d = x1.shape
    r = ga.shape[1]
    tiles_per_mod = rows_per_mod // tm
    return pl.pallas_call(
        functools.partial(_combine_kernel, tm=tm),
        grid_spec=pltpu.PrefetchScalarGridSpec(
            num_scalar_prefetch=1, grid=(n // tm,),
            in_specs=[pl.BlockSpec(memory_space=pl.ANY),
                      pl.BlockSpec((tm, d), lambda i, p: (i, 0)),
                      pl.BlockSpec((None, r, d), lambda i, p: (i // tiles_per_mod, 0, 0)),
                      pl.BlockSpec((tm, LANES), lambda i, p: (i, 0))],
            out_specs=pl.BlockSpec((tm, d), lambda i, p: (i, 0)),
            scratch_shapes=[pltpu.VMEM((2, 2 * tm, d), F32), pltpu.SemaphoreType.DMA((2,))]),
        out_shape=jax.ShapeDtypeStruct((n, d), F32),
        compiler_params=_cparams(("arbitrary",)),
        name="moe_combine",
    )(pos, ys, x1, ga, rw)


def _grouped_moe(big, small, w_gate, w_up, w_down, tm_e=256, tm_c=512):
    ne = w_gate.shape[0]
    n_all = big["h"].shape[0] + small["h"].shape[0]
    tm_e = min(tm_e, big["h"].shape[0])
    cnt_big = big["cnt"][0, ROUTE_LANE0:ROUTE_LANE0 + ne]
    counts = cnt_big + small["cnt"][0, ROUTE_LANE0:ROUTE_LANE0 + ne]
    tiles = (counts + tm_e - 1) // tm_e
    tile_end = jnp.cumsum(tiles)
    start_row = (tile_end - tiles) * tm_e
    n_tiles = (2 * n_all + tm_e - 1) // tm_e + ne
    tile_expert = jnp.minimum(jnp.sum(tile_end[None, :] <= jnp.arange(n_tiles)[:, None], axis=1), ne - 1)

    def positions(stream, first_rank):
        e, rank = stream["idx"][:, 0:2], stream["idx"][:, 2:4]
        return (jnp.take(start_row + first_rank, e, axis=0) + rank).reshape(-1).astype(jnp.int32)

    pos_big, pos_small = positions(big, 0), positions(small, cnt_big)
    pad = jnp.concatenate([start_row + counts, tiles * tm_e - counts, tile_end[ne - 1:]]).astype(jnp.int32)
    xs = _dispatch(jnp.concatenate([pos_big, pos_small]), pad, big["h"], small["h"], n_tiles * tm_e, tm_e, ne)
    ys = _experts(tile_expert.astype(jnp.int32), tile_end[ne - 1:].astype(jnp.int32), xs, w_gate, w_up, w_down, tm_e)
    return [_combine(pos, ys, s["x1"], s["ga"], s["rw"], min(tm_c, s["h"].shape[0]), s["rows_per_mod"])
            for s, pos in ((big, pos_big), (small, pos_small))]


def _pick_tile(n, pref):
    t = min(pref, n)
    while n % t:
        t //= 2
    return t


def _layer(x, mods, s0, attend, lw, lam_init, hg_chunk, hg_heads_per_step, per_row_mod):
    (n1, n2, w_in, gq, gk, subln, lam4, hlb, hg_g, w_out, w_route, b_route, w_gate, w_up, w_down) = lw
    b, t, d = x.shape
    n = b * t
    x2d = x.reshape(n, d)
    if per_row_mod:
        tm = n
        rows_per_mod = n
        mods = [jnp.repeat(m, t, axis=0).reshape(1, n, d) for m in mods]
    else:
        tm = _pick_tile(t, 512)
        rows_per_mod = t
        mods = [m.reshape(b, 1, d) for m in mods]
    sh1, sc1, ga1, sh2, sc2, ga2 = mods

    tm_in = tm if per_row_mod else _pick_tile(t, 1024)
    q, k, v, hmain, hk = _in_proj(x2d, n1, sc1, sh1, w_in, gq, gk, hlb, tm_in, rows_per_mod)
    a = attend(q.reshape(b, t, -1), k.reshape(b, t, -1), v.reshape(b, t, -1))
    tpad = -(-t // hg_chunk) * hg_chunk
    hm3, hk3 = hmain.reshape(b, t, -1), hk.reshape(b, t, -1)
    if tpad != t:
        hm3 = jnp.pad(hm3, ((0, 0), (0, tpad - t), (0, 0)))
        hk3 = jnp.pad(hk3, ((0, 0), (0, tpad - t), (0, 0)))
    m, s_new = _hgrn2(hm3, hk3, hg_g, s0, hg_chunk, _pick_tile(tpad, 2 * hg_chunk), hg_heads_per_step)
    m = m[:, :t]
    x1, hp, idx, rw, cnt = _out_proj_router(a.reshape(n, -1), m.reshape(n, -1), w_out, x2d, ga1, n2, sc2, sh2,
                                            w_route, b_route, tm, rows_per_mod)
    routed = dict(h=hp, idx=idx, rw=rw, cnt=cnt, x1=x1, ga=ga2, rows_per_mod=rows_per_mod)
    return routed, k, v, s_new


def kernel(x_prompt, x_sample, cache_k, cache_v, state_hgrn, page_table, c_prompt, c_sample, norm1_g, norm2_g, w_ada, b_ada, w_in, q_norm_g, k_norm_g, lambda_q1, lambda_k1, lambda_q2, lambda_k2, subln_g, hg_lower_bound, hg_norm_g, w_out, w_router_group, b_router_group, w_router_expert, b_router_expert, w_exp_gate, w_exp_up, w_exp_down):
    depth = norm1_g.shape[0]
    assert depth == 1, "single-layer trunk"
    l = 0
    lam_init = 0.8 - 0.6 * math.exp(-0.3 * l)
    bp, tp, d = x_prompt.shape
    bs, ts, _ = x_sample.shape

    c_all = jnp.concatenate([c_prompt, c_sample], axis=0)
    rpad = -(-c_all.shape[0] // 8) * 8
    c_all = jnp.pad(c_all, ((0, rpad - c_all.shape[0]), (0, 0)))
    mod = _adaln(c_all, w_ada[l], b_ada[l])
    mods_p = [mod[:bp, i * d:(i + 1) * d] for i in range(6)]
    mods_s = [mod[bp:bp + bs, i * d:(i + 1) * d] for i in range(6)]

    w_re = jnp.transpose(w_router_expert[l], (1, 0, 2)).reshape(d, N_EXPERTS)
    w_route = jnp.pad(jnp.concatenate([w_router_group[l], w_re], axis=1), ((0, 0), (0, LANES - N_GROUPS - N_EXPERTS)))
    b_route = jnp.pad(jnp.concatenate([b_router_group[l], b_router_expert[l].reshape(-1)]),
                      (0, LANES - N_GROUPS - N_EXPERTS)).reshape(1, LANES)
    lam4 = jnp.stack([lambda_q1[l], lambda_k1[l], lambda_q2[l], lambda_k2[l]], axis=0)

    lw = (norm1_g[l], norm2_g[l], w_in[l], q_norm_g[l], k_norm_g[l], subln_g[l], lam4, hg_lower_bound,
          hg_norm_g[l], w_out[l], w_route, b_route, w_exp_gate[l], w_exp_up[l], w_exp_down[l])

    def attend_prompt(q, k, v):
        return _prompt_attention(q, k, v, lam4, subln_g[l], lam_init)

    def attend_sample(q, k, v):
        pad = lambda a: jnp.pad(a, ((0, 0), (0, -ts % 8), (0, 0)))
        o = _sample_attention(pad(q), pad(k), pad(v), ts, cache_k[l], cache_v[l], page_table, lam4,
                              subln_g[l], lam_init)
        return o[:, :ts]

    routed_p, kp, vp, sp = _layer(x_prompt, mods_p, None, attend_prompt, lw, lam_init, 128, HG_HEADS, False)
    routed_s, kn, vn, sn = _layer(x_sample, mods_s, state_hgrn[l], attend_sample, lw, lam_init, 8, HG_HEADS, True)
    yp, ys = _grouped_moe(routed_p, routed_s, w_exp_gate[l], w_exp_up[l], w_exp_down[l])

    kv_shape = lambda b, t: (1, b, t, ATTN_KV_HEADS, 2 * QK_DIM)
    return (yp.reshape(x_prompt.shape), ys.reshape(x_sample.shape),
            kp.reshape(kv_shape(bp, tp)), vp.reshape(kv_shape(bp, tp)),
            kn.reshape(kv_shape(bs, ts)), vn.reshape(kv_shape(bs, ts)),
            sp[None], sn[None])
```
